```python
import functools
import jax, jax.numpy as jnp
from jax import lax
import numpy as np

D_MODEL = 2048
BATCH = 4
SEQ = 2048
DEPTH = 2
DEC_BATCH = 32
DEC_SEQ = 4
PAST_LEN = 8192
PAGE_SIZE = 128

HEAD_DIM = 64
D_MIX = D_MODEL
D_ATT = D_MIX // 4
D_SSD = D_MIX // 2
D_LRU = D_MIX - D_ATT - D_SSD
H_ATT = D_ATT // HEAD_DIM
H_SSD = D_SSD // HEAD_DIM
SSD_GROUPS = 2
SSD_STATE = 128
SSD_CHUNK = 128
CONV_W = 4
SSD_CONV_DIM = D_SSD + 2 * SSD_GROUPS * SSD_STATE
LRU_BLOCKS = D_LRU // HEAD_DIM
LRU_BW = D_LRU // LRU_BLOCKS
LRU_C = 8.0
D_FF = ((8 * D_MODEL // 3 + 127) // 128) * 128
Q_BLOCK = 128
RMS_EPS = 1e-6
IN_SIZES = (D_ATT, D_ATT, D_ATT, H_ATT, D_SSD, SSD_CONV_DIM, H_SSD, D_LRU, D_LRU)
N_IN = sum(IN_SIZES)

kernel_name = "hymba_fox_ssd_rglru_macaron_step"

F32 = jnp.float32


def rmsnorm(x, g):
    xf = x.astype(F32)
    xf = xf * lax.rsqrt(jnp.mean(xf * xf, axis=-1, keepdims=True) + RMS_EPS)
    return (xf * g.astype(F32)).astype(x.dtype)


def swiglu(x, wg, wu, wd):
    return (jax.nn.silu(x @ wg) * (x @ wu)) @ wd


def causal_conv(u, hist, w, b):
    l = u.shape[1]
    full = jnp.concatenate([hist.astype(u.dtype), u], axis=1)
    out = b + full[:, 0:l] * w[0]
    for j in range(1, CONV_W):
        out = out + full[:, j:j + l] * w[j]
    return out, full[:, l:]


def fox_prompt(q, k, v, logf):
    b, s, h, d = q.shape
    nb = s // Q_BLOCK
    scale = 1.0 / np.sqrt(d)
    c_k = jnp.cumsum(logf, axis=1).transpose(0, 2, 1)
    q_blocks = q.reshape(b, nb, Q_BLOCK, h, d).swapaxes(0, 1)
    c_blocks = c_k.reshape(b, h, nb, Q_BLOCK).transpose(2, 0, 1, 3)
    k_pos = jnp.arange(s)

    def block(args):
        qi, ci, i = args
        sc = jnp.einsum('bqhd,bkhd->bhqk', qi, k, preferred_element_type=F32) * scale
        sc = sc + ci[..., :, None] - c_k[..., None, :]
        q_pos = i * Q_BLOCK + jnp.arange(Q_BLOCK)
        sc = jnp.where(k_pos[None, :] <= q_pos[:, None], sc, -jnp.inf)
        pr = jax.nn.softmax(sc, axis=-1)
        return jnp.einsum('bhqk,bkhd->bqhd', pr.astype(v.dtype), v)

    out = lax.map(block, (q_blocks, c_blocks, jnp.arange(nb)))
    return out.swapaxes(0, 1).reshape(b, s, h * d)


def fox_sample(q, k, v, logf, pk, pv, plf):
    b, t, h, d = q.shape
    n_past = pk.shape[1]
    scale = 1.0 / np.sqrt(d)
    plf = plf.astype(F32)
    suffix = lax.cumsum(plf, axis=1, reverse=True) - plf
    cn = jnp.cumsum(logf, axis=1).transpose(0, 2, 1)
    s_past = jnp.einsum('bthd,bshd->bhts', q, pk, preferred_element_type=F32) * scale
    s_past = s_past + cn[..., None] + suffix.transpose(0, 2, 1)[:, :, None, :]
    s_new = jnp.einsum('bthd,bkhd->bhtk', q, k, preferred_element_type=F32) * scale
    s_new = s_new + cn[..., :, None] - cn[..., None, :]
    s_new = jnp.where(jnp.tril(jnp.ones((t, t), bool)), s_new, -jnp.inf)
    pr = jax.nn.softmax(jnp.concatenate([s_past, s_new], axis=-1), axis=-1)
    out = (jnp.einsum('bhts,bshd->bthd', pr[..., :n_past].astype(pv.dtype), pv)
           + jnp.einsum('bhtk,bkhd->bthd', pr[..., n_past:].astype(v.dtype), v))
    return out.reshape(b, t, h * d)


def ssd_scan(x, dt, a, bm, cm, d_skip, h0, chunk):
    b, l, h, p = x.shape
    g, n = bm.shape[2], bm.shape[3]
    nc = l // chunk
    xf = x.astype(F32)
    bh = jnp.repeat(bm.astype(F32), h // g, axis=2).reshape(b, nc, chunk, h, n)
    ch = jnp.repeat(cm.astype(F32), h // g, axis=2).reshape(b, nc, chunk, h, n)
    xdt = (xf * dt[..., None]).reshape(b, nc, chunk, h, p)
    acs = jnp.cumsum((dt * a).reshape(b, nc, chunk, h).transpose(0, 3, 1, 2), axis=-1)
    tri = jnp.tril(jnp.ones((chunk, chunk), bool))
    seg = acs[..., :, None] - acs[..., None, :]
    lmat = jnp.where(tri, jnp.exp(jnp.where(tri, seg, 0.0)), 0.0)
    scores = jnp.einsum('bclhn,bcshn->bhcls', ch, bh) * lmat
    y_diag = jnp.einsum('bhcls,bcshp->bclhp', scores, xdt)
    decay_states = jnp.exp(acs[..., -1:] - acs)
    states = jnp.einsum('bclhn,bhcl,bclhp->bchpn', bh, decay_states, xdt)
    cs = jnp.cumsum(jnp.concatenate([jnp.zeros((b, h, 1), F32), acs[..., -1]], axis=-1), axis=-1)
    tri2 = jnp.tril(jnp.ones((nc + 1, nc + 1), bool))
    seg2 = cs[..., :, None] - cs[..., None, :]
    dchunk = jnp.where(tri2, jnp.exp(jnp.where(tri2, seg2, 0.0)), 0.0)
    all_states = jnp.concatenate([h0.astype(F32)[:, None], states], axis=1)
    new_states = jnp.einsum('bhzc,bchpn->bzhpn', dchunk, all_states)
    y_off = jnp.einsum('bclhn,bchpn,bhcl->bclhp', ch, new_states[:, :-1], jnp.exp(acs))
    y = (y_diag + y_off).reshape(b, l, h, p) + xf * d_skip.astype(F32)[:, None]
    return y, new_states[:, -1]


def rglru(u, w_a, b_a, w_x, b_x, lam, h0):
    b, l, dl = u.shape
    uf = u.astype(F32)
    ub = uf.reshape(b, l, LRU_BLOCKS, LRU_BW)
    r = jax.nn.sigmoid(jnp.einsum('blhi,hij->blhj', ub, w_a.astype(F32)).reshape(b, l, dl) + b_a)
    i = jax.nn.sigmoid(jnp.einsum('blhi,hij->blhj', ub, w_x.astype(F32)).reshape(b, l, dl) + b_x)
    log_a = -LRU_C * r * jax.nn.softplus(-lam.astype(F32))
    a = jnp.exp(log_a)
    inp = jnp.sqrt(-jnp.expm1(2.0 * log_a)) * (i * uf)
    inp = inp.at[:, 0].add(a[:, 0] * h0.astype(F32))

    def combine(e1, e2):
        return e1[0] * e2[0], e2[0] * e1[1] + e2[1]

    _, hs = lax.associative_scan(combine, (a, inp), axis=1)
    return hs, hs[:, -1]


def token_mix(xn, lp, attend, ssd_hist, ssd_h0, lru_hist, lru_h0, chunk):
    b, l, _ = xn.shape
    splits = np.cumsum(IN_SIZES)[:-1].tolist()
    q, k, v, f_raw, z, xbc, dt_raw, g_raw, u_raw = jnp.split(xn @ lp['w_in'], splits, axis=-1)
    q = q.reshape(b, l, H_ATT, HEAD_DIM)
    k = k.reshape(b, l, H_ATT, HEAD_DIM)
    v = v.reshape(b, l, H_ATT, HEAD_DIM)
    logf = jax.nn.log_sigmoid(f_raw.astype(F32) + lp['fox_b_f'].astype(F32))
    att = attend(q, k, v, logf)
    xbc, ssd_hist_new = causal_conv(xbc, ssd_hist, lp['ssd_conv_w'], lp['ssd_conv_b'])
    xbc = jax.nn.silu(xbc)
    xs, bm, cm = jnp.split(xbc, [D_SSD, D_SSD + SSD_GROUPS * SSD_STATE], axis=-1)
    dt = jax.nn.softplus(dt_raw.astype(F32) + lp['ssd_dt_bias'].astype(F32))
    a = -jnp.exp(lp['ssd_a_log'].astype(F32))
    y, ssd_state = ssd_scan(xs.reshape(b, l, H_SSD, HEAD_DIM), dt, a,
                            bm.reshape(b, l, SSD_GROUPS, SSD_STATE), cm.reshape(b, l, SSD_GROUPS, SSD_STATE),
                            lp['ssd_d'], ssd_h0, chunk)
    yg = (y.reshape(b, l, D_SSD) * jax.nn.silu(z.astype(F32))).reshape(b, l, SSD_GROUPS, D_SSD // SSD_GROUPS)
    yg = yg * lax.rsqrt(jnp.mean(yg * yg, axis=-1, keepdims=True) + RMS_EPS)
    ssd_out = yg.reshape(b, l, D_SSD) * lp['ssd_norm'].astype(F32)
    u, lru_hist_new = causal_conv(u_raw, lru_hist, lp['lru_conv_w'], lp['lru_conv_b'])
    hs, lru_state = rglru(u, lp['lru_w_a'], lp['lru_b_a'], lp['lru_w_x'], lp['lru_b_x'], lp['lru_lambda'], lru_h0)
    lru_out = hs * jax.nn.gelu(g_raw.astype(F32))
    merged = jnp.concatenate([att.astype(xn.dtype), ssd_out.astype(xn.dtype), lru_out.astype(xn.dtype)], axis=-1)
    return merged @ lp['w_out'], (k, v, logf, ssd_state, ssd_hist_new, lru_state, lru_hist_new)


def decoder_layer(x, lp, attend, ssd_hist, ssd_h0, lru_hist, lru_h0, chunk):
    x = x + 0.5 * swiglu(rmsnorm(x, lp['ffn1_norm']), lp['ffn1_w_gate'], lp['ffn1_w_up'], lp['ffn1_w_down'])
    m, st = token_mix(rmsnorm(x, lp['mix_norm']), lp, attend, ssd_hist, ssd_h0, lru_hist, lru_h0, chunk)
    x = x + m
    x = x + 0.5 * swiglu(rmsnorm(x, lp['ffn2_norm']), lp['ffn2_w_gate'], lp['ffn2_w_up'], lp['ffn2_w_down'])
    return x, st


def setup_inputs(seed: int = 0) -> dict:
    key = jax.random.key(seed)
    ks = iter(jax.random.split(key, 48))
    nrm = lambda shape, s=1.0: jax.random.normal(next(ks), shape, F32) * s
    n_pages = PAST_LEN // PAGE_SIZE
    n_used = DEC_BATCH * n_pages
    n_pool = n_used + n_used // 4
    page_table = jax.random.permutation(next(ks), n_pool)[:n_used].reshape(DEC_BATCH, n_pages).astype(jnp.int32)
    dt0 = jnp.exp(jax.random.uniform(next(ks), (DEPTH, H_SSD), F32, np.log(1e-3), np.log(1e-1)))
    lam_s = jax.random.uniform(next(ks), (DEPTH, D_LRU), F32, 0.9, 0.999) ** (1.0 / LRU_C)
    inp = {
        'x_prompt': nrm((BATCH, SEQ, D_MODEL)),
        'x_sample': nrm((DEC_BATCH, DEC_SEQ, D_MODEL)),
        'cache_k': nrm((DEPTH, n_pool, PAGE_SIZE, H_ATT, HEAD_DIM)),
        'cache_v': nrm((DEPTH, n_pool, PAGE_SIZE, H_ATT, HEAD_DIM)),
        'cache_logf': jax.nn.log_sigmoid(3.0 + nrm((DEPTH, n_pool, PAGE_SIZE, H_ATT))),
        'page_table': page_table,
        'state_ssd': nrm((DEPTH, DEC_BATCH, H_SSD, HEAD_DIM, SSD_STATE), 0.1),
        'state_ssd_conv': nrm((DEPTH, DEC_BATCH, CONV_W - 1, SSD_CONV_DIM)),
        'state_lru': nrm((DEPTH, DEC_BATCH, D_LRU), 0.5),
        'state_lru_conv': nrm((DEPTH, DEC_BATCH, CONV_W - 1, D_LRU)),
        'ffn1_norm': 1.0 + nrm((DEPTH, D_MODEL), 0.02),
        'ffn1_w_gate': nrm((DEPTH, D_MODEL, D_FF), D_MODEL ** -0.5),
        'ffn1_w_up': nrm((DEPTH, D_MODEL, D_FF), D_MODEL ** -0.5),
        'ffn1_w_down': nrm((DEPTH, D_FF, D_MODEL), D_FF ** -0.5),
        'mix_norm': 1.0 + nrm((DEPTH, D_MODEL), 0.02),
        'w_in': nrm((DEPTH, D_MODEL, N_IN), D_MODEL ** -0.5),
        'fox_b_f': jax.random.uniform(next(ks), (DEPTH, H_ATT), F32, 1.0, 5.0),
        'ssd_conv_w': nrm((DEPTH, CONV_W, SSD_CONV_DIM), CONV_W ** -0.5),
        'ssd_conv_b': nrm((DEPTH, SSD_CONV_DIM), 0.02),
        'ssd_dt_bias': dt0 + jnp.log(-jnp.expm1(-dt0)),
        'ssd_a_log': jnp.log(jax.random.uniform(next(ks), (DEPTH, H_SSD), F32, 1.0, 16.0)),
        'ssd_d': 1.0 + nrm((DEPTH, H_SSD), 0.1),
        'ssd_norm': 1.0 + nrm((DEPTH, D_SSD), 0.02),
        'lru_conv_w': nrm((DEPTH, CONV_W, D_LRU), CONV_W ** -0.5),
        'lru_conv_b': nrm((DEPTH, D_LRU), 0.02),
        'lru_w_a': nrm((DEPTH, LRU_BLOCKS, LRU_BW, LRU_BW), LRU_BW ** -0.5),
        'lru_b_a': nrm((DEPTH, D_LRU), 0.02),
        'lru_w_x': nrm((DEPTH, LRU_BLOCKS, LRU_BW, LRU_BW), LRU_BW ** -0.5),
        'lru_b_x': nrm((DEPTH, D_LRU), 0.02),
        'lru_lambda': jnp.log(lam_s) - jnp.log1p(-lam_s),
        'w_out': nrm((DEPTH, D_MIX, D_MODEL), D_MIX ** -0.5),
        'ffn2_norm': 1.0 + nrm((DEPTH, D_MODEL), 0.02),
        'ffn2_w_gate': nrm((DEPTH, D_MODEL, D_FF), D_MODEL ** -0.5),
        'ffn2_w_up': nrm((DEPTH, D_MODEL, D_FF), D_MODEL ** -0.5),
        'ffn2_w_down': nrm((DEPTH, D_FF, D_MODEL), D_FF ** -0.5),
        'final_norm': 1.0 + nrm((D_MODEL,), 0.02),
    }
    return inp


def reference(x_prompt, x_sample, cache_k, cache_v, cache_logf, page_table, state_ssd, state_ssd_conv,
              state_lru, state_lru_conv, ffn1_norm, ffn1_w_gate, ffn1_w_up, ffn1_w_down, mix_norm, w_in,
              fox_b_f, ssd_conv_w, ssd_conv_b, ssd_dt_bias, ssd_a_log, ssd_d, ssd_norm, lru_conv_w,
              lru_conv_b, lru_w_a, lru_b_a, lru_w_x, lru_b_x, lru_lambda, w_out, ffn2_norm, ffn2_w_gate,
              ffn2_w_up, ffn2_w_down, final_norm):
    bp, s_len, _ = x_prompt.shape
    bd, t_len, _ = x_sample.shape
    past = page_table.shape[1] * PAGE_SIZE
    yp, ys = x_prompt, x_sample
    st_p, st_s = [], []
    for l in range(DEPTH):
        lp = dict(ffn1_norm=ffn1_norm[l], ffn1_w_gate=ffn1_w_gate[l], ffn1_w_up=ffn1_w_up[l],
                  ffn1_w_down=ffn1_w_down[l], mix_norm=mix_norm[l], w_in=w_in[l], fox_b_f=fox_b_f[l],
                  ssd_conv_w=ssd_conv_w[l], ssd_conv_b=ssd_conv_b[l], ssd_dt_bias=ssd_dt_bias[l],
                  ssd_a_log=ssd_a_log[l], ssd_d=ssd_d[l], ssd_norm=ssd_norm[l], lru_conv_w=lru_conv_w[l],
                  lru_conv_b=lru_conv_b[l], lru_w_a=lru_w_a[l], lru_b_a=lru_b_a[l], lru_w_x=lru_w_x[l],
                  lru_b_x=lru_b_x[l], lru_lambda=lru_lambda[l], w_out=w_out[l], ffn2_norm=ffn2_norm[l],
                  ffn2_w_gate=ffn2_w_gate[l], ffn2_w_up=ffn2_w_up[l], ffn2_w_down=ffn2_w_down[l])
        yp, sp = decoder_layer(
            yp, lp, fox_prompt,
            jnp.zeros((bp, CONV_W - 1, SSD_CONV_DIM), yp.dtype), jnp.zeros((bp, H_SSD, HEAD_DIM, SSD_STATE), F32),
            jnp.zeros((bp, CONV_W - 1, D_LRU), yp.dtype), jnp.zeros((bp, D_LRU), F32),
            min(SSD_CHUNK, s_len))
        st_p.append(sp)
        pk = cache_k[l][page_table].reshape(bd, past, H_ATT, HEAD_DIM)
        pv = cache_v[l][page_table].reshape(bd, past, H_ATT, HEAD_DIM)
        plf = cache_logf[l][page_table].reshape(bd, past, H_ATT)
        attend_s = functools.partial(fox_sample, pk=pk, pv=pv, plf=plf)
        ys, ss = decoder_layer(ys, lp, attend_s, state_ssd_conv[l], state_ssd[l], state_lru_conv[l],
                               state_lru[l], t_len)
        st_s.append(ss)
    y_prompt = rmsnorm(yp, final_norm)
    y_sample = rmsnorm(ys, final_norm)
    k_p = jnp.stack([s[0] for s in st_p]); v_p = jnp.stack([s[1] for s in st_p]); lf_p = jnp.stack([s[2] for s in st_p])
    ssd_p = jnp.stack([s[3] for s in st_p]); ssdc_p = jnp.stack([s[4] for s in st_p])
    lru_p = jnp.stack([s[5] for s in st_p]); lruc_p = jnp.stack([s[6] for s in st_p])
    k_s = jnp.stack([s[0] for s in st_s]); v_s = jnp.stack([s[1] for s in st_s]); lf_s = jnp.stack([s[2] for s in st_s])
    ssd_s = jnp.stack([s[3] for s in st_s]); ssdc_s = jnp.stack([s[4] for s in st_s])
    lru_s = jnp.stack([s[5] for s in st_s]); lruc_s = jnp.stack([s[6] for s in st_s])
    return (y_prompt, y_sample, k_p, v_p, lf_p, k_s, v_s, lf_s, ssd_p, ssdc_p, ssd_s, ssdc_s,
            lru_p, lruc_p, lru_s, lruc_s)
```

```python
import functools
import math

import numpy as np
import jax
import jax.numpy as jnp
from jax import lax
from jax.experimental import pallas as pl
from jax.experimental.pallas import tpu as pltpu

F32 = jnp.float32
BF16 = jnp.bfloat16

D_MODEL = 2048
HEAD_DIM = 64
D_ATT = 512
D_SSD = 1024
D_LRU = 512
H_ATT = 8
H_SSD = 16
SSD_GROUPS = 2
SSD_STATE = 128
D_BC = 2 * SSD_GROUPS * SSD_STATE
CONV_W = 4
LRU_BLOCKS = 8
LRU_BW = 64
LRU_C = 8.0
D_FF = 5504
PAGE = 128
RMS_EPS = 1e-6

LANES = 128
SUBLANES = 8
VMEM_LIMIT = 56 * 1024 * 1024

CHUNK = 128
SAMPLE_ROWS = 8
D_FF_PAD = 5632
FF_TILE = 512
PAGES_PER_STEP = 8

C_Z = 0
C_XS = 1024
C_BC = 2048
C_Q = 2560
C_K = 3072
C_V = 3584
C_G = 4096
C_U = 4608
C_F = 5120
C_DT = 5248
N_PROJ = 5376
PROJ_TILE = 768


def _cparams(sem):
    return pltpu.CompilerParams(dimension_semantics=sem, vmem_limit_bytes=VMEM_LIMIT)


def _rms(x, g):
    return x * lax.rsqrt(jnp.mean(x * x, axis=-1, keepdims=True) + RMS_EPS) * g


def _silu(x):
    return x * jax.nn.sigmoid(x)


def _softplus(x):
    return jnp.maximum(x, 0.0) + jnp.log1p(jnp.exp(-jnp.abs(x)))


def _split3(a):
    a1 = a.astype(BF16)
    r1 = a - a1.astype(F32)
    a2 = r1.astype(BF16)
    r2 = r1 - a2.astype(F32)
    return a1, a2, r2.astype(BF16)


def _dot_sel_r(a, sel):
    a1, a2, a3 = _split3(a)
    d = lambda p: jnp.dot(p, sel, preferred_element_type=F32)
    return d(a1) + d(a2) + d(a3)


def _dot_sel_l(sel, a):
    a1, a2, a3 = _split3(a)
    d = lambda p: jnp.dot(sel, p, preferred_element_type=F32)
    return d(a1) + d(a2) + d(a3)


def _dot_nt(a, b):
    return lax.dot_general(a, b, (((1,), (1,)), ((), ())), preferred_element_type=F32)


def _tile_dtype(rows):
    return BF16 if rows % (2 * SUBLANES) == 0 else F32


def _iota(shape, dim):
    return lax.broadcasted_iota(jnp.int32, shape, dim)


def _tril_bf16(n):
    return (_iota((n, n), 0) >= _iota((n, n), 1)).astype(BF16)


def _ffn_kernel(*refs, n_ff, final):
    if final:
        x_ref, g_ref, wg_ref, wu_ref, wd_ref, fg_ref, o_ref, xn_ref = refs
    else:
        x_ref, g_ref, wg_ref, wu_ref, wd_ref, o_ref, xn_ref = refs
    j = pl.program_id(1)

    @pl.when(j == 0)
    def _():
        xn_ref[...] = _rms(x_ref[...], g_ref[...]).astype(BF16)
        o_ref[...] = jnp.zeros_like(o_ref)

    xn = xn_ref[...]
    a = jnp.dot(xn, wg_ref[...], preferred_element_type=F32)
    b = jnp.dot(xn, wu_ref[...], preferred_element_type=F32)
    h = (_silu(a) * b).astype(BF16)
    o_ref[...] += jnp.dot(h, wd_ref[...], preferred_element_type=F32)

    @pl.when(j == n_ff - 1)
    def _():
        y = x_ref[...] + 0.5 * o_ref[...]
        if final:
            y = _rms(y, fg_ref[...])
        o_ref[...] = y


def _ffn(x, g, wg, wu, wd, final_g, tm):
    t = x.shape[0]
    n_ff = D_FF_PAD // FF_TILE
    final = final_g is not None
    in_specs = [
        pl.BlockSpec((tm, D_MODEL), lambda i, j: (i, 0)),
        pl.BlockSpec((1, D_MODEL), lambda i, j: (0, 0)),
        pl.BlockSpec((D_MODEL, FF_TILE), lambda i, j: (0, j)),
        pl.BlockSpec((D_MODEL, FF_TILE), lambda i, j: (0, j)),
        pl.BlockSpec((FF_TILE, D_MODEL), lambda i, j: (j, 0)),
    ]
    args = [x, g, wg, wu, wd]
    if final:
        in_specs.append(pl.BlockSpec((1, D_MODEL), lambda i, j: (0, 0)))
        args.append(final_g)
    return pl.pallas_call(
        functools.partial(_ffn_kernel, n_ff=n_ff, final=final),
        grid=(t // tm, n_ff),
        in_specs=in_specs,
        out_specs=pl.BlockSpec((tm, D_MODEL), lambda i, j: (i, 0)),
        out_shape=jax.ShapeDtypeStruct((t, D_MODEL), F32),
        scratch_shapes=[pltpu.VMEM((tm, D_MODEL), BF16)],
        compiler_params=_cparams(("parallel", "arbitrary")),
        name="ffn",
    )(*args)


def _inproj_kernel(x_ref, g_ref, w_ref, o_ref, xn_ref):
    @pl.when(pl.program_id(1) == 0)
    def _():
        xn_ref[...] = _rms(x_ref[...], g_ref[...]).astype(BF16)

    o_ref[...] = jnp.dot(xn_ref[...], w_ref[...], preferred_element_type=F32)


def _inproj(x, g, w, tm):
    t = x.shape[0]
    return pl.pallas_call(
        _inproj_kernel,
        grid=(t // tm, N_PROJ // PROJ_TILE),
        in_specs=[
            pl.BlockSpec((tm, D_MODEL), lambda i, j: (i, 0)),
            pl.BlockSpec((1, D_MODEL), lambda i, j: (0, 0)),
            pl.BlockSpec((D_MODEL, PROJ_TILE), lambda i, j: (0, j)),
        ],
        out_specs=pl.BlockSpec((tm, PROJ_TILE), lambda i, j: (i, j)),
        out_shape=jax.ShapeDtypeStruct((t, N_PROJ), F32),
        scratch_shapes=[pltpu.VMEM((tm, D_MODEL), BF16)],
        compiler_params=_cparams(("parallel", "arbitrary")),
        name="inproj",
    )(x, g, w)


def _outproj_kernel(x_ref, att_ref, ssd_ref, lru_ref, w_ref, o_ref):
    acc = jnp.dot(att_ref[...], w_ref[0:D_ATT, :], preferred_element_type=F32)
    acc += jnp.dot(ssd_ref[...], w_ref[D_ATT:D_ATT + D_SSD, :], preferred_element_type=F32)
    acc += jnp.dot(lru_ref[...], w_ref[D_ATT + D_SSD:, :], preferred_element_type=F32)
    o_ref[...] = x_ref[...] + acc


def _outproj(x, att, ssd, lru, w, tm):
    t = x.shape[0]
    return pl.pallas_call(
        _outproj_kernel,
        grid=(t // tm,),
        in_specs=[
            pl.BlockSpec((tm, D_MODEL), lambda i: (i, 0)),
            pl.BlockSpec((tm, D_ATT), lambda i: (i, 0)),
            pl.BlockSpec((tm, D_SSD), lambda i: (i, 0)),
            pl.BlockSpec((tm, D_LRU), lambda i: (i, 0)),
            pl.BlockSpec((D_MODEL, D_MODEL), lambda i: (0, 0)),
        ],
        out_specs=pl.BlockSpec((tm, D_MODEL), lambda i: (i, 0)),
        out_shape=jax.ShapeDtypeStruct((t, D_MODEL), F32),
        compiler_params=_cparams(("parallel",)),
        name="outproj",
    )(x, att, ssd, lru, w)


def _fox_prep_kernel(f_ref, bf_ref, logf_ref, c_ref, ct_ref, *, n_blk):
    tril = _tril_bf16(CHUNK)
    carry = jnp.zeros((1, LANES), F32)
    for i in range(n_blk):
        rows = slice(i * CHUNK, (i + 1) * CHUNK)
        lf = -_softplus(-(f_ref[rows, :] + bf_ref[...]))
        logf_ref[rows, :] = lf
        cs = _dot_sel_l(tril, lf) + carry
        carry = cs[CHUNK - 1:CHUNK, :]
        c_ref[rows, :] = cs
        ct_ref[0, :, rows] = cs.T[0:SUBLANES, :]


def _fox_prep(proj, bf_pad, n_seq, seq):
    n_blk = seq // CHUNK
    return pl.pallas_call(
        functools.partial(_fox_prep_kernel, n_blk=n_blk),
        grid=(n_seq,),
        in_specs=[
            pl.BlockSpec((seq, LANES), lambda b: (b, C_F // LANES)),
            pl.BlockSpec((1, LANES), lambda b: (0, 0)),
        ],
        out_specs=[
            pl.BlockSpec((seq, LANES), lambda b: (b, 0)),
            pl.BlockSpec((seq, LANES), lambda b: (b, 0)),
            pl.BlockSpec((1, SUBLANES, seq), lambda b: (b, 0, 0)),
        ],
        out_shape=[
            jax.ShapeDtypeStruct((n_seq * seq, LANES), F32),
            jax.ShapeDtypeStruct((n_seq * seq, LANES), F32),
            jax.ShapeDtypeStruct((n_seq, SUBLANES, seq), F32),
        ],
        compiler_params=_cparams(("parallel",)),
        name="fox_prep",
    )(proj, bf_pad)


def _fox_prompt_kernel(q_ref, k_ref, v_ref, c_ref, ct_ref, o_ref, *, tq):
    hp = pl.program_id(1)
    qi = pl.program_id(2)
    scale = 1.0 / math.sqrt(HEAD_DIM)
    q = q_ref[...]
    c_blk = c_ref[...]
    lane = _iota((tq, LANES), 1)
    upper = lane >= HEAD_DIM
    tri = _iota((tq, tq), 0) >= _iota((tq, tq), 1)
    outs = []
    for e in range(2):
        h = 2 * hp + e
        in_head = upper if e else jnp.logical_not(upper)
        qm = (jnp.where(in_head, q, 0.0) * scale).astype(BF16)
        c_col = jnp.sum(jnp.where(lane == h, c_blk, 0.0), axis=1, keepdims=True)

        def scores(j, qm=qm, c_col=c_col, h=h):
            start = pl.multiple_of(j * tq, tq)
            kb = k_ref[pl.ds(start, tq), :].astype(BF16)
            c_row = ct_ref[0, pl.ds(h, 1), pl.ds(start, tq)]
            return _dot_nt(qm, kb) + (c_col - c_row), start

        def update(carry, s, start):
            m, l, acc = carry
            m_new = jnp.maximum(m, jnp.max(s, axis=1, keepdims=True))
            alpha = jnp.exp(m - m_new)
            p = jnp.exp(s - m_new)
            l = alpha * l + jnp.sum(p, axis=1, keepdims=True)
            vb = v_ref[pl.ds(start, tq), :].astype(BF16)
            acc = alpha * acc + jnp.dot(p.astype(BF16), vb, preferred_element_type=F32)
            return m_new, l, acc

        def body(j, carry):
            s, start = scores(j)
            return update(carry, s, start)

        init = (jnp.full((tq, 1), -1e30, F32), jnp.zeros((tq, 1), F32), jnp.zeros((tq, LANES), F32))
        carry = lax.fori_loop(0, qi, body, init)
        s, start = scores(qi)
        m, l, acc = update(carry, jnp.where(tri, s, -1e30), start)
        outs.append(acc / l)
    o_ref[...] = jnp.where(upper, outs[1], outs[0]).astype(o_ref.dtype)


def _fox_prompt(proj, c, ct, n_seq, seq, tq):
    nq = seq // tq
    nkb = seq // tq
    del nkb
    return pl.pallas_call(
        functools.partial(_fox_prompt_kernel, tq=tq),
        grid=(n_seq, H_ATT // 2, nq),
        in_specs=[
            pl.BlockSpec((tq, LANES), lambda b, hp, qi: (b * nq + qi, C_Q // LANES + hp)),
            pl.BlockSpec((seq, LANES), lambda b, hp, qi: (b, C_K // LANES + hp)),
            pl.BlockSpec((seq, LANES), lambda b, hp, qi: (b, C_V // LANES + hp)),
            pl.BlockSpec((tq, LANES), lambda b, hp, qi: (b * nq + qi, 0)),
            pl.BlockSpec((1, SUBLANES, seq), lambda b, hp, qi: (b, 0, 0)),
        ],
        out_specs=pl.BlockSpec((tq, LANES), lambda b, hp, qi: (b * nq + qi, hp)),
        out_shape=jax.ShapeDtypeStruct((n_seq * seq, D_ATT), BF16),
        compiler_params=_cparams(("parallel", "parallel", "arbitrary")),
        name="fox_prompt",
    )(proj, proj, proj, c, ct)


def _fox_sample_kernel(pt_ref, *refs, n_steps, n_valid):
    del pt_ref
    npg = PAGES_PER_STEP
    q_ref, kn_ref, vn_ref, f_ref, bf_ref = refs[0:5]
    kp_refs = refs[5:5 + npg]
    vp_refs = refs[5 + npg:5 + 2 * npg]
    lf_refs = refs[5 + 2 * npg:5 + 3 * npg]
    o_ref, logf_ref = refs[5 + 3 * npg:7 + 3 * npg]
    qbd_ref, m_ref, l_ref, acc_ref, carry_ref, cn_ref, cnt_ref, pad_ref = refs[7 + 3 * npg:]
    j = pl.program_id(1)
    rows = SAMPLE_ROWS * H_ATT
    scale = 1.0 / math.sqrt(HEAD_DIM)
    row_i = _iota((rows, LANES), 0)
    lane_i = _iota((rows, LANES), 1)

    def rep_tokens(x):
        return jnp.concatenate(
            [jnp.broadcast_to(x[t:t + 1, :], (H_ATT, x.shape[1])) for t in range(SAMPLE_ROWS)], axis=0)

    def tile_heads(x):
        return jnp.concatenate([x] * SAMPLE_ROWS, axis=0)

    @pl.when(j == 0)
    def _():
        q = q_ref[...]
        head_of_lane = _iota((rows, D_ATT), 1) // HEAD_DIM
        head_of_row = _iota((rows, D_ATT), 0) % H_ATT
        qbd_ref[...] = (jnp.where(head_of_lane == head_of_row, rep_tokens(q), 0.0) * scale).astype(BF16)
        lf = -_softplus(-(f_ref[...] + bf_ref[...]))
        logf_ref[...] = lf
        ri = _iota((SAMPLE_ROWS, LANES), 0)
        cn = lf
        sh = 1
        while sh < SAMPLE_ROWS:
            cn = cn + jnp.where(ri >= sh, pltpu.roll(cn, sh, axis=0), 0.0)
            sh *= 2
        cn_ref[...] = jnp.broadcast_to(
            jnp.sum(jnp.where(lane_i == row_i % H_ATT, rep_tokens(cn), 0.0), axis=1, keepdims=True),
            (rows, LANES))
        pad_ref[...] = jnp.zeros_like(pad_ref)
        pad_ref[0:SAMPLE_ROWS, :] = cn
        cnt_ref[...] = pad_ref[...].T[0:SUBLANES, :]
        m_ref[...] = jnp.full_like(m_ref, -1e30)
        l_ref[...] = jnp.zeros_like(l_ref)
        acc_ref[...] = jnp.zeros_like(acc_ref)
        carry_ref[...] = jnp.zeros_like(carry_ref)

    qbd = qbd_ref[...]
    cn_col = cn_ref[...][:, 0:1]
    lane8 = _iota((SUBLANES, LANES), 1)

    def online(s, v_bf16):
        m = m_ref[...][:, 0:1]
        m_new = jnp.maximum(m, jnp.max(s, axis=1, keepdims=True))
        alpha = jnp.exp(m - m_new)
        p = jnp.exp(s - m_new)
        l_ref[...] = alpha * l_ref[...] + jnp.sum(p, axis=1, keepdims=True)
        acc_ref[...] = alpha * acc_ref[...] + jnp.dot(p.astype(BF16), v_bf16, preferred_element_type=F32)
        m_ref[...] = jnp.broadcast_to(m_new, m_ref.shape)

    carry = carry_ref[...][:, 0:1]
    s_parts = []
    for p in range(npg):
        pad_ref[...] = jnp.zeros_like(pad_ref)
        pad_ref[:, 0:H_ATT] = lf_refs[p][0, 0]
        lft = pad_ref[...].T[0:SUBLANES, :]
        y = lft
        sh = 1
        while sh < PAGE:
            y = y + jnp.where(lane8 + sh < PAGE, pltpu.roll(y, PAGE - sh, axis=1), 0.0)
            sh *= 2
        suffix = (y - lft) + carry
        carry = carry + y[:, 0:1]
        kb = kp_refs[p][0, 0].astype(BF16)
        s_parts.append(_dot_nt(qbd, kb) + (tile_heads(suffix) + cn_col))
    carry_ref[...] = jnp.broadcast_to(carry, carry_ref.shape)
    s = jnp.concatenate(s_parts, axis=1)
    vb = jnp.concatenate([vp_refs[p][0, 0].astype(BF16) for p in range(npg)], axis=0)
    online(s, vb)

    @pl.when(j == n_steps - 1)
    def _():
        zk = jnp.zeros((PAGE - SAMPLE_ROWS, D_ATT), F32)
        kb = jnp.concatenate([kn_ref[...], zk], axis=0).astype(BF16)
        vb = jnp.concatenate([vn_ref[...], zk], axis=0).astype(BF16)
        s = _dot_nt(qbd, kb) + (cn_col - tile_heads(cnt_ref[...]))
        ok = (lane_i <= row_i // H_ATT) & (lane_i < n_valid)
        online(jnp.where(ok, s, -1e30), vb)
        out = acc_ref[...] / l_ref[...][:, 0:1]
        head_of_lane = _iota((H_ATT, D_ATT), 1) // HEAD_DIM
        head_of_row = _iota((H_ATT, D_ATT), 0)
        for t in range(SAMPLE_ROWS):
            blk = out[t * H_ATT:(t + 1) * H_ATT, :]
            o_ref[t:t + 1, :] = jnp.sum(jnp.where(head_of_lane == head_of_row, blk, 0.0),
                                        axis=0, keepdims=True).astype(o_ref.dtype)


def _fox_sample(proj, bf_pad, cache_k, cache_v, cache_lf, page_table, layer, row0, n_seq, n_valid):
    npg = PAGES_PER_STEP
    n_pages = page_table.shape[1]
    n_steps = n_pages // npg
    rb0 = row0 // SAMPLE_ROWS
    rows = SAMPLE_ROWS * H_ATT

    def page_map(p):
        return lambda b, j, pt: (layer, pt[b, n_pages - 1 - (j * npg + p)], 0, 0)

    in_specs = [
        pl.BlockSpec((SAMPLE_ROWS, D_ATT), lambda b, j, pt: (rb0 + b, C_Q // D_ATT)),
        pl.BlockSpec((SAMPLE_ROWS, D_ATT), lambda b, j, pt: (rb0 + b, C_K // D_ATT)),
        pl.BlockSpec((SAMPLE_ROWS, D_ATT), lambda b, j, pt: (rb0 + b, C_V // D_ATT)),
        pl.BlockSpec((SAMPLE_ROWS, LANES), lambda b, j, pt: (rb0 + b, C_F // LANES)),
        pl.BlockSpec((1, LANES), lambda b, j, pt: (0, 0)),
    ]
    in_specs += [pl.BlockSpec((1, 1, PAGE, D_ATT), page_map(p)) for p in range(npg)]
    in_specs += [pl.BlockSpec((1, 1, PAGE, D_ATT), page_map(p)) for p in range(npg)]
    in_specs += [pl.BlockSpec((1, 1, PAGE, H_ATT), page_map(p)) for p in range(npg)]
    grid_spec = pltpu.PrefetchScalarGridSpec(
        num_scalar_prefetch=1,
        grid=(n_seq, n_steps),
        in_specs=in_specs,
        out_specs=[
            pl.BlockSpec((SAMPLE_ROWS, D_ATT), lambda b, j, pt: (b, 0)),
            pl.BlockSpec((SAMPLE_ROWS, LANES), lambda b, j, pt: (b, 0)),
        ],
        scratch_shapes=[
            pltpu.VMEM((rows, D_ATT), BF16),
            pltpu.VMEM((rows, LANES), F32),
            pltpu.VMEM((rows, LANES), F32),
            pltpu.VMEM((rows, D_ATT), F32),
            pltpu.VMEM((SUBLANES, LANES), F32),
            pltpu.VMEM((rows, LANES), F32),
            pltpu.VMEM((SUBLANES, LANES), F32),
            pltpu.VMEM((PAGE, LANES), F32),
        ],
    )
    return pl.pallas_call(
        functools.partial(_fox_sample_kernel, n_steps=n_steps, n_valid=n_valid),
        grid_spec=grid_spec,
        out_shape=[
            jax.ShapeDtypeStruct((n_seq * SAMPLE_ROWS, D_ATT), F32),
            jax.ShapeDtypeStruct((n_seq * SAMPLE_ROWS, LANES), F32),
        ],
        compiler_params=_cparams(("parallel", "arbitrary")),
        name="fox_sample",
    )(page_table, proj, proj, proj, proj, bf_pad,
      *([cache_k] * npg), *([cache_v] * npg), *([cache_lf] * npg))


def _conv_from_ext(ext_ref, w_ref, b_ref, rows):
    out = b_ref[...] + ext_ref[SUBLANES:SUBLANES + rows, :] * w_ref[CONV_W - 1:CONV_W, :]
    for j in range(CONV_W - 1):
        off = SUBLANES - (CONV_W - 1) + j
        out = out + ext_ref[off:off + rows, :] * w_ref[j:j + 1, :]
    return out


def _ssd_kernel(z_ref, xs_ref, bc_ref, dt_ref, hx_ref, hbc_ref, h0_ref,
                cwx_ref, cbx_ref, cwbc_ref, cbbc_ref, dtb_ref, alog_ref, dexp_ref, nw_ref, e_ref, et_ref,
                y_ref, st_ref, extx_ref, extbc_ref, state_ref, *, blk_rows, n_valid):
    c = pl.program_id(1)
    r = CHUNK

    @pl.when(c == 0)
    def _():
        extx_ref[...] = jnp.zeros_like(extx_ref)
        extbc_ref[...] = jnp.zeros_like(extbc_ref)
        extx_ref[0:SUBLANES, :] = hx_ref[0]
        extbc_ref[0:SUBLANES, :] = hbc_ref[0]
        state_ref[...] = h0_ref[0, 0]

    extx_ref[SUBLANES:SUBLANES + blk_rows, :] = xs_ref[...]
    extbc_ref[SUBLANES:SUBLANES + blk_rows, :] = bc_ref[...]
    xs = _silu(_conv_from_ext(extx_ref, cwx_ref, cbx_ref, r))
    bc = _silu(_conv_from_ext(extbc_ref, cwbc_ref, cbbc_ref, r))
    if blk_rows == r:
        extx_ref[0:SUBLANES, :] = extx_ref[r:r + SUBLANES, :]
        extbc_ref[0:SUBLANES, :] = extbc_ref[r:r + SUBLANES, :]
        z = z_ref[...]
        dt_raw = dt_ref[...]
    else:
        z = jnp.concatenate([z_ref[...], jnp.zeros((r - blk_rows, D_SSD), F32)], axis=0)
        dt_raw = jnp.concatenate([dt_ref[...], jnp.zeros((r - blk_rows, LANES), F32)], axis=0)

    dt = _softplus(dt_raw + dtb_ref[...])
    if n_valid < r:
        dt = jnp.where(_iota((r, LANES), 0) < n_valid, dt, 0.0)
    a = -jnp.exp(alog_ref[...])
    acs = _dot_sel_l(_tril_bf16(r), dt * a)
    acs_t = acs.T
    e = e_ref[...]
    acs_x = _dot_sel_r(acs, e)
    dt_x = _dot_sel_r(dt, e)
    xdt = xs * dt_x
    xd = xdt * jnp.exp(acs_x[r - 1:r, :] - acs_x)
    state = state_ref[...]
    state_bf = state.astype(BF16)
    xd_t = xd.T.astype(BF16)
    xdt_bf = xdt.astype(BF16)
    tri = _iota((r, r), 0) >= _iota((r, r), 1)
    upper = _iota((r, LANES), 1) >= HEAD_DIM
    gw = D_SSD // SSD_GROUPS
    heads_per_group = H_SSD // SSD_GROUPS
    y_diag, y_off, new_state = [], [], []
    for g in range(SSD_GROUPS):
        bg = bc[:, g * SSD_STATE:(g + 1) * SSD_STATE].astype(BF16)
        cg = bc[:, D_BC // 2 + g * SSD_STATE:D_BC // 2 + (g + 1) * SSD_STATE].astype(BF16)
        cb = _dot_nt(cg, bg)
        y_off.append(_dot_nt(cg, state_bf[g * gw:(g + 1) * gw, :]))
        new_state.append(jnp.dot(xd_t[g * gw:(g + 1) * gw, :], bg, preferred_element_type=F32))
        for pi in range(heads_per_group // 2):
            lane0 = g * gw + pi * LANES
            pair = xdt_bf[:, lane0:lane0 + LANES]
            acc = None
            for hh in range(2):
                h = g * heads_per_group + 2 * pi + hh
                seg = acs[:, h:h + 1] - acs_t[h:h + 1, :]
                lm = jnp.where(tri, jnp.exp(jnp.where(tri, seg, 0.0)), 0.0)
                mh = (cb * lm).astype(BF16)
                in_head = upper if hh else jnp.logical_not(upper)
                d = jnp.dot(mh, jnp.where(in_head, pair, jnp.zeros_like(pair)), preferred_element_type=F32)
                acc = d if acc is None else acc + d
            y_diag.append(acc)
    y_diag = jnp.concatenate(y_diag, axis=1)
    y_off = jnp.concatenate(y_off, axis=1)
    new_state = jnp.concatenate(new_state, axis=0)
    last_col = _dot_sel_l(et_ref[...], acs_t)[:, r - 1:r]
    state_new = jnp.exp(last_col) * state + new_state
    state_ref[...] = state_new
    st_ref[0, 0] = state_new

    y = y_diag + y_off * jnp.exp(acs_x) + xs * dexp_ref[...]
    yg = y * _silu(z)
    parts = []
    for g in range(SSD_GROUPS):
        p = yg[:, g * gw:(g + 1) * gw]
        parts.append(p * lax.rsqrt(jnp.mean(p * p, axis=-1, keepdims=True) + RMS_EPS))
    out = jnp.concatenate(parts, axis=1) * nw_ref[...]
    y_ref[...] = out[0:blk_rows, :].astype(y_ref.dtype)


def _ssd(proj, hist_x, hist_bc, h0, layer, p, row0, n_seq, n_chunks, blk_rows, n_valid):
    rb0 = row0 // blk_rows
    row_map = lambda col: (lambda b, c: (rb0 + b * n_chunks + c, col))
    const = lambda b, c: (0, 0)
    return pl.pallas_call(
        functools.partial(_ssd_kernel, blk_rows=blk_rows, n_valid=n_valid),
        grid=(n_seq, n_chunks),
        in_specs=[
            pl.BlockSpec((blk_rows, D_SSD), row_map(C_Z // D_SSD)),
            pl.BlockSpec((blk_rows, D_SSD), row_map(C_XS // D_SSD)),
            pl.BlockSpec((blk_rows, D_BC), row_map(C_BC // D_BC)),
            pl.BlockSpec((blk_rows, LANES), row_map(C_DT // LANES)),
            pl.BlockSpec((1, SUBLANES, D_SSD), lambda b, c: (b, 0, 0)),
            pl.BlockSpec((1, SUBLANES, D_BC), lambda b, c: (b, 0, 0)),
            pl.BlockSpec((1, 1, D_SSD, SSD_STATE), lambda b, c: (layer, b, 0, 0)),
            pl.BlockSpec((CONV_W, D_SSD), const),
            pl.BlockSpec((1, D_SSD), const),
            pl.BlockSpec((CONV_W, D_BC), const),
            pl.BlockSpec((1, D_BC), const),
            pl.BlockSpec((1, LANES), const),
            pl.BlockSpec((1, LANES), const),
            pl.BlockSpec((1, D_SSD), const),
            pl.BlockSpec((1, D_SSD), const),
            pl.BlockSpec((LANES, D_SSD), const),
            pl.BlockSpec((D_SSD, LANES), const),
        ],
        out_specs=[
            pl.BlockSpec((blk_rows, D_SSD), lambda b, c: (b * n_chunks + c, 0)),
            pl.BlockSpec((1, 1, D_SSD, SSD_STATE), lambda b, c: (0, b, 0, 0)),
        ],
        out_shape=[
            jax.ShapeDtypeStruct((n_seq * n_chunks * blk_rows, D_SSD), _tile_dtype(blk_rows)),
            jax.ShapeDtypeStruct((1, n_seq, D_SSD, SSD_STATE), F32),
        ],
        scratch_shapes=[
            pltpu.VMEM((SUBLANES + CHUNK, D_SSD), F32),
            pltpu.VMEM((SUBLANES + CHUNK, D_BC), F32),
            pltpu.VMEM((D_SSD, SSD_STATE), F32),
        ],
        compiler_params=_cparams(("parallel", "arbitrary")),
        name="ssd",
    )(proj, proj, proj, proj, hist_x, hist_bc, h0,
      p["cwx"], p["cbx"], p["cwbc"], p["cbbc"], p["dtb"], p["alog"], p["dexp"], p["nw"], p["e"], p["et"])


def _lru_kernel(g_ref, u_ref, hist_ref, h0_ref, cw_ref, cb_ref, wa_ref, ba_ref, wx_ref, bx_ref, lam_ref,
                y_ref, st_ref, ext_ref, hprev_ref, *, blk_rows, n_valid):
    c = pl.program_id(1)
    r = blk_rows

    @pl.when(c == 0)
    def _():
        ext_ref[0:SUBLANES, :] = hist_ref[0]
        hprev_ref[...] = h0_ref[0]

    ext_ref[SUBLANES:SUBLANES + r, :] = u_ref[...]
    u = _conv_from_ext(ext_ref, cw_ref, cb_ref, r)
    ext_ref[0:SUBLANES, :] = ext_ref[r:r + SUBLANES, :]
    ub = u.astype(BF16)
    rg = jax.nn.sigmoid(jnp.dot(ub, wa_ref[...], preferred_element_type=F32) + ba_ref[...])
    ig = jax.nn.sigmoid(jnp.dot(ub, wx_ref[...], preferred_element_type=F32) + bx_ref[...])
    log_a = -LRU_C * rg * _softplus(-lam_ref[...])
    a = jnp.exp(log_a)
    x = jnp.sqrt(-jnp.tanh(log_a) * (a * a + 1.0)) * (ig * u)
    row = _iota((r, D_LRU), 0)
    sh = 1
    while sh < r:
        keep = row >= sh
        a_prev = jnp.where(keep, pltpu.roll(a, sh, axis=0), 1.0)
        x_prev = jnp.where(keep, pltpu.roll(x, sh, axis=0), 0.0)
        x = x + a * x_prev
        a = a * a_prev
        sh *= 2
    hs = x + a * hprev_ref[...]
    hprev_ref[...] = hs[r - 1:r, :]
    st_ref[0] = hs[n_valid - 1:n_valid, :]
    y_ref[...] = (hs * jax.nn.gelu(g_ref[...])).astype(y_ref.dtype)


def _lru(proj, hist, h0, p, row0, n_seq, n_chunks, blk_rows, n_valid):
    rb0 = row0 // blk_rows
    row_map = lambda col: (lambda b, c: (rb0 + b * n_chunks + c, col))
    const = lambda b, c: (0, 0)
    return pl.pallas_call(
        functools.partial(_lru_kernel, blk_rows=blk_rows, n_valid=n_valid),
        grid=(n_seq, n_chunks),
        in_specs=[
            pl.BlockSpec((blk_rows, D_LRU), row_map(C_G // D_LRU)),
            pl.BlockSpec((blk_rows, D_LRU), row_map(C_U // D_LRU)),
            pl.BlockSpec((1, SUBLANES, D_LRU), lambda b, c: (b, 0, 0)),
            pl.BlockSpec((1, 1, D_LRU), lambda b, c: (b, 0, 0)),
            pl.BlockSpec((CONV_W, D_LRU), const),
            pl.BlockSpec((1, D_LRU), const),
            pl.BlockSpec((D_LRU, D_LRU), const),
            pl.BlockSpec((1, D_LRU), const),
            pl.BlockSpec((D_LRU, D_LRU), const),
            pl.BlockSpec((1, D_LRU), const),
            pl.BlockSpec((1, D_LRU), const),
        ],
        out_specs=[
            pl.BlockSpec((blk_rows, D_LRU), lambda b, c: (b * n_chunks + c, 0)),
            pl.BlockSpec((1, 1, D_LRU), lambda b, c: (b, 0, 0)),
        ],
        out_shape=[
            jax.ShapeDtypeStruct((n_seq * n_chunks * blk_rows, D_LRU), _tile_dtype(blk_rows)),
            jax.ShapeDtypeStruct((n_seq, 1, D_LRU), F32),
        ],
        scratch_shapes=[
            pltpu.VMEM((SUBLANES + blk_rows, D_LRU), F32),
            pltpu.VMEM((1, D_LRU), F32),
        ],
        compiler_params=_cparams(("parallel", "arbitrary")),
        name="lru",
    )(proj, proj, hist, h0, p["cw"], p["cb"], p["wa"], p["ba"], p["wx"], p["bx"], p["lam"])


def _pad_lanes(v, n=LANES):
    v = v.reshape(1, -1).astype(F32)
    return jnp.pad(v, ((0, 0), (0, n - v.shape[1])))


def _hist_tile(h):
    return jnp.pad(h, ((0, 0), (SUBLANES - (CONV_W - 1), 0), (0, 0)))


def _block_diag(w):
    eye = jnp.eye(LRU_BLOCKS, dtype=w.dtype)
    return jnp.einsum("hij,hg->higj", w, eye).reshape(D_LRU, D_LRU)


def _pick_tile(t, cap):
    best = SUBLANES
    for tm in range(SUBLANES, cap + 1, SUBLANES):
        if t % tm == 0:
            best = tm
    return best


def kernel(x_prompt, x_sample, cache_k, cache_v, cache_logf, page_table, state_ssd, state_ssd_conv, state_lru, state_lru_conv, ffn1_norm, ffn1_w_gate, ffn1_w_up, ffn1_w_down, mix_norm, w_in, fox_b_f, ssd_conv_w, ssd_conv_b, ssd_dt_bias, ssd_a_log, ssd_d, ssd_norm, lru_conv_w, lru_conv_b, lru_w_a, lru_b_a, lru_w_x, lru_b_x, lru_lambda, w_out, ffn2_norm, ffn2_w_gate, ffn2_w_up, ffn2_w_down, final_norm):
    bp, seq, _ = x_prompt.shape
    bd, t_len, _ = x_sample.shape
    depth = w_in.shape[0]
    n_pool = cache_k.shape[1]
    assert seq % CHUNK == 0 and t_len <= SAMPLE_ROWS and t_len >= CONV_W - 1
    assert page_table.shape[1] % PAGES_PER_STEP == 0
    tp = bp * seq
    ts = bd * SAMPLE_ROWS
    t = tp + ts
    tm_ffn = _pick_tile(t, 768)
    tm_proj = _pick_tile(t, 1056)
    tq = 256 if seq % 256 == 0 else CHUNK

    xs_pad = jnp.pad(x_sample, ((0, 0), (0, SAMPLE_ROWS - t_len), (0, 0)))
    x = jnp.concatenate([x_prompt.reshape(tp, D_MODEL), xs_pad.reshape(ts, D_MODEL)], axis=0)

    ck = cache_k.reshape(depth, n_pool, PAGE, D_ATT)
    cv = cache_v.reshape(depth, n_pool, PAGE, D_ATT)
    st_ssd = state_ssd.reshape(depth, bd, D_SSD, SSD_STATE)
    zeros_ssd = jnp.zeros((1, bp, D_SSD, SSD_STATE), F32)
    e_mat = (jnp.arange(LANES)[:, None] == (jnp.arange(D_SSD)[None, :] // HEAD_DIM)).astype(BF16)
    row = lambda v: v.reshape(1, -1).astype(F32)
    pad_ff = D_FF_PAD - D_FF

    def ffn_weights(wg, wu, wd):
        wg = jnp.pad(wg.astype(BF16), ((0, 0), (0, pad_ff)))
        wu = jnp.pad(wu.astype(BF16), ((0, 0), (0, pad_ff)))
        wd = jnp.pad(wd.astype(BF16), ((0, pad_ff), (0, 0)))
        return wg, wu, wd

    outs_p = [[] for _ in range(7)]
    outs_s = [[] for _ in range(7)]
    for l in range(depth):
        last = l == depth - 1
        x = _ffn(x, row(ffn1_norm[l]), *ffn_weights(ffn1_w_gate[l], ffn1_w_up[l], ffn1_w_down[l]), None, tm_ffn)

        w = w_in[l]
        o_f, o_z, o_xbc = 3 * D_ATT, 3 * D_ATT + H_ATT, 3 * D_ATT + H_ATT + D_SSD
        o_dt = o_xbc + D_SSD + D_BC
        o_g = o_dt + H_SSD
        w_p = jnp.concatenate([
            w[:, o_z:o_z + D_SSD], w[:, o_xbc:o_xbc + D_SSD + D_BC], w[:, 0:3 * D_ATT],
            w[:, o_g:o_g + 2 * D_LRU],
            jnp.pad(w[:, o_f:o_f + H_ATT], ((0, 0), (0, LANES - H_ATT))),
            jnp.pad(w[:, o_dt:o_dt + H_SSD], ((0, 0), (0, LANES - H_SSD))),
        ], axis=1).astype(BF16)
        proj = _inproj(x, row(mix_norm[l]), w_p, tm_proj)

        bf_pad = _pad_lanes(fox_b_f[l])
        ssd_p = dict(
            cwx=ssd_conv_w[l][:, :D_SSD], cbx=row(ssd_conv_b[l][:D_SSD]),
            cwbc=ssd_conv_w[l][:, D_SSD:], cbbc=row(ssd_conv_b[l][D_SSD:]),
            dtb=_pad_lanes(ssd_dt_bias[l]), alog=_pad_lanes(ssd_a_log[l]),
            dexp=row(jnp.repeat(ssd_d[l], HEAD_DIM)), nw=row(ssd_norm[l]), e=e_mat, et=e_mat.T)
        lru_p = dict(
            cw=lru_conv_w[l], cb=row(lru_conv_b[l]),
            wa=_block_diag(lru_w_a[l]).astype(BF16), ba=row(lru_b_a[l]),
            wx=_block_diag(lru_w_x[l]).astype(BF16), bx=row(lru_b_x[l]), lam=row(lru_lambda[l]))

        logf_p, c_p, ct_p = _fox_prep(proj, bf_pad, bp, seq)
        att_p = _fox_prompt(proj, c_p, ct_p, bp, seq, tq)
        ssd_y_p, ssd_st_p = _ssd(proj, jnp.zeros((bp, SUBLANES, D_SSD), F32), jnp.zeros((bp, SUBLANES, D_BC), F32),
                                 zeros_ssd, 0, ssd_p, 0, bp, seq // CHUNK, CHUNK, CHUNK)
        lru_y_p, lru_st_p = _lru(proj, jnp.zeros((bp, SUBLANES, D_LRU), F32), jnp.zeros((bp, 1, D_LRU), F32),
                                 lru_p, 0, bp, seq // CHUNK, CHUNK, CHUNK)

        att_s, logf_s = _fox_sample(proj, bf_pad, ck, cv, cache_logf, page_table, l, tp, bd, t_len)
        hist = _hist_tile(state_ssd_conv[l])
        ssd_y_s, ssd_st_s = _ssd(proj, hist[:, :, :D_SSD], hist[:, :, D_SSD:], st_ssd, l, ssd_p,
                                 tp, bd, 1, SAMPLE_ROWS, t_len)
        lru_y_s, lru_st_s = _lru(proj, _hist_tile(state_lru_conv[l]), state_lru[l].reshape(bd, 1, D_LRU),
                                 lru_p, tp, bd, 1, SAMPLE_ROWS, t_len)

        att = jnp.concatenate([att_p, att_s.astype(BF16)], axis=0)
        ssd_y = jnp.concatenate([ssd_y_p, ssd_y_s.astype(BF16)], axis=0)
        lru_y = jnp.concatenate([lru_y_p, lru_y_s.astype(BF16)], axis=0)
        x = _outproj(x, att, ssd_y, lru_y, w_out[l].astype(BF16), tm_ffn)
        x = _ffn(x, row(ffn2_norm[l]), *ffn_weights(ffn2_w_gate[l], ffn2_w_up[l], ffn2_w_down[l]),
                 row(final_norm) if last else None, tm_ffn)

        pp = proj[:tp].reshape(bp, seq, N_PROJ)
        ps = proj[tp:].reshape(bd, SAMPLE_ROWS, N_PROJ)
        outs_p[0].append(pp[:, :, C_K:C_K + D_ATT].reshape(bp, seq, H_ATT, HEAD_DIM))
        outs_p[1].append(pp[:, :, C_V:C_V + D_ATT].reshape(bp, seq, H_ATT, HEAD_DIM))
        outs_p[2].append(logf_p[:, :H_ATT].reshape(bp, seq, H_ATT))
        outs_p[3].append(ssd_st_p.reshape(bp, H_SSD, HEAD_DIM, SSD_STATE))
        outs_p[4].append(pp[:, seq - (CONV_W - 1):, C_XS:C_XS + D_SSD + D_BC])
        outs_p[5].append(lru_st_p.reshape(bp, D_LRU))
        outs_p[6].append(pp[:, seq - (CONV_W - 1):, C_U:C_U + D_LRU])
        outs_s[0].append(ps[:, :t_len, C_K:C_K + D_ATT].reshape(bd, t_len, H_ATT, HEAD_DIM))
        outs_s[1].append(ps[:, :t_len, C_V:C_V + D_ATT].reshape(bd, t_len, H_ATT, HEAD_DIM))
        outs_s[2].append(logf_s.reshape(bd, SAMPLE_ROWS, LANES)[:, :t_len, :H_ATT])
        outs_s[3].append(ssd_st_s.reshape(bd, H_SSD, HEAD_DIM, SSD_STATE))
        outs_s[4].append(ps[:, t_len - (CONV_W - 1):t_len, C_XS:C_XS + D_SSD + D_BC])
        outs_s[5].append(lru_st_s.reshape(bd, D_LRU))
        outs_s[6].append(ps[:, t_len - (CONV_W - 1):t_len, C_U:C_U + D_LRU])

    y_prompt = x[:tp].reshape(bp, seq, D_MODEL)
    y_sample = x[tp:].reshape(bd, SAMPLE_ROWS, D_MODEL)[:, :t_len]
    sp = [jnp.stack(o) for o in outs_p]
    ss = [jnp.stack(o) for o in outs_s]
    return (y_prompt, y_sample, sp[0], sp[1], sp[2], ss[0], ss[1], ss[2], sp[3], sp[4], ss[3], ss[4],
            sp[5], sp[6], ss[5], ss[6])
```

```python
import functools
import math

import numpy as np
import jax
import jax.numpy as jnp
from jax import lax
from jax.experimental import pallas as pl
from jax.experimental.pallas import tpu as pltpu

F32 = jnp.float32
BF16 = jnp.bfloat16

D_MODEL = 2048
HEAD_DIM = 64
D_ATT = 512
D_SSD = 1024
D_LRU = 512
H_ATT = 8
H_SSD = 16
SSD_GROUPS = 2
SSD_STATE = 128
D_BC = 2 * SSD_GROUPS * SSD_STATE
CONV_W = 4
LRU_BLOCKS = 8
LRU_BW = 64
LRU_C = 8.0
D_FF = 5504
PAGE = 128
RMS_EPS = 1e-6

LANES = 128
SUBLANES = 8
VMEM_LIMIT = 56 * 1024 * 1024

CHUNK = 128
SAMPLE_ROWS = 8
D_FF_PAD = 5632
FF_TILE = 512
PAGES_PER_STEP = 8

C_Z = 0
C_XS = 1024
C_BC = 2048
C_Q = 2560
C_K = 3072
C_V = 3584
C_G = 4096
C_U = 4608
C_F = 5120
C_DT = 5248
N_PROJ = 5376
PROJ_TILE = 768


def _cparams(sem):
    return pltpu.CompilerParams(dimension_semantics=sem, vmem_limit_bytes=VMEM_LIMIT)


def _rms(x, g):
    return x * lax.rsqrt(jnp.mean(x * x, axis=-1, keepdims=True) + RMS_EPS) * g


def _silu(x):
    return x * jax.nn.sigmoid(x)


def _softplus(x):
    return jnp.maximum(x, 0.0) + jnp.log1p(jnp.exp(-jnp.abs(x)))


def _split3(a):
    a1 = a.astype(BF16)
    r1 = a - a1.astype(F32)
    a2 = r1.astype(BF16)
    r2 = r1 - a2.astype(F32)
    return a1, a2, r2.astype(BF16)


def _dot_sel_r(a, sel):
    a1, a2, a3 = _split3(a)
    d = lambda p: jnp.dot(p, sel, preferred_element_type=F32)
    return d(a1) + d(a2) + d(a3)


def _dot_sel_l(sel, a):
    a1, a2, a3 = _split3(a)
    d = lambda p: jnp.dot(sel, p, preferred_element_type=F32)
    return d(a1) + d(a2) + d(a3)


def _dot_nt(a, b):
    return lax.dot_general(a, b, (((1,), (1,)), ((), ())), preferred_element_type=F32)


def _tile_dtype(rows):
    return BF16 if rows % (2 * SUBLANES) == 0 else F32


def _iota(shape, dim):
    return lax.broadcasted_iota(jnp.int32, shape, dim)


def _tril_bf16(n):
    return (_iota((n, n), 0) >= _iota((n, n), 1)).astype(BF16)


def _ffn_kernel(*refs, n_ff, final):
    if final:
        x_ref, g_ref, wg_ref, wu_ref, wd_ref, fg_ref, o_ref, xn_ref = refs
    else:
        x_ref, g_ref, wg_ref, wu_ref, wd_ref, o_ref, xn_ref = refs
    j = pl.program_id(1)

    @pl.when(j == 0)
    def _():
        xn_ref[...] = _rms(x_ref[...], g_ref[...]).astype(BF16)
        o_ref[...] = jnp.zeros_like(o_ref)

    xn = xn_ref[...]
    a = jnp.dot(xn, wg_ref[...], preferred_element_type=F32)
    b = jnp.dot(xn, wu_ref[...], preferred_element_type=F32)
    h = (_silu(a) * b).astype(BF16)
    o_ref[...] += jnp.dot(h, wd_ref[...], preferred_element_type=F32)

    @pl.when(j == n_ff - 1)
    def _():
        y = x_ref[...] + 0.5 * o_ref[...]
        if final:
            y = _rms(y, fg_ref[...])
        o_ref[...] = y


def _ffn(x, g, wg, wu, wd, final_g, tm):
    t = x.shape[0]
    n_ff = D_FF_PAD // FF_TILE
    final = final_g is not None
    in_specs = [
        pl.BlockSpec((tm, D_MODEL), lambda i, j: (i, 0)),
        pl.BlockSpec((1, D_MODEL), lambda i, j: (0, 0)),
        pl.BlockSpec((D_MODEL, FF_TILE), lambda i, j: (0, j)),
        pl.BlockSpec((D_MODEL, FF_TILE), lambda i, j: (0, j)),
        pl.BlockSpec((FF_TILE, D_MODEL), lambda i, j: (j, 0)),
    ]
    args = [x, g, wg, wu, wd]
    if final:
        in_specs.append(pl.BlockSpec((1, D_MODEL), lambda i, j: (0, 0)))
        args.append(final_g)
    return pl.pallas_call(
        functools.partial(_ffn_kernel, n_ff=n_ff, final=final),
        grid=(t // tm, n_ff),
        in_specs=in_specs,
        out_specs=pl.BlockSpec((tm, D_MODEL), lambda i, j: (i, 0)),
        out_shape=jax.ShapeDtypeStruct((t, D_MODEL), F32),
        scratch_shapes=[pltpu.VMEM((tm, D_MODEL), BF16)],
        compiler_params=_cparams(("parallel", "arbitrary")),
        name="ffn",
    )(*args)


def _inproj_kernel(x_ref, g_ref, w_ref, o_ref, xn_ref):
    @pl.when(pl.program_id(1) == 0)
    def _():
        xn_ref[...] = _rms(x_ref[...], g_ref[...]).astype(BF16)

    o_ref[...] = jnp.dot(xn_ref[...], w_ref[...], preferred_element_type=F32)


def _inproj(x, g, w, tm):
    t = x.shape[0]
    return pl.pallas_call(
        _inproj_kernel,
        grid=(t // tm, N_PROJ // PROJ_TILE),
        in_specs=[
            pl.BlockSpec((tm, D_MODEL), lambda i, j: (i, 0)),
            pl.BlockSpec((1, D_MODEL), lambda i, j: (0, 0)),
            pl.BlockSpec((D_MODEL, PROJ_TILE), lambda i, j: (0, j)),
        ],
        out_specs=pl.BlockSpec((tm, PROJ_TILE), lambda i, j: (i, j)),
        out_shape=jax.ShapeDtypeStruct((t, N_PROJ), F32),
        scratch_shapes=[pltpu.VMEM((tm, D_MODEL), BF16)],
        compiler_params=_cparams(("parallel", "arbitrary")),
        name="inproj",
    )(x, g, w)


def _outproj_kernel(x_ref, att_ref, ssd_ref, lru_ref, w_ref, o_ref):
    acc = jnp.dot(att_ref[...], w_ref[0:D_ATT, :], preferred_element_type=F32)
    acc += jnp.dot(ssd_ref[...], w_ref[D_ATT:D_ATT + D_SSD, :], preferred_element_type=F32)
    acc += jnp.dot(lru_ref[...], w_ref[D_ATT + D_SSD:, :], preferred_element_type=F32)
    o_ref[...] = x_ref[...] + acc


def _outproj(x, att, ssd, lru, w, tm):
    t = x.shape[0]
    return pl.pallas_call(
        _outproj_kernel,
        grid=(t // tm,),
        in_specs=[
            pl.BlockSpec((tm, D_MODEL), lambda i: (i, 0)),
            pl.BlockSpec((tm, D_ATT), lambda i: (i, 0)),
            pl.BlockSpec((tm, D_SSD), lambda i: (i, 0)),
            pl.BlockSpec((tm, D_LRU), lambda i: (i, 0)),
            pl.BlockSpec((D_MODEL, D_MODEL), lambda i: (0, 0)),
        ],
        out_specs=pl.BlockSpec((tm, D_MODEL), lambda i: (i, 0)),
        out_shape=jax.ShapeDtypeStruct((t, D_MODEL), F32),
        compiler_params=_cparams(("parallel",)),
        name="outproj",
    )(x, att, ssd, lru, w)


def _fox_prep_kernel(f_ref, bf_ref, logf_ref, c_ref, ct_ref, *, n_blk):
    tril = _tril_bf16(CHUNK)
    carry = jnp.zeros((1, LANES), F32)
    for i in range(n_blk):
        rows = slice(i * CHUNK, (i + 1) * CHUNK)
        lf = -_softplus(-(f_ref[rows, :] + bf_ref[...]))
        logf_ref[rows, :] = lf
        cs = _dot_sel_l(tril, lf) + carry
        carry = cs[CHUNK - 1:CHUNK, :]
        c_ref[rows, :] = cs
        ct_ref[0, :, rows] = cs.T[0:SUBLANES, :]


def _fox_prep(proj, bf_pad, n_seq, seq):
    n_blk = seq // CHUNK
    return pl.pallas_call(
        functools.partial(_fox_prep_kernel, n_blk=n_blk),
        grid=(n_seq,),
        in_specs=[
            pl.BlockSpec((seq, LANES), lambda b: (b, C_F // LANES)),
            pl.BlockSpec((1, LANES), lambda b: (0, 0)),
        ],
        out_specs=[
            pl.BlockSpec((seq, LANES), lambda b: (b, 0)),
            pl.BlockSpec((seq, LANES), lambda b: (b, 0)),
            pl.BlockSpec((1, SUBLANES, seq), lambda b: (b, 0, 0)),
        ],
        out_shape=[
            jax.ShapeDtypeStruct((n_seq * seq, LANES), F32),
            jax.ShapeDtypeStruct((n_seq * seq, LANES), F32),
            jax.ShapeDtypeStruct((n_seq, SUBLANES, seq), F32),
        ],
        compiler_params=_cparams(("parallel",)),
        name="fox_prep",
    )(proj, bf_pad)


def _fox_prompt_kernel(q_ref, k_ref, v_ref, c_ref, ct_ref, o_ref, *, tq):
    hp = pl.program_id(1)
    qi = pl.program_id(2)
    scale = 1.0 / math.sqrt(HEAD_DIM)
    q = q_ref[...]
    c_blk = c_ref[...]
    lane = _iota((tq, LANES), 1)
    upper = lane >= HEAD_DIM
    tri = _iota((tq, tq), 0) >= _iota((tq, tq), 1)
    heads = []
    for e in range(2):
        h = 2 * hp + e
        in_head = upper if e else jnp.logical_not(upper)
        qm = (jnp.where(in_head, q, 0.0) * scale).astype(BF16)
        c_col = jnp.sum(jnp.where(lane == h, c_blk, 0.0), axis=1, keepdims=True)
        heads.append((h, in_head, qm, c_col))

    def step(j, carry, masked):
        start = pl.multiple_of(j * tq, tq)
        kb = k_ref[pl.ds(start, tq), :].astype(BF16)
        v = v_ref[pl.ds(start, tq), :]
        new = []
        for (h, in_head, qm, c_col), (m, acc) in zip(heads, carry):
            c_row = ct_ref[0, pl.ds(h, 1), pl.ds(start, tq)]
            s = _dot_nt(qm, kb) + (c_col - c_row)
            if masked:
                s = jnp.where(tri, s, -1e30)
            m_new = jnp.maximum(m, jnp.max(s, axis=1, keepdims=True))
            p = jnp.exp(s - m_new).astype(BF16)
            ve = jnp.where(in_head, v, 1.0).astype(BF16)
            acc = jnp.exp(m - m_new) * acc + jnp.dot(p, ve, preferred_element_type=F32)
            new.append((m_new, acc))
        return tuple(new)

    init = tuple((jnp.full((tq, 1), -1e30, F32), jnp.zeros((tq, LANES), F32)) for _ in heads)
    carry = lax.fori_loop(0, qi, lambda j, cr: step(j, cr, False), init)
    (_, acc0), (_, acc1) = step(qi, carry, True)
    out0 = acc0 / acc0[:, HEAD_DIM:HEAD_DIM + 1]
    out1 = acc1 / acc1[:, 0:1]
    o_ref[...] = jnp.where(upper, out1, out0).astype(o_ref.dtype)


def _fox_prompt(proj, c, ct, n_seq, seq, tq):
    nq = seq // tq
    nkb = seq // tq
    del nkb
    return pl.pallas_call(
        functools.partial(_fox_prompt_kernel, tq=tq),
        grid=(n_seq, H_ATT // 2, nq),
        in_specs=[
            pl.BlockSpec((tq, LANES), lambda b, hp, qi: (b * nq + qi, C_Q // LANES + hp)),
            pl.BlockSpec((seq, LANES), lambda b, hp, qi: (b, C_K // LANES + hp)),
            pl.BlockSpec((seq, LANES), lambda b, hp, qi: (b, C_V // LANES + hp)),
            pl.BlockSpec((tq, LANES), lambda b, hp, qi: (b * nq + qi, 0)),
            pl.BlockSpec((1, SUBLANES, seq), lambda b, hp, qi: (b, 0, 0)),
        ],
        out_specs=pl.BlockSpec((tq, LANES), lambda b, hp, qi: (b * nq + qi, hp)),
        out_shape=jax.ShapeDtypeStruct((n_seq * seq, D_ATT), BF16),
        compiler_params=_cparams(("parallel", "parallel", "arbitrary")),
        name="fox_prompt",
    )(proj, proj, proj, c, ct)


def _fox_sample_kernel(pt_ref, *refs, n_steps, n_valid):
    del pt_ref
    npg = PAGES_PER_STEP
    q_ref, kn_ref, vn_ref, f_ref, bf_ref = refs[0:5]
    kp_refs = refs[5:5 + npg]
    vp_refs = refs[5 + npg:5 + 2 * npg]
    lf_refs = refs[5 + 2 * npg:5 + 3 * npg]
    o_ref, logf_ref = refs[5 + 3 * npg:7 + 3 * npg]
    qbd_ref, m_ref, l_ref, acc_ref, carry_ref, cn_ref, cnt_ref, pad_ref = refs[7 + 3 * npg:]
    j = pl.program_id(1)
    rows = SAMPLE_ROWS * H_ATT
    scale = 1.0 / math.sqrt(HEAD_DIM)
    row_i = _iota((rows, LANES), 0)
    lane_i = _iota((rows, LANES), 1)

    def rep_tokens(x):
        return jnp.concatenate(
            [jnp.broadcast_to(x[t:t + 1, :], (H_ATT, x.shape[1])) for t in range(SAMPLE_ROWS)], axis=0)

    def tile_heads(x):
        return jnp.concatenate([x] * SAMPLE_ROWS, axis=0)

    @pl.when(j == 0)
    def _():
        q = q_ref[...]
        head_of_lane = _iota((rows, D_ATT), 1) // HEAD_DIM
        head_of_row = _iota((rows, D_ATT), 0) % H_ATT
        qbd_ref[...] = (jnp.where(head_of_lane == head_of_row, rep_tokens(q), 0.0) * scale).astype(BF16)
        lf = -_softplus(-(f_ref[...] + bf_ref[...]))
        logf_ref[...] = lf
        ri = _iota((SAMPLE_ROWS, LANES), 0)
        cn = lf
        sh = 1
        while sh < SAMPLE_ROWS:
            cn = cn + jnp.where(ri >= sh, pltpu.roll(cn, sh, axis=0), 0.0)
            sh *= 2
        cn_ref[...] = jnp.broadcast_to(
            jnp.sum(jnp.where(lane_i == row_i % H_ATT, rep_tokens(cn), 0.0), axis=1, keepdims=True),
            (rows, LANES))
        pad_ref[...] = jnp.zeros_like(pad_ref)
        pad_ref[0:SAMPLE_ROWS, :] = cn
        cnt_ref[...] = pad_ref[...].T[0:SUBLANES, :]
        m_ref[...] = jnp.full_like(m_ref, -1e30)
        l_ref[...] = jnp.zeros_like(l_ref)
        acc_ref[...] = jnp.zeros_like(acc_ref)
        carry_ref[...] = jnp.zeros_like(carry_ref)

    qbd = qbd_ref[...]
    cn_col = cn_ref[...][:, 0:1]
    lane8 = _iota((SUBLANES, LANES), 1)

    def online(s, v_bf16, v_transposed):
        m = m_ref[...][:, 0:1]
        m_new = jnp.maximum(m, jnp.max(s, axis=1, keepdims=True))
        alpha = jnp.exp(m - m_new)
        p = jnp.exp(s - m_new)
        l_ref[...] = alpha * l_ref[...] + jnp.sum(p, axis=1, keepdims=True)
        pb = p.astype(BF16)
        pv = _dot_nt(pb, v_bf16) if v_transposed else jnp.dot(pb, v_bf16, preferred_element_type=F32)
        acc_ref[...] = alpha * acc_ref[...] + pv
        m_ref[...] = jnp.broadcast_to(m_new, m_ref.shape)

    carry = carry_ref[...][:, 0:1]
    s_parts = []
    for p in range(npg):
        lft = lf_refs[p][0, 0]
        y = lft
        sh = 1
        while sh < PAGE:
            y = y + jnp.where(lane8 + sh < PAGE, pltpu.roll(y, PAGE - sh, axis=1), 0.0)
            sh *= 2
        suffix = (y - lft) + carry
        carry = carry + y[:, 0:1]
        kt = kp_refs[p][0, 0].astype(BF16)
        s_parts.append(jnp.dot(qbd, kt, preferred_element_type=F32) + (tile_heads(suffix) + cn_col))
    carry_ref[...] = jnp.broadcast_to(carry, carry_ref.shape)
    s = jnp.concatenate(s_parts, axis=1)
    vt = jnp.concatenate([vp_refs[p][0, 0].astype(BF16) for p in range(npg)], axis=1)
    online(s, vt, True)

    @pl.when(j == n_steps - 1)
    def _():
        zk = jnp.zeros((PAGE - SAMPLE_ROWS, D_ATT), F32)
        kb = jnp.concatenate([kn_ref[...], zk], axis=0).astype(BF16)
        vb = jnp.concatenate([vn_ref[...], zk], axis=0).astype(BF16)
        s = _dot_nt(qbd, kb) + (cn_col - tile_heads(cnt_ref[...]))
        ok = (lane_i <= row_i // H_ATT) & (lane_i < n_valid)
        online(jnp.where(ok, s, -1e30), vb, False)
        out = acc_ref[...] / l_ref[...][:, 0:1]
        head_of_lane = _iota((H_ATT, D_ATT), 1) // HEAD_DIM
        head_of_row = _iota((H_ATT, D_ATT), 0)
        for t in range(SAMPLE_ROWS):
            blk = out[t * H_ATT:(t + 1) * H_ATT, :]
            o_ref[t:t + 1, :] = jnp.sum(jnp.where(head_of_lane == head_of_row, blk, 0.0),
                                        axis=0, keepdims=True).astype(o_ref.dtype)


def _fox_sample(proj, bf_pad, cache_k, cache_v, cache_lf, page_table, layer, row0, n_seq, n_valid):
    npg = PAGES_PER_STEP
    n_pages = page_table.shape[1]
    n_steps = n_pages // npg
    rb0 = row0 // SAMPLE_ROWS
    rows = SAMPLE_ROWS * H_ATT

    def page_map(p):
        return lambda b, j, pt: (layer, pt[b, n_pages - 1 - (j * npg + p)], 0, 0)

    in_specs = [
        pl.BlockSpec((SAMPLE_ROWS, D_ATT), lambda b, j, pt: (rb0 + b, C_Q // D_ATT)),
        pl.BlockSpec((SAMPLE_ROWS, D_ATT), lambda b, j, pt: (rb0 + b, C_K // D_ATT)),
        pl.BlockSpec((SAMPLE_ROWS, D_ATT), lambda b, j, pt: (rb0 + b, C_V // D_ATT)),
        pl.BlockSpec((SAMPLE_ROWS, LANES), lambda b, j, pt: (rb0 + b, C_F // LANES)),
        pl.BlockSpec((1, LANES), lambda b, j, pt: (0, 0)),
    ]
    in_specs += [pl.BlockSpec((1, 1, D_ATT, PAGE), page_map(p)) for p in range(npg)]
    in_specs += [pl.BlockSpec((1, 1, D_ATT, PAGE), page_map(p)) for p in range(npg)]
    in_specs += [pl.BlockSpec((1, 1, H_ATT, PAGE), page_map(p)) for p in range(npg)]
    grid_spec = pltpu.PrefetchScalarGridSpec(
        num_scalar_prefetch=1,
        grid=(n_seq, n_steps),
        in_specs=in_specs,
        out_specs=[
            pl.BlockSpec((SAMPLE_ROWS, D_ATT), lambda b, j, pt: (b, 0)),
            pl.BlockSpec((SAMPLE_ROWS, LANES), lambda b, j, pt: (b, 0)),
        ],
        scratch_shapes=[
            pltpu.VMEM((rows, D_ATT), BF16),
            pltpu.VMEM((rows, LANES), F32),
            pltpu.VMEM((rows, LANES), F32),
            pltpu.VMEM((rows, D_ATT), F32),
            pltpu.VMEM((SUBLANES, LANES), F32),
            pltpu.VMEM((rows, LANES), F32),
            pltpu.VMEM((SUBLANES, LANES), F32),
            pltpu.VMEM((PAGE, LANES), F32),
        ],
    )
    return pl.pallas_call(
        functools.partial(_fox_sample_kernel, n_steps=n_steps, n_valid=n_valid),
        grid_spec=grid_spec,
        out_shape=[
            jax.ShapeDtypeStruct((n_seq * SAMPLE_ROWS, D_ATT), F32),
            jax.ShapeDtypeStruct((n_seq * SAMPLE_ROWS, LANES), F32),
        ],
        compiler_params=_cparams(("parallel", "arbitrary")),
        name="fox_sample",
    )(page_table, proj, proj, proj, proj, bf_pad,
      *([cache_k] * npg), *([cache_v] * npg), *([cache_lf] * npg))


def _conv_from_ext(ext_ref, w_ref, b_ref, rows):
    out = b_ref[...] + ext_ref[SUBLANES:SUBLANES + rows, :] * w_ref[CONV_W - 1:CONV_W, :]
    for j in range(CONV_W - 1):
        off = SUBLANES - (CONV_W - 1) + j
        out = out + ext_ref[off:off + rows, :] * w_ref[j:j + 1, :]
    return out


def _ssd_kernel(z_ref, xs_ref, bc_ref, dt_ref, hx_ref, hbc_ref, h0_ref,
                cwx_ref, cbx_ref, cwbc_ref, cbbc_ref, dtb_ref, alog_ref, dexp_ref, nw_ref, e_ref, et_ref,
                y_ref, st_ref, extx_ref, extbc_ref, state_ref, *, blk_rows, n_valid):
    c = pl.program_id(1)
    r = CHUNK

    @pl.when(c == 0)
    def _():
        extx_ref[...] = jnp.zeros_like(extx_ref)
        extbc_ref[...] = jnp.zeros_like(extbc_ref)
        extx_ref[0:SUBLANES, :] = hx_ref[0]
        extbc_ref[0:SUBLANES, :] = hbc_ref[0]
        state_ref[...] = h0_ref[0, 0]

    extx_ref[SUBLANES:SUBLANES + blk_rows, :] = xs_ref[...]
    extbc_ref[SUBLANES:SUBLANES + blk_rows, :] = bc_ref[...]
    xs = _silu(_conv_from_ext(extx_ref, cwx_ref, cbx_ref, r))
    bc = _silu(_conv_from_ext(extbc_ref, cwbc_ref, cbbc_ref, r))
    if blk_rows == r:
        extx_ref[0:SUBLANES, :] = extx_ref[r:r + SUBLANES, :]
        extbc_ref[0:SUBLANES, :] = extbc_ref[r:r + SUBLANES, :]
        z = z_ref[...]
        dt_raw = dt_ref[...]
    else:
        z = jnp.concatenate([z_ref[...], jnp.zeros((r - blk_rows, D_SSD), F32)], axis=0)
        dt_raw = jnp.concatenate([dt_ref[...], jnp.zeros((r - blk_rows, LANES), F32)], axis=0)

    dt = _softplus(dt_raw + dtb_ref[...])
    if n_valid < r:
        dt = jnp.where(_iota((r, LANES), 0) < n_valid, dt, 0.0)
    a = -jnp.exp(alog_ref[...])
    acs = _dot_sel_l(_tril_bf16(r), dt * a)
    acs_t = acs.T
    e = e_ref[...]
    acs_x = _dot_sel_r(acs, e)
    dt_x = _dot_sel_r(dt, e)
    xdt = xs * dt_x
    xd = xdt * jnp.exp(acs_x[r - 1:r, :] - acs_x)
    state = state_ref[...]
    state_bf = state.astype(BF16)
    xd_t = xd.T.astype(BF16)
    xdt_bf = xdt.astype(BF16)
    tri = _iota((r, r), 0) >= _iota((r, r), 1)
    upper = _iota((r, LANES), 1) >= HEAD_DIM
    gw = D_SSD // SSD_GROUPS
    heads_per_group = H_SSD // SSD_GROUPS
    y_diag, y_off, new_state = [], [], []
    for g in range(SSD_GROUPS):
        bg = bc[:, g * SSD_STATE:(g + 1) * SSD_STATE].astype(BF16)
        cg = bc[:, D_BC // 2 + g * SSD_STATE:D_BC // 2 + (g + 1) * SSD_STATE].astype(BF16)
        cb = _dot_nt(cg, bg)
        y_off.append(_dot_nt(cg, state_bf[g * gw:(g + 1) * gw, :]))
        new_state.append(jnp.dot(xd_t[g * gw:(g + 1) * gw, :], bg, preferred_element_type=F32))
        for pi in range(heads_per_group // 2):
            lane0 = g * gw + pi * LANES
            pair = xdt_bf[:, lane0:lane0 + LANES]
            acc = None
            for hh in range(2):
                h = g * heads_per_group + 2 * pi + hh
                seg = acs[:, h:h + 1] - acs_t[h:h + 1, :]
                lm = jnp.where(tri, jnp.exp(jnp.where(tri, seg, 0.0)), 0.0)
                mh = (cb * lm).astype(BF16)
                in_head = upper if hh else jnp.logical_not(upper)
                d = jnp.dot(mh, jnp.where(in_head, pair, jnp.zeros_like(pair)), preferred_element_type=F32)
                acc = d if acc is None else acc + d
            y_diag.append(acc)
    y_diag = jnp.concatenate(y_diag, axis=1)
    y_off = jnp.concatenate(y_off, axis=1)
    new_state = jnp.concatenate(new_state, axis=0)
    last_col = _dot_sel_l(et_ref[...], acs_t)[:, r - 1:r]
    state_new = jnp.exp(last_col) * state + new_state
    state_ref[...] = state_new
    st_ref[0, 0] = state_new

    y = y_diag + y_off * jnp.exp(acs_x) + xs * dexp_ref[...]
    yg = y * _silu(z)
    parts = []
    for g in range(SSD_GROUPS):
        p = yg[:, g * gw:(g + 1) * gw]
        parts.append(p * lax.rsqrt(jnp.mean(p * p, axis=-1, keepdims=True) + RMS_EPS))
    out = jnp.concatenate(parts, axis=1) * nw_ref[...]
    y_ref[...] = out[0:blk_rows, :].astype(y_ref.dtype)


def _ssd(proj, hist_x, hist_bc, h0, layer, p, row0, n_seq, n_chunks, blk_rows, n_valid):
    rb0 = row0 // blk_rows
    row_map = lambda col: (lambda b, c: (rb0 + b * n_chunks + c, col))
    const = lambda b, c: (0, 0)
    return pl.pallas_call(
        functools.partial(_ssd_kernel, blk_rows=blk_rows, n_valid=n_valid),
        grid=(n_seq, n_chunks),
        in_specs=[
            pl.BlockSpec((blk_rows, D_SSD), row_map(C_Z // D_SSD)),
            pl.BlockSpec((blk_rows, D_SSD), row_map(C_XS // D_SSD)),
            pl.BlockSpec((blk_rows, D_BC), row_map(C_BC // D_BC)),
            pl.BlockSpec((blk_rows, LANES), row_map(C_DT // LANES)),
            pl.BlockSpec((1, SUBLANES, D_SSD), lambda b, c: (b, 0, 0)),
            pl.BlockSpec((1, SUBLANES, D_BC), lambda b, c: (b, 0, 0)),
            pl.BlockSpec((1, 1, D_SSD, SSD_STATE), lambda b, c: (layer, b, 0, 0)),
            pl.BlockSpec((CONV_W, D_SSD), const),
            pl.BlockSpec((1, D_SSD), const),
            pl.BlockSpec((CONV_W, D_BC), const),
            pl.BlockSpec((1, D_BC), const),
            pl.BlockSpec((1, LANES), const),
            pl.BlockSpec((1, LANES), const),
            pl.BlockSpec((1, D_SSD), const),
            pl.BlockSpec((1, D_SSD), const),
            pl.BlockSpec((LANES, D_SSD), const),
            pl.BlockSpec((D_SSD, LANES), const),
        ],
        out_specs=[
            pl.BlockSpec((blk_rows, D_SSD), lambda b, c: (b * n_chunks + c, 0)),
            pl.BlockSpec((1, 1, D_SSD, SSD_STATE), lambda b, c: (0, b, 0, 0)),
        ],
        out_shape=[
            jax.ShapeDtypeStruct((n_seq * n_chunks * blk_rows, D_SSD), _tile_dtype(blk_rows)),
            jax.ShapeDtypeStruct((1, n_seq, D_SSD, SSD_STATE), F32),
        ],
        scratch_shapes=[
            pltpu.VMEM((SUBLANES + CHUNK, D_SSD), F32),
            pltpu.VMEM((SUBLANES + CHUNK, D_BC), F32),
            pltpu.VMEM((D_SSD, SSD_STATE), F32),
        ],
        compiler_params=_cparams(("parallel", "arbitrary")),
        name="ssd",
    )(proj, proj, proj, proj, hist_x, hist_bc, h0,
      p["cwx"], p["cbx"], p["cwbc"], p["cbbc"], p["dtb"], p["alog"], p["dexp"], p["nw"], p["e"], p["et"])


def _lru_kernel(g_ref, u_ref, hist_ref, h0_ref, cw_ref, cb_ref, wa_ref, ba_ref, wx_ref, bx_ref, lam_ref,
                y_ref, st_ref, ext_ref, hprev_ref, *, blk_rows, n_valid):
    c = pl.program_id(1)
    r = blk_rows

    @pl.when(c == 0)
    def _():
        ext_ref[0:SUBLANES, :] = hist_ref[0]
        hprev_ref[...] = h0_ref[0]

    ext_ref[SUBLANES:SUBLANES + r, :] = u_ref[...]
    u = _conv_from_ext(ext_ref, cw_ref, cb_ref, r)
    ext_ref[0:SUBLANES, :] = ext_ref[r:r + SUBLANES, :]
    ub = u.astype(BF16)
    rg = jax.nn.sigmoid(jnp.dot(ub, wa_ref[...], preferred_element_type=F32) + ba_ref[...])
    ig = jax.nn.sigmoid(jnp.dot(ub, wx_ref[...], preferred_element_type=F32) + bx_ref[...])
    log_a = -LRU_C * rg * _softplus(-lam_ref[...])
    a = jnp.exp(log_a)
    x = jnp.sqrt(-jnp.tanh(log_a) * (a * a + 1.0)) * (ig * u)
    row = _iota((r, D_LRU), 0)
    sh = 1
    while sh < r:
        keep = row >= sh
        a_prev = jnp.where(keep, pltpu.roll(a, sh, axis=0), 1.0)
        x_prev = jnp.where(keep, pltpu.roll(x, sh, axis=0), 0.0)
        x = x + a * x_prev
        a = a * a_prev
        sh *= 2
    hs = x + a * hprev_ref[...]
    hprev_ref[...] = hs[r - 1:r, :]
    st_ref[0] = hs[n_valid - 1:n_valid, :]
    y_ref[...] = (hs * jax.nn.gelu(g_ref[...])).astype(y_ref.dtype)


def _lru(proj, hist, h0, p, row0, n_seq, n_chunks, blk_rows, n_valid):
    rb0 = row0 // blk_rows
    row_map = lambda col: (lambda b, c: (rb0 + b * n_chunks + c, col))
    const = lambda b, c: (0, 0)
    return pl.pallas_call(
        functools.partial(_lru_kernel, blk_rows=blk_rows, n_valid=n_valid),
        grid=(n_seq, n_chunks),
        in_specs=[
            pl.BlockSpec((blk_rows, D_LRU), row_map(C_G // D_LRU)),
            pl.BlockSpec((blk_rows, D_LRU), row_map(C_U // D_LRU)),
            pl.BlockSpec((1, SUBLANES, D_LRU), lambda b, c: (b, 0, 0)),
            pl.BlockSpec((1, 1, D_LRU), lambda b, c: (b, 0, 0)),
            pl.BlockSpec((CONV_W, D_LRU), const),
            pl.BlockSpec((1, D_LRU), const),
            pl.BlockSpec((D_LRU, D_LRU), const),
            pl.BlockSpec((1, D_LRU), const),
            pl.BlockSpec((D_LRU, D_LRU), const),
            pl.BlockSpec((1, D_LRU), const),
            pl.BlockSpec((1, D_LRU), const),
        ],
        out_specs=[
            pl.BlockSpec((blk_rows, D_LRU), lambda b, c: (b * n_chunks + c, 0)),
            pl.BlockSpec((1, 1, D_LRU), lambda b, c: (b, 0, 0)),
        ],
        out_shape=[
            jax.ShapeDtypeStruct((n_seq * n_chunks * blk_rows, D_LRU), _tile_dtype(blk_rows)),
            jax.ShapeDtypeStruct((n_seq, 1, D_LRU), F32),
        ],
        scratch_shapes=[
            pltpu.VMEM((SUBLANES + blk_rows, D_LRU), F32),
            pltpu.VMEM((1, D_LRU), F32),
        ],
        compiler_params=_cparams(("parallel", "arbitrary")),
        name="lru",
    )(proj, proj, hist, h0, p["cw"], p["cb"], p["wa"], p["ba"], p["wx"], p["bx"], p["lam"])


def _pad_lanes(v, n=LANES):
    v = v.reshape(1, -1).astype(F32)
    return jnp.pad(v, ((0, 0), (0, n - v.shape[1])))


def _hist_tile(h):
    return jnp.pad(h, ((0, 0), (SUBLANES - (CONV_W - 1), 0), (0, 0)))


def _block_diag(w):
    eye = jnp.eye(LRU_BLOCKS, dtype=w.dtype)
    return jnp.einsum("hij,hg->higj", w, eye).reshape(D_LRU, D_LRU)


def _pick_tile(t, cap):
    best = SUBLANES
    for tm in range(SUBLANES, cap + 1, SUBLANES):
        if t % tm == 0:
            best = tm
    return best


def kernel(x_prompt, x_sample, cache_k, cache_v, cache_logf, page_table, state_ssd, state_ssd_conv, state_lru, state_lru_conv, ffn1_norm, ffn1_w_gate, ffn1_w_up, ffn1_w_down, mix_norm, w_in, fox_b_f, ssd_conv_w, ssd_conv_b, ssd_dt_bias, ssd_a_log, ssd_d, ssd_norm, lru_conv_w, lru_conv_b, lru_w_a, lru_b_a, lru_w_x, lru_b_x, lru_lambda, w_out, ffn2_norm, ffn2_w_gate, ffn2_w_up, ffn2_w_down, final_norm):
    bp, seq, _ = x_prompt.shape
    bd, t_len, _ = x_sample.shape
    depth = w_in.shape[0]
    n_pool = cache_k.shape[1]
    assert seq % CHUNK == 0 and t_len <= SAMPLE_ROWS and t_len >= CONV_W - 1
    assert page_table.shape[1] % PAGES_PER_STEP == 0
    tp = bp * seq
    ts = bd * SAMPLE_ROWS
    t = tp + ts
    tm_ffn = _pick_tile(t, 768)
    tm_proj = _pick_tile(t, 1056)
    tq = 256 if seq % 256 == 0 else CHUNK

    xs_pad = jnp.pad(x_sample, ((0, 0), (0, SAMPLE_ROWS - t_len), (0, 0)))
    x = jnp.concatenate([x_prompt.reshape(tp, D_MODEL), xs_pad.reshape(ts, D_MODEL)], axis=0)

    ck = jnp.transpose(cache_k, (0, 1, 3, 4, 2)).reshape(depth, n_pool, D_ATT, PAGE)
    cv = jnp.transpose(cache_v, (0, 1, 3, 4, 2)).reshape(depth, n_pool, D_ATT, PAGE)
    clf = jnp.transpose(cache_logf, (0, 1, 3, 2))
    st_ssd = state_ssd.reshape(depth, bd, D_SSD, SSD_STATE)
    zeros_ssd = jnp.zeros((1, bp, D_SSD, SSD_STATE), F32)
    e_mat = (jnp.arange(LANES)[:, None] == (jnp.arange(D_SSD)[None, :] // HEAD_DIM)).astype(BF16)
    row = lambda v: v.reshape(1, -1).astype(F32)
    pad_ff = D_FF_PAD - D_FF

    def ffn_weights(wg, wu, wd):
        wg = jnp.pad(wg.astype(BF16), ((0, 0), (0, pad_ff)))
        wu = jnp.pad(wu.astype(BF16), ((0, 0), (0, pad_ff)))
        wd = jnp.pad(wd.astype(BF16), ((0, pad_ff), (0, 0)))
        return wg, wu, wd

    outs_p = [[] for _ in range(7)]
    outs_s = [[] for _ in range(7)]
    for l in range(depth):
        last = l == depth - 1
        x = _ffn(x, row(ffn1_norm[l]), *ffn_weights(ffn1_w_gate[l], ffn1_w_up[l], ffn1_w_down[l]), None, tm_ffn)

        w = w_in[l]
        o_f, o_z, o_xbc = 3 * D_ATT, 3 * D_ATT + H_ATT, 3 * D_ATT + H_ATT + D_SSD
        o_dt = o_xbc + D_SSD + D_BC
        o_g = o_dt + H_SSD
        w_p = jnp.concatenate([
            w[:, o_z:o_z + D_SSD], w[:, o_xbc:o_xbc + D_SSD + D_BC], w[:, 0:3 * D_ATT],
            w[:, o_g:o_g + 2 * D_LRU],
            jnp.pad(w[:, o_f:o_f + H_ATT], ((0, 0), (0, LANES - H_ATT))),
            jnp.pad(w[:, o_dt:o_dt + H_SSD], ((0, 0), (0, LANES - H_SSD))),
        ], axis=1).astype(BF16)
        proj = _inproj(x, row(mix_norm[l]), w_p, tm_proj)

        bf_pad = _pad_lanes(fox_b_f[l])
        ssd_p = dict(
            cwx=ssd_conv_w[l][:, :D_SSD], cbx=row(ssd_conv_b[l][:D_SSD]),
            cwbc=ssd_conv_w[l][:, D_SSD:], cbbc=row(ssd_conv_b[l][D_SSD:]),
            dtb=_pad_lanes(ssd_dt_bias[l]), alog=_pad_lanes(ssd_a_log[l]),
            dexp=row(jnp.repeat(ssd_d[l], HEAD_DIM)), nw=row(ssd_norm[l]), e=e_mat, et=e_mat.T)
        lru_p = dict(
            cw=lru_conv_w[l], cb=row(lru_conv_b[l]),
            wa=_block_diag(lru_w_a[l]).astype(BF16), ba=row(lru_b_a[l]),
            wx=_block_diag(lru_w_x[l]).astype(BF16), bx=row(lru_b_x[l]), lam=row(lru_lambda[l]))

        logf_p, c_p, ct_p = _fox_prep(proj, bf_pad, bp, seq)
        att_p = _fox_prompt(proj, c_p, ct_p, bp, seq, tq)
        ssd_y_p, ssd_st_p = _ssd(proj, jnp.zeros((bp, SUBLANES, D_SSD), F32), jnp.zeros((bp, SUBLANES, D_BC), F32),
                                 zeros_ssd, 0, ssd_p, 0, bp, seq // CHUNK, CHUNK, CHUNK)
        lru_y_p, lru_st_p = _lru(proj, jnp.zeros((bp, SUBLANES, D_LRU), F32), jnp.zeros((bp, 1, D_LRU), F32),
                                 lru_p, 0, bp, seq // CHUNK, CHUNK, CHUNK)

        att_s, logf_s = _fox_sample(proj, bf_pad, ck, cv, clf, page_table, l, tp, bd, t_len)
        hist = _hist_tile(state_ssd_conv[l])
        ssd_y_s, ssd_st_s = _ssd(proj, hist[:, :, :D_SSD], hist[:, :, D_SSD:], st_ssd, l, ssd_p,
                                 tp, bd, 1, SAMPLE_ROWS, t_len)
        lru_y_s, lru_st_s = _lru(proj, _hist_tile(state_lru_conv[l]), state_lru[l].reshape(bd, 1, D_LRU),
                                 lru_p, tp, bd, 1, SAMPLE_ROWS, t_len)

        att = jnp.concatenate([att_p, att_s.astype(BF16)], axis=0)
        ssd_y = jnp.concatenate([ssd_y_p, ssd_y_s.astype(BF16)], axis=0)
        lru_y = jnp.concatenate([lru_y_p, lru_y_s.astype(BF16)], axis=0)
        x = _outproj(x, att, ssd_y, lru_y, w_out[l].astype(BF16), tm_ffn)
        x = _ffn(x, row(ffn2_norm[l]), *ffn_weights(ffn2_w_gate[l], ffn2_w_up[l], ffn2_w_down[l]),
                 row(final_norm) if last else None, tm_ffn)

        def cols_p(c0, n, r0=0):
            if r0:
                return jnp.stack([lax.slice(proj, (b * seq + r0, c0), ((b + 1) * seq, c0 + n)) for b in range(bp)])
            return lax.slice(proj, (0, c0), (tp, c0 + n)).reshape(bp, seq, n)

        def cols_s(c0, n, r0, r1):
            return lax.slice(proj, (tp, c0), (t, c0 + n)).reshape(bd, SAMPLE_ROWS, n)[:, r0:r1]

        tail = seq - (CONV_W - 1)
        outs_p[0].append(cols_p(C_K, D_ATT).reshape(bp, seq, H_ATT, HEAD_DIM))
        outs_p[1].append(cols_p(C_V, D_ATT).reshape(bp, seq, H_ATT, HEAD_DIM))
        outs_p[2].append(logf_p[:, :H_ATT].reshape(bp, seq, H_ATT))
        outs_p[3].append(ssd_st_p.reshape(bp, H_SSD, HEAD_DIM, SSD_STATE))
        outs_p[4].append(cols_p(C_XS, D_SSD + D_BC, tail))
        outs_p[5].append(lru_st_p.reshape(bp, D_LRU))
        outs_p[6].append(cols_p(C_U, D_LRU, tail))
        outs_s[0].append(cols_s(C_K, D_ATT, 0, t_len).reshape(bd, t_len, H_ATT, HEAD_DIM))
        outs_s[1].append(cols_s(C_V, D_ATT, 0, t_len).reshape(bd, t_len, H_ATT, HEAD_DIM))
        outs_s[2].append(logf_s.reshape(bd, SAMPLE_ROWS, LANES)[:, :t_len, :H_ATT])
        outs_s[3].append(ssd_st_s.reshape(bd, H_SSD, HEAD_DIM, SSD_STATE))
        outs_s[4].append(cols_s(C_XS, D_SSD + D_BC, t_len - (CONV_W - 1), t_len))
        outs_s[5].append(lru_st_s.reshape(bd, D_LRU))
        outs_s[6].append(cols_s(C_U, D_LRU, t_len - (CONV_W - 1), t_len))

    y_prompt = x[:tp].reshape(bp, seq, D_MODEL)
    y_sample = x[tp:].reshape(bd, SAMPLE_ROWS, D_MODEL)[:, :t_len]
    sp = [jnp.stack(o) for o in outs_p]
    ss = [jnp.stack(o) for o in outs_s]
    return (y_prompt, y_sample, sp[0], sp[1], sp[2], ss[0], ss[1], ss[2], sp[3], sp[4], ss[3], ss[4],
            sp[5], sp[6], ss[5], ss[6])
```

```python
import functools
import math

import numpy as np
import jax
import jax.numpy as jnp
from jax import lax
from jax.experimental import pallas as pl
from jax.experimental.pallas import tpu as pltpu

F32 = jnp.float32
BF16 = jnp.bfloat16

D_MODEL = 2048
HEAD_DIM = 64
D_ATT = 512
D_SSD = 1024
D_LRU = 512
H_ATT = 8
H_SSD = 16
SSD_GROUPS = 2
SSD_STATE = 128
D_BC = 2 * SSD_GROUPS * SSD_STATE
CONV_W = 4
LRU_BLOCKS = 8
LRU_BW = 64
LRU_C = 8.0
D_FF = 5504
PAGE = 128
RMS_EPS = 1e-6

LANES = 128
SUBLANES = 8
VMEM_LIMIT = 56 * 1024 * 1024

CHUNK = 128
SAMPLE_ROWS = 8
D_FF_PAD = 5632
FF_TILE = 512
PAGES_PER_STEP = 16

C_Z = 0
C_XS = 1024
C_BC = 2048
C_Q = 2560
C_K = 3072
C_V = 3584
C_G = 4096
C_U = 4608
C_F = 5120
C_DT = 5248
N_PROJ = 5376
PROJ_TILE = 768


def _cparams(sem):
    return pltpu.CompilerParams(dimension_semantics=sem, vmem_limit_bytes=VMEM_LIMIT)


def _rms(x, g):
    return x * lax.rsqrt(jnp.mean(x * x, axis=-1, keepdims=True) + RMS_EPS) * g


def _silu(x):
    return x * jax.nn.sigmoid(x)


def _softplus(x):
    return jnp.maximum(x, 0.0) + jnp.log1p(jnp.exp(-jnp.abs(x)))


def _split3(a):
    a1 = a.astype(BF16)
    r1 = a - a1.astype(F32)
    a2 = r1.astype(BF16)
    r2 = r1 - a2.astype(F32)
    return a1, a2, r2.astype(BF16)


def _dot_sel_r(a, sel):
    a1, a2, a3 = _split3(a)
    d = lambda p: jnp.dot(p, sel, preferred_element_type=F32)
    return d(a1) + d(a2) + d(a3)


def _dot_sel_l(sel, a):
    a1, a2, a3 = _split3(a)
    d = lambda p: jnp.dot(sel, p, preferred_element_type=F32)
    return d(a1) + d(a2) + d(a3)


def _dot_nt(a, b):
    return lax.dot_general(a, b, (((1,), (1,)), ((), ())), preferred_element_type=F32)


def _tile_dtype(rows):
    return BF16 if rows % (2 * SUBLANES) == 0 else F32


def _iota(shape, dim):
    return lax.broadcasted_iota(jnp.int32, shape, dim)


def _tril_bf16(n):
    return (_iota((n, n), 0) >= _iota((n, n), 1)).astype(BF16)


def _ffn_kernel(*refs, n_ff, final):
    if final:
        x_ref, g_ref, wg_ref, wu_ref, wd_ref, fg_ref, o_ref, xn_ref = refs
    else:
        x_ref, g_ref, wg_ref, wu_ref, wd_ref, o_ref, xn_ref = refs
    j = pl.program_id(1)

    @pl.when(j == 0)
    def _():
        xn_ref[...] = _rms(x_ref[...], g_ref[...]).astype(BF16)
        o_ref[...] = jnp.zeros_like(o_ref)

    xn = xn_ref[...]
    a = jnp.dot(xn, wg_ref[...], preferred_element_type=F32)
    b = jnp.dot(xn, wu_ref[...], preferred_element_type=F32)
    h = (_silu(a) * b).astype(BF16)
    o_ref[...] += jnp.dot(h, wd_ref[...], preferred_element_type=F32)

    @pl.when(j == n_ff - 1)
    def _():
        y = x_ref[...] + 0.5 * o_ref[...]
        if final:
            y = _rms(y, fg_ref[...])
        o_ref[...] = y


def _ffn(x, g, wg, wu, wd, final_g, layer, tm):
    t = x.shape[0]
    n_ff = D_FF_PAD // FF_TILE
    final = final_g is not None
    in_specs = [
        pl.BlockSpec((tm, D_MODEL), lambda i, j: (i, 0)),
        pl.BlockSpec((None, 1, D_MODEL), lambda i, j: (layer, 0, 0)),
        pl.BlockSpec((None, D_MODEL, FF_TILE), lambda i, j: (layer, 0, j)),
        pl.BlockSpec((None, D_MODEL, FF_TILE), lambda i, j: (layer, 0, j)),
        pl.BlockSpec((None, FF_TILE, D_MODEL), lambda i, j: (layer, j, 0)),
    ]
    args = [x, g, wg, wu, wd]
    if final:
        in_specs.append(pl.BlockSpec((1, D_MODEL), lambda i, j: (0, 0)))
        args.append(final_g)
    return pl.pallas_call(
        functools.partial(_ffn_kernel, n_ff=n_ff, final=final),
        grid=(t // tm, n_ff),
        in_specs=in_specs,
        out_specs=pl.BlockSpec((tm, D_MODEL), lambda i, j: (i, 0)),
        out_shape=jax.ShapeDtypeStruct((t, D_MODEL), F32),
        scratch_shapes=[pltpu.VMEM((tm, D_MODEL), BF16)],
        compiler_params=_cparams(("parallel", "arbitrary")),
        name="ffn",
    )(*args)


def _inproj_kernel(x_ref, g_ref, w_ref, o_ref, xn_ref):
    @pl.when(pl.program_id(1) == 0)
    def _():
        xn_ref[...] = _rms(x_ref[...], g_ref[...]).astype(BF16)

    o_ref[...] = jnp.dot(xn_ref[...], w_ref[...], preferred_element_type=F32)


def _inproj(x, g, w, layer, tm):
    t = x.shape[0]
    return pl.pallas_call(
        _inproj_kernel,
        grid=(t // tm, N_PROJ // PROJ_TILE),
        in_specs=[
            pl.BlockSpec((tm, D_MODEL), lambda i, j: (i, 0)),
            pl.BlockSpec((None, 1, D_MODEL), lambda i, j: (layer, 0, 0)),
            pl.BlockSpec((None, D_MODEL, PROJ_TILE), lambda i, j: (layer, 0, j)),
        ],
        out_specs=pl.BlockSpec((tm, PROJ_TILE), lambda i, j: (i, j)),
        out_shape=jax.ShapeDtypeStruct((t, N_PROJ), F32),
        scratch_shapes=[pltpu.VMEM((tm, D_MODEL), BF16)],
        compiler_params=_cparams(("parallel", "arbitrary")),
        name="inproj",
    )(x, g, w)


def _outproj_kernel(x_ref, att_ref, ssd_ref, lru_ref, w_ref, o_ref):
    acc = jnp.dot(att_ref[...], w_ref[0:D_ATT, :], preferred_element_type=F32)
    acc += jnp.dot(ssd_ref[...], w_ref[D_ATT:D_ATT + D_SSD, :], preferred_element_type=F32)
    acc += jnp.dot(lru_ref[...], w_ref[D_ATT + D_SSD:, :], preferred_element_type=F32)
    o_ref[...] = x_ref[...] + acc


def _outproj(x, att, ssd, lru, w, layer, tm):
    t = x.shape[0]
    return pl.pallas_call(
        _outproj_kernel,
        grid=(t // tm,),
        in_specs=[
            pl.BlockSpec((tm, D_MODEL), lambda i: (i, 0)),
            pl.BlockSpec((tm, D_ATT), lambda i: (i, 0)),
            pl.BlockSpec((tm, D_SSD), lambda i: (i, 0)),
            pl.BlockSpec((tm, D_LRU), lambda i: (i, 0)),
            pl.BlockSpec((None, D_MODEL, D_MODEL), lambda i: (layer, 0, 0)),
        ],
        out_specs=pl.BlockSpec((tm, D_MODEL), lambda i: (i, 0)),
        out_shape=jax.ShapeDtypeStruct((t, D_MODEL), F32),
        compiler_params=_cparams(("parallel",)),
        name="outproj",
    )(x, att, ssd, lru, w)


def _fox_prep_kernel(f_ref, bf_ref, logf_ref, c_ref, ct_ref, *, n_blk):
    tril = _tril_bf16(CHUNK)
    carry = jnp.zeros((1, LANES), F32)
    for i in range(n_blk):
        rows = slice(i * CHUNK, (i + 1) * CHUNK)
        lf = -_softplus(-(f_ref[rows, :] + bf_ref[...]))
        logf_ref[rows, :] = lf
        cs = _dot_sel_l(tril, lf) + carry
        carry = cs[CHUNK - 1:CHUNK, :]
        c_ref[rows, :] = cs
        ct_ref[0, :, rows] = cs.T[0:SUBLANES, :]


def _fox_prep(proj, bf_pad, layer, n_seq, seq):
    n_blk = seq // CHUNK
    return pl.pallas_call(
        functools.partial(_fox_prep_kernel, n_blk=n_blk),
        grid=(n_seq,),
        in_specs=[
            pl.BlockSpec((seq, LANES), lambda b: (b, C_F // LANES)),
            pl.BlockSpec((None, 1, LANES), lambda b: (layer, 0, 0)),
        ],
        out_specs=[
            pl.BlockSpec((seq, LANES), lambda b: (b, 0)),
            pl.BlockSpec((seq, LANES), lambda b: (b, 0)),
            pl.BlockSpec((1, SUBLANES, seq), lambda b: (b, 0, 0)),
        ],
        out_shape=[
            jax.ShapeDtypeStruct((n_seq * seq, LANES), F32),
            jax.ShapeDtypeStruct((n_seq * seq, LANES), F32),
            jax.ShapeDtypeStruct((n_seq, SUBLANES, seq), F32),
        ],
        compiler_params=_cparams(("parallel",)),
        name="fox_prep",
    )(proj, bf_pad)


def _fox_prompt_kernel(q_ref, k_ref, v_ref, c_ref, ct_ref, o_ref, *, tq):
    hp = pl.program_id(1)
    qi = pl.program_id(2)
    scale = 1.0 / math.sqrt(HEAD_DIM)
    q = q_ref[...]
    c_blk = c_ref[...]
    lane = _iota((tq, LANES), 1)
    upper = lane >= HEAD_DIM
    tri = _iota((tq, tq), 0) >= _iota((tq, tq), 1)
    heads = []
    for e in range(2):
        h = 2 * hp + e
        in_head = upper if e else jnp.logical_not(upper)
        qm = (jnp.where(in_head, q, 0.0) * scale).astype(BF16)
        c_col = jnp.sum(jnp.where(lane == h, c_blk, 0.0), axis=1, keepdims=True)
        heads.append((h, in_head, qm, c_col))

    def step(j, carry, masked):
        start = pl.multiple_of(j * tq, tq)
        kb = k_ref[pl.ds(start, tq), :].astype(BF16)
        v = v_ref[pl.ds(start, tq), :]
        new = []
        for (h, in_head, qm, c_col), (m, acc) in zip(heads, carry):
            c_row = ct_ref[0, pl.ds(h, 1), pl.ds(start, tq)]
            s = _dot_nt(qm, kb) + (c_col - c_row)
            if masked:
                s = jnp.where(tri, s, -1e30)
            m_new = jnp.maximum(m, jnp.max(s, axis=1, keepdims=True))
            p = jnp.exp(s - m_new).astype(BF16)
            ve = jnp.where(in_head, v, 1.0).astype(BF16)
            acc = jnp.exp(m - m_new) * acc + jnp.dot(p, ve, preferred_element_type=F32)
            new.append((m_new, acc))
        return tuple(new)

    init = tuple((jnp.full((tq, 1), -1e30, F32), jnp.zeros((tq, LANES), F32)) for _ in heads)
    carry = lax.fori_loop(0, qi, lambda j, cr: step(j, cr, False), init)
    (_, acc0), (_, acc1) = step(qi, carry, True)
    out0 = acc0 / acc0[:, HEAD_DIM:HEAD_DIM + 1]
    out1 = acc1 / acc1[:, 0:1]
    o_ref[...] = jnp.where(upper, out1, out0).astype(o_ref.dtype)


def _fox_prompt(proj, c, ct, n_seq, seq, tq):
    nq = seq // tq
    nkb = seq // tq
    del nkb
    return pl.pallas_call(
        functools.partial(_fox_prompt_kernel, tq=tq),
        grid=(n_seq, H_ATT // 2, nq),
        in_specs=[
            pl.BlockSpec((tq, LANES), lambda b, hp, qi: (b * nq + qi, C_Q // LANES + hp)),
            pl.BlockSpec((seq, LANES), lambda b, hp, qi: (b, C_K // LANES + hp)),
            pl.BlockSpec((seq, LANES), lambda b, hp, qi: (b, C_V // LANES + hp)),
            pl.BlockSpec((tq, LANES), lambda b, hp, qi: (b * nq + qi, 0)),
            pl.BlockSpec((1, SUBLANES, seq), lambda b, hp, qi: (b, 0, 0)),
        ],
        out_specs=pl.BlockSpec((tq, LANES), lambda b, hp, qi: (b * nq + qi, hp)),
        out_shape=jax.ShapeDtypeStruct((n_seq * seq, D_ATT), BF16),
        compiler_params=_cparams(("parallel", "parallel", "arbitrary")),
        name="fox_prompt",
    )(proj, proj, proj, c, ct)


def _fox_sample_kernel(pt_ref, *refs, n_steps, n_valid):
    del pt_ref
    npg = PAGES_PER_STEP
    q_ref, kn_ref, vn_ref, f_ref, bf_ref = refs[0:5]
    kp_refs = refs[5:5 + npg]
    vp_refs = refs[5 + npg:5 + 2 * npg]
    lf_refs = refs[5 + 2 * npg:5 + 3 * npg]
    o_ref, logf_ref = refs[5 + 3 * npg:7 + 3 * npg]
    qbd_ref, m_ref, l_ref, acc_ref, carry_ref, cnt_ref, pad_ref = refs[7 + 3 * npg:]
    j = pl.program_id(1)
    rows = SAMPLE_ROWS * H_ATT
    scale = 1.0 / math.sqrt(HEAD_DIM)
    row_i = _iota((rows, LANES), 0)
    lane_i = _iota((rows, LANES), 1)

    def rep_tokens(x):
        return jnp.concatenate(
            [jnp.broadcast_to(x[t:t + 1, :], (H_ATT, x.shape[1])) for t in range(SAMPLE_ROWS)], axis=0)

    def tile_heads(x):
        return jnp.concatenate([x] * SAMPLE_ROWS, axis=0)

    @pl.when(j == 0)
    def _():
        q = q_ref[...]
        head_of_lane = _iota((rows, D_ATT), 1) // HEAD_DIM
        head_of_row = _iota((rows, D_ATT), 0) % H_ATT
        qbd_ref[...] = (jnp.where(head_of_lane == head_of_row, rep_tokens(q), 0.0) * scale).astype(BF16)
        lf = -_softplus(-(f_ref[...] + bf_ref[...]))
        logf_ref[...] = lf
        ri = _iota((SAMPLE_ROWS, LANES), 0)
        cn = lf
        sh = 1
        while sh < SAMPLE_ROWS:
            cn = cn + jnp.where(ri >= sh, pltpu.roll(cn, sh, axis=0), 0.0)
            sh *= 2
        pad_ref[...] = jnp.zeros_like(pad_ref)
        pad_ref[0:SAMPLE_ROWS, :] = cn
        cnt_ref[...] = pad_ref[...].T[0:SUBLANES, :]
        m_ref[...] = jnp.full_like(m_ref, -1e30)
        l_ref[...] = jnp.zeros_like(l_ref)
        acc_ref[...] = jnp.zeros_like(acc_ref)
        carry_ref[...] = jnp.zeros_like(carry_ref)

    qbd = qbd_ref[...]

    def online(s, v_bf16, v_transposed):
        m = m_ref[...][:, 0:1]
        m_new = jnp.maximum(m, jnp.max(s, axis=1, keepdims=True))
        alpha = jnp.exp(m - m_new)
        p = jnp.exp(s - m_new)
        l_ref[...] = alpha * l_ref[...] + jnp.sum(p, axis=1, keepdims=True)
        pb = p.astype(BF16)
        pv = _dot_nt(pb, v_bf16) if v_transposed else jnp.dot(pb, v_bf16, preferred_element_type=F32)
        acc_ref[...] = alpha * acc_ref[...] + pv
        m_ref[...] = jnp.broadcast_to(m_new, m_ref.shape)

    carry = carry_ref[...][:, 0:1]
    lft = jnp.concatenate([lf_refs[p][0, 0] for p in range(npg)], axis=0)
    later = (_iota((PAGE, PAGE), 0) >= _iota((PAGE, PAGE), 1)).astype(BF16)
    incl = _dot_sel_r(lft, later)
    excl = incl - lft
    s_parts = []
    for p in range(npg):
        hs = slice(p * H_ATT, (p + 1) * H_ATT)
        suffix = excl[hs, :] + carry
        carry = carry + incl[hs, 0:1]
        kt = kp_refs[p][0, 0].astype(BF16)
        s_parts.append(jnp.dot(qbd, kt, preferred_element_type=F32) + tile_heads(suffix))
    carry_ref[...] = jnp.broadcast_to(carry, carry_ref.shape)
    s = jnp.concatenate(s_parts, axis=1)
    vt = jnp.concatenate([vp_refs[p][0, 0].astype(BF16) for p in range(npg)], axis=1)
    online(s, vt, True)

    @pl.when(j == n_steps - 1)
    def _():
        zk = jnp.zeros((PAGE - SAMPLE_ROWS, D_ATT), F32)
        kb = jnp.concatenate([kn_ref[...], zk], axis=0).astype(BF16)
        vb = jnp.concatenate([vn_ref[...], zk], axis=0).astype(BF16)
        s = _dot_nt(qbd, kb) - tile_heads(cnt_ref[...])
        ok = (lane_i <= row_i // H_ATT) & (lane_i < n_valid)
        online(jnp.where(ok, s, -1e30), vb, False)
        out = acc_ref[...] / l_ref[...][:, 0:1]
        head_of_lane = _iota((H_ATT, D_ATT), 1) // HEAD_DIM
        head_of_row = _iota((H_ATT, D_ATT), 0)
        for t in range(SAMPLE_ROWS):
            blk = out[t * H_ATT:(t + 1) * H_ATT, :]
            o_ref[t:t + 1, :] = jnp.sum(jnp.where(head_of_lane == head_of_row, blk, 0.0),
                                        axis=0, keepdims=True).astype(o_ref.dtype)


def _fox_sample(proj, bf_pad, cache_k, cache_v, cache_lf, page_table, layer, row0, n_seq, n_valid):
    npg = PAGES_PER_STEP
    n_pages = page_table.shape[1]
    n_steps = n_pages // npg
    rb0 = row0 // SAMPLE_ROWS
    rows = SAMPLE_ROWS * H_ATT

    def page_map(p):
        return lambda b, j, pt: (layer, pt[b, n_pages - 1 - (j * npg + p)], 0, 0)

    in_specs = [
        pl.BlockSpec((SAMPLE_ROWS, D_ATT), lambda b, j, pt: (rb0 + b, C_Q // D_ATT)),
        pl.BlockSpec((SAMPLE_ROWS, D_ATT), lambda b, j, pt: (rb0 + b, C_K // D_ATT)),
        pl.BlockSpec((SAMPLE_ROWS, D_ATT), lambda b, j, pt: (rb0 + b, C_V // D_ATT)),
        pl.BlockSpec((SAMPLE_ROWS, LANES), lambda b, j, pt: (rb0 + b, C_F // LANES)),
        pl.BlockSpec((None, 1, LANES), lambda b, j, pt: (layer, 0, 0)),
    ]
    in_specs += [pl.BlockSpec((1, 1, D_ATT, PAGE), page_map(p)) for p in range(npg)]
    in_specs += [pl.BlockSpec((1, 1, D_ATT, PAGE), page_map(p)) for p in range(npg)]
    in_specs += [pl.BlockSpec((1, 1, H_ATT, PAGE), page_map(p)) for p in range(npg)]
    grid_spec = pltpu.PrefetchScalarGridSpec(
        num_scalar_prefetch=1,
        grid=(n_seq, n_steps),
        in_specs=in_specs,
        out_specs=[
            pl.BlockSpec((SAMPLE_ROWS, D_ATT), lambda b, j, pt: (b, 0)),
            pl.BlockSpec((SAMPLE_ROWS, LANES), lambda b, j, pt: (b, 0)),
        ],
        scratch_shapes=[
            pltpu.VMEM((rows, D_ATT), BF16),
            pltpu.VMEM((rows, LANES), F32),
            pltpu.VMEM((rows, LANES), F32),
            pltpu.VMEM((rows, D_ATT), F32),
            pltpu.VMEM((SUBLANES, LANES), F32),
            pltpu.VMEM((SUBLANES, LANES), F32),
            pltpu.VMEM((PAGE, LANES), F32),
        ],
    )
    return pl.pallas_call(
        functools.partial(_fox_sample_kernel, n_steps=n_steps, n_valid=n_valid),
        grid_spec=grid_spec,
        out_shape=[
            jax.ShapeDtypeStruct((n_seq * SAMPLE_ROWS, D_ATT), F32),
            jax.ShapeDtypeStruct((n_seq * SAMPLE_ROWS, LANES), F32),
        ],
        compiler_params=_cparams(("parallel", "arbitrary")),
        name="fox_sample",
    )(page_table, proj, proj, proj, proj, bf_pad,
      *([cache_k] * npg), *([cache_v] * npg), *([cache_lf] * npg))


def _conv_from_ext(ext_ref, w_ref, b_ref, rows):
    out = b_ref[...] + ext_ref[SUBLANES:SUBLANES + rows, :] * w_ref[CONV_W - 1:CONV_W, :]
    for j in range(CONV_W - 1):
        off = SUBLANES - (CONV_W - 1) + j
        out = out + ext_ref[off:off + rows, :] * w_ref[j:j + 1, :]
    return out


def _ssd_kernel(z_ref, xs_ref, bc_ref, dt_ref, hx_ref, hbc_ref, h0_ref,
                cwx_ref, cbx_ref, cwbc_ref, cbbc_ref, dtb_ref, alog_ref, dexp_ref, nw_ref, e_ref, et_ref,
                y_ref, st_ref, extx_ref, extbc_ref, state_ref, *, blk_rows, n_valid):
    c = pl.program_id(1)
    r = CHUNK

    @pl.when(c == 0)
    def _():
        extx_ref[...] = jnp.zeros_like(extx_ref)
        extbc_ref[...] = jnp.zeros_like(extbc_ref)
        extx_ref[0:SUBLANES, :] = hx_ref[0]
        extbc_ref[0:SUBLANES, :] = hbc_ref[0]
        state_ref[...] = h0_ref[0, 0]

    extx_ref[SUBLANES:SUBLANES + blk_rows, :] = xs_ref[...]
    extbc_ref[SUBLANES:SUBLANES + blk_rows, :] = bc_ref[...]
    xs = _silu(_conv_from_ext(extx_ref, cwx_ref, cbx_ref, r))
    bc = _silu(_conv_from_ext(extbc_ref, cwbc_ref, cbbc_ref, r))
    if blk_rows == r:
        extx_ref[0:SUBLANES, :] = extx_ref[r:r + SUBLANES, :]
        extbc_ref[0:SUBLANES, :] = extbc_ref[r:r + SUBLANES, :]
        z = z_ref[...]
        dt_raw = dt_ref[...]
    else:
        z = jnp.concatenate([z_ref[...], jnp.zeros((r - blk_rows, D_SSD), F32)], axis=0)
        dt_raw = jnp.concatenate([dt_ref[...], jnp.zeros((r - blk_rows, LANES), F32)], axis=0)

    dt = _softplus(dt_raw + dtb_ref[...])
    if n_valid < r:
        dt = jnp.where(_iota((r, LANES), 0) < n_valid, dt, 0.0)
    a = -jnp.exp(alog_ref[...])
    acs = _dot_sel_l(_tril_bf16(r), dt * a)
    acs_t = acs.T
    e = e_ref[...]
    acs_x = _dot_sel_r(acs, e)
    dt_x = _dot_sel_r(dt, e)
    xdt = xs * dt_x
    xd = xdt * jnp.exp(acs_x[r - 1:r, :] - acs_x)
    state = state_ref[...]
    state_bf = state.astype(BF16)
    xd_t = xd.T.astype(BF16)
    xdt_bf = xdt.astype(BF16)
    tri = _iota((r, r), 0) >= _iota((r, r), 1)
    upper = _iota((r, LANES), 1) >= HEAD_DIM
    gw = D_SSD // SSD_GROUPS
    heads_per_group = H_SSD // SSD_GROUPS
    y_diag, y_off, new_state = [], [], []
    for g in range(SSD_GROUPS):
        bg = bc[:, g * SSD_STATE:(g + 1) * SSD_STATE].astype(BF16)
        cg = bc[:, D_BC // 2 + g * SSD_STATE:D_BC // 2 + (g + 1) * SSD_STATE].astype(BF16)
        cb = _dot_nt(cg, bg)
        y_off.append(_dot_nt(cg, state_bf[g * gw:(g + 1) * gw, :]))
        new_state.append(jnp.dot(xd_t[g * gw:(g + 1) * gw, :], bg, preferred_element_type=F32))
        for pi in range(heads_per_group // 2):
            lane0 = g * gw + pi * LANES
            pair = xdt_bf[:, lane0:lane0 + LANES]
            acc = None
            for hh in range(2):
                h = g * heads_per_group + 2 * pi + hh
                seg = acs[:, h:h + 1] - acs_t[h:h + 1, :]
                lm = jnp.where(tri, jnp.exp(jnp.where(tri, seg, 0.0)), 0.0)
                mh = (cb * lm).astype(BF16)
                in_head = upper if hh else jnp.logical_not(upper)
                d = jnp.dot(mh, jnp.where(in_head, pair, jnp.zeros_like(pair)), preferred_element_type=F32)
                acc = d if acc is None else acc + d
            y_diag.append(acc)
    y_diag = jnp.concatenate(y_diag, axis=1)
    y_off = jnp.concatenate(y_off, axis=1)
    new_state = jnp.concatenate(new_state, axis=0)
    last_col = _dot_sel_l(et_ref[...], acs_t)[:, r - 1:r]
    state_new = jnp.exp(last_col) * state + new_state
    state_ref[...] = state_new
    st_ref[0, 0] = state_new

    y = y_diag + y_off * jnp.exp(acs_x) + xs * dexp_ref[...]
    yg = y * _silu(z)
    parts = []
    for g in range(SSD_GROUPS):
        p = yg[:, g * gw:(g + 1) * gw]
        parts.append(p * lax.rsqrt(jnp.mean(p * p, axis=-1, keepdims=True) + RMS_EPS))
    out = jnp.concatenate(parts, axis=1) * nw_ref[...]
    y_ref[...] = out[0:blk_rows, :].astype(y_ref.dtype)


def _ssd(proj, hist, h0, state_layer, p, layer, row0, n_seq, n_chunks, blk_rows, n_valid):
    rb0 = row0 // blk_rows
    row_map = lambda col: (lambda b, c: (rb0 + b * n_chunks + c, col))
    const = lambda b, c: (0, 0)
    lp = lambda r, n: pl.BlockSpec((None, r, n), lambda b, c: (layer, 0, 0))
    return pl.pallas_call(
        functools.partial(_ssd_kernel, blk_rows=blk_rows, n_valid=n_valid),
        grid=(n_seq, n_chunks),
        in_specs=[
            pl.BlockSpec((blk_rows, D_SSD), row_map(C_Z // D_SSD)),
            pl.BlockSpec((blk_rows, D_SSD), row_map(C_XS // D_SSD)),
            pl.BlockSpec((blk_rows, D_BC), row_map(C_BC // D_BC)),
            pl.BlockSpec((blk_rows, LANES), row_map(C_DT // LANES)),
            pl.BlockSpec((None, 1, SUBLANES, D_SSD), lambda b, c: (state_layer, b, 0, 0)),
            pl.BlockSpec((None, 1, SUBLANES, D_BC), lambda b, c: (state_layer, b, 0, D_SSD // D_BC)),
            pl.BlockSpec((1, 1, D_SSD, SSD_STATE), lambda b, c: (state_layer, b, 0, 0)),
            lp(CONV_W, D_SSD), lp(1, D_SSD), lp(CONV_W, D_BC), lp(1, D_BC),
            lp(1, LANES), lp(1, LANES), lp(1, D_SSD), lp(1, D_SSD),
            pl.BlockSpec((LANES, D_SSD), const),
            pl.BlockSpec((D_SSD, LANES), const),
        ],
        out_specs=[
            pl.BlockSpec((blk_rows, D_SSD), lambda b, c: (b * n_chunks + c, 0)),
            pl.BlockSpec((1, 1, D_SSD, SSD_STATE), lambda b, c: (0, b, 0, 0)),
        ],
        out_shape=[
            jax.ShapeDtypeStruct((n_seq * n_chunks * blk_rows, D_SSD), _tile_dtype(blk_rows)),
            jax.ShapeDtypeStruct((1, n_seq, D_SSD, SSD_STATE), F32),
        ],
        scratch_shapes=[
            pltpu.VMEM((SUBLANES + CHUNK, D_SSD), F32),
            pltpu.VMEM((SUBLANES + CHUNK, D_BC), F32),
            pltpu.VMEM((D_SSD, SSD_STATE), F32),
        ],
        compiler_params=_cparams(("parallel", "arbitrary")),
        name="ssd",
    )(proj, proj, proj, proj, hist, hist, h0,
      p["cwx"], p["cbx"], p["cwbc"], p["cbbc"], p["dtb"], p["alog"], p["dexp"], p["nw"], p["e"], p["et"])


def _lru_kernel(g_ref, u_ref, hist_ref, h0_ref, cw_ref, cb_ref, wa_ref, ba_ref, wx_ref, bx_ref, lam_ref,
                y_ref, st_ref, ext_ref, hprev_ref, *, blk_rows, n_valid):
    c = pl.program_id(1)
    r = blk_rows

    @pl.when(c == 0)
    def _():
        ext_ref[0:SUBLANES, :] = hist_ref[0]
        hprev_ref[...] = h0_ref[0]

    ext_ref[SUBLANES:SUBLANES + r, :] = u_ref[...]
    u = _conv_from_ext(ext_ref, cw_ref, cb_ref, r)
    ext_ref[0:SUBLANES, :] = ext_ref[r:r + SUBLANES, :]
    ub = u.astype(BF16)
    rg = jax.nn.sigmoid(jnp.dot(ub, wa_ref[...], preferred_element_type=F32) + ba_ref[...])
    ig = jax.nn.sigmoid(jnp.dot(ub, wx_ref[...], preferred_element_type=F32) + bx_ref[...])
    log_a = -LRU_C * rg * _softplus(-lam_ref[...])
    a = jnp.exp(log_a)
    x = jnp.sqrt(-jnp.tanh(log_a) * (a * a + 1.0)) * (ig * u)
    row = _iota((r, D_LRU), 0)
    sh = 1
    while sh < r:
        keep = row >= sh
        a_prev = jnp.where(keep, pltpu.roll(a, sh, axis=0), 1.0)
        x_prev = jnp.where(keep, pltpu.roll(x, sh, axis=0), 0.0)
        x = x + a * x_prev
        a = a * a_prev
        sh *= 2
    hs = x + a * hprev_ref[...]
    hprev_ref[...] = hs[r - 1:r, :]
    st_ref[0] = hs[n_valid - 1:n_valid, :]
    y_ref[...] = (hs * jax.nn.gelu(g_ref[...])).astype(y_ref.dtype)


def _lru(proj, hist, h0, state_layer, p, layer, row0, n_seq, n_chunks, blk_rows, n_valid):
    rb0 = row0 // blk_rows
    row_map = lambda col: (lambda b, c: (rb0 + b * n_chunks + c, col))
    lp = lambda r, n: pl.BlockSpec((None, r, n), lambda b, c: (layer, 0, 0))
    return pl.pallas_call(
        functools.partial(_lru_kernel, blk_rows=blk_rows, n_valid=n_valid),
        grid=(n_seq, n_chunks),
        in_specs=[
            pl.BlockSpec((blk_rows, D_LRU), row_map(C_G // D_LRU)),
            pl.BlockSpec((blk_rows, D_LRU), row_map(C_U // D_LRU)),
            pl.BlockSpec((None, 1, SUBLANES, D_LRU), lambda b, c: (state_layer, b, 0, 0)),
            pl.BlockSpec((None, 1, 1, D_LRU), lambda b, c: (state_layer, b, 0, 0)),
            lp(CONV_W, D_LRU), lp(1, D_LRU), lp(D_LRU, D_LRU), lp(1, D_LRU), lp(D_LRU, D_LRU), lp(1, D_LRU),
            lp(1, D_LRU),
        ],
        out_specs=[
            pl.BlockSpec((blk_rows, D_LRU), lambda b, c: (b * n_chunks + c, 0)),
            pl.BlockSpec((1, 1, D_LRU), lambda b, c: (b, 0, 0)),
        ],
        out_shape=[
            jax.ShapeDtypeStruct((n_seq * n_chunks * blk_rows, D_LRU), _tile_dtype(blk_rows)),
            jax.ShapeDtypeStruct((n_seq, 1, D_LRU), F32),
        ],
        scratch_shapes=[
            pltpu.VMEM((SUBLANES + blk_rows, D_LRU), F32),
            pltpu.VMEM((1, D_LRU), F32),
        ],
        compiler_params=_cparams(("parallel", "arbitrary")),
        name="lru",
    )(proj, proj, hist, h0, p["cw"], p["cb"], p["wa"], p["ba"], p["wx"], p["bx"], p["lam"])


def _hist_tile(h):
    return jnp.pad(h, ((0, 0), (0, 0), (SUBLANES - (CONV_W - 1), 0), (0, 0)))


def _block_diag(w):
    eye = jnp.eye(LRU_BLOCKS, dtype=w.dtype)
    return jnp.einsum("lhij,hg->lhigj", w, eye).reshape(w.shape[0], D_LRU, D_LRU)


def _pick_tile(t, cap):
    best = SUBLANES
    for tm in range(SUBLANES, cap + 1, SUBLANES):
        if t % tm == 0:
            best = tm
    return best


def kernel(x_prompt, x_sample, cache_k, cache_v, cache_logf, page_table, state_ssd, state_ssd_conv, state_lru, state_lru_conv, ffn1_norm, ffn1_w_gate, ffn1_w_up, ffn1_w_down, mix_norm, w_in, fox_b_f, ssd_conv_w, ssd_conv_b, ssd_dt_bias, ssd_a_log, ssd_d, ssd_norm, lru_conv_w, lru_conv_b, lru_w_a, lru_b_a, lru_w_x, lru_b_x, lru_lambda, w_out, ffn2_norm, ffn2_w_gate, ffn2_w_up, ffn2_w_down, final_norm):
    bp, seq, _ = x_prompt.shape
    bd, t_len, _ = x_sample.shape
    depth = w_in.shape[0]
    n_pool = cache_k.shape[1]
    assert seq % CHUNK == 0 and t_len <= SAMPLE_ROWS and t_len >= CONV_W - 1
    assert page_table.shape[1] % PAGES_PER_STEP == 0
    tp = bp * seq
    ts = bd * SAMPLE_ROWS
    t = tp + ts
    tm_ffn = _pick_tile(t, 768)
    tm_proj = _pick_tile(t, 1056)
    tq = 256 if seq % 256 == 0 else CHUNK

    xs_pad = jnp.pad(x_sample, ((0, 0), (0, SAMPLE_ROWS - t_len), (0, 0)))
    x = jnp.concatenate([x_prompt.reshape(tp, D_MODEL), xs_pad.reshape(ts, D_MODEL)], axis=0)

    ck = jnp.transpose(cache_k, (0, 1, 3, 4, 2)).reshape(depth, n_pool, D_ATT, PAGE)
    cv = jnp.transpose(cache_v, (0, 1, 3, 4, 2)).reshape(depth, n_pool, D_ATT, PAGE)
    clf = jnp.transpose(cache_logf, (0, 1, 3, 2))
    e_mat = (jnp.arange(LANES)[:, None] == (jnp.arange(D_SSD)[None, :] // HEAD_DIM)).astype(BF16)

    rows3 = lambda v: v.reshape(depth, 1, -1).astype(F32)
    lanes3 = lambda v: jnp.pad(rows3(v), ((0, 0), (0, 0), (0, LANES - v.shape[-1])))
    pad_ff = D_FF_PAD - D_FF
    cols_bf16 = lambda w: jnp.pad(w.astype(BF16), ((0, 0), (0, 0), (0, pad_ff)))
    rows_bf16 = lambda w: jnp.pad(w.astype(BF16), ((0, 0), (0, pad_ff), (0, 0)))
    ffn1 = (rows3(ffn1_norm), cols_bf16(ffn1_w_gate), cols_bf16(ffn1_w_up), rows_bf16(ffn1_w_down))
    ffn2 = (rows3(ffn2_norm), cols_bf16(ffn2_w_gate), cols_bf16(ffn2_w_up), rows_bf16(ffn2_w_down))
    o_f, o_z, o_xbc = 3 * D_ATT, 3 * D_ATT + H_ATT, 3 * D_ATT + H_ATT + D_SSD
    o_dt = o_xbc + D_SSD + D_BC
    o_g = o_dt + H_SSD
    lane_pad = lambda w: jnp.pad(w, ((0, 0), (0, 0), (0, LANES - w.shape[-1])))
    w_in_p = jnp.concatenate([
        w_in[:, :, o_z:o_z + D_SSD], w_in[:, :, o_xbc:o_xbc + D_SSD + D_BC], w_in[:, :, 0:3 * D_ATT],
        w_in[:, :, o_g:o_g + 2 * D_LRU],
        lane_pad(w_in[:, :, o_f:o_f + H_ATT]), lane_pad(w_in[:, :, o_dt:o_dt + H_SSD]),
    ], axis=2).astype(BF16)
    mix_g = rows3(mix_norm)
    w_out_b = w_out.astype(BF16)
    bf_pad = lanes3(fox_b_f)
    ssd_p = dict(
        cwx=ssd_conv_w[:, :, :D_SSD], cbx=rows3(ssd_conv_b[:, :D_SSD]),
        cwbc=ssd_conv_w[:, :, D_SSD:], cbbc=rows3(ssd_conv_b[:, D_SSD:]),
        dtb=lanes3(ssd_dt_bias), alog=lanes3(ssd_a_log),
        dexp=rows3(jnp.repeat(ssd_d, HEAD_DIM, axis=1)), nw=rows3(ssd_norm), e=e_mat, et=e_mat.T)
    lru_p = dict(
        cw=lru_conv_w, cb=rows3(lru_conv_b),
        wa=_block_diag(lru_w_a).astype(BF16), ba=rows3(lru_b_a),
        wx=_block_diag(lru_w_x).astype(BF16), bx=rows3(lru_b_x), lam=rows3(lru_lambda))
    st_ssd = state_ssd.reshape(depth, bd, D_SSD, SSD_STATE)
    hist_ssd = _hist_tile(state_ssd_conv)
    hist_lru = _hist_tile(state_lru_conv)
    st_lru = state_lru.reshape(depth, bd, 1, D_LRU)
    zeros_ssd = jnp.zeros((1, bp, D_SSD, SSD_STATE), F32)
    zeros_hist_ssd = jnp.zeros((1, bp, SUBLANES, D_SSD + D_BC), F32)
    zeros_hist_lru = jnp.zeros((1, bp, SUBLANES, D_LRU), F32)
    zeros_lru = jnp.zeros((1, bp, 1, D_LRU), F32)
    n_chunks = seq // CHUNK

    outs_p = [[] for _ in range(7)]
    outs_s = [[] for _ in range(7)]
    for l in range(depth):
        last = l == depth - 1
        x = _ffn(x, *ffn1, None, l, tm_ffn)
        proj = _inproj(x, mix_g, w_in_p, l, tm_proj)

        logf_p, c_p, ct_p = _fox_prep(proj, bf_pad, l, bp, seq)
        att_p = _fox_prompt(proj, c_p, ct_p, bp, seq, tq)
        ssd_y_p, ssd_st_p = _ssd(proj, zeros_hist_ssd, zeros_ssd, 0, ssd_p, l, 0, bp, n_chunks, CHUNK, CHUNK)
        lru_y_p, lru_st_p = _lru(proj, zeros_hist_lru, zeros_lru, 0, lru_p, l, 0, bp, n_chunks, CHUNK, CHUNK)

        att_s, logf_s = _fox_sample(proj, bf_pad, ck, cv, clf, page_table, l, tp, bd, t_len)
        ssd_y_s, ssd_st_s = _ssd(proj, hist_ssd, st_ssd, l, ssd_p, l, tp, bd, 1, SAMPLE_ROWS, t_len)
        lru_y_s, lru_st_s = _lru(proj, hist_lru, st_lru, l, lru_p, l, tp, bd, 1, SAMPLE_ROWS, t_len)

        att = jnp.concatenate([att_p, att_s.astype(BF16)], axis=0)
        ssd_y = jnp.concatenate([ssd_y_p, ssd_y_s.astype(BF16)], axis=0)
        lru_y = jnp.concatenate([lru_y_p, lru_y_s.astype(BF16)], axis=0)
        x = _outproj(x, att, ssd_y, lru_y, w_out_b, l, tm_ffn)
        x = _ffn(x, *ffn2, final_norm.reshape(1, D_MODEL) if last else None, l, tm_ffn)

        def cols_p(c0, n, r0=0):
            if r0:
                return jnp.stack([lax.slice(proj, (b * seq + r0, c0), ((b + 1) * seq, c0 + n)) for b in range(bp)])
            return lax.slice(proj, (0, c0), (tp, c0 + n)).reshape(bp, seq, n)

        def cols_s(c0, n, r0, r1):
            return lax.slice(proj, (tp, c0), (t, c0 + n)).reshape(bd, SAMPLE_ROWS, n)[:, r0:r1]

        tail = seq - (CONV_W - 1)
        outs_p[0].append(cols_p(C_K, D_ATT).reshape(bp, seq, H_ATT, HEAD_DIM))
        outs_p[1].append(cols_p(C_V, D_ATT).reshape(bp, seq, H_ATT, HEAD_DIM))
        outs_p[2].append(logf_p[:, :H_ATT].reshape(bp, seq, H_ATT))
        outs_p[3].append(ssd_st_p.reshape(bp, H_SSD, HEAD_DIM, SSD_STATE))
        outs_p[4].append(cols_p(C_XS, D_SSD + D_BC, tail))
        outs_p[5].append(lru_st_p.reshape(bp, D_LRU))
        outs_p[6].append(cols_p(C_U, D_LRU, tail))
        outs_s[0].append(cols_s(C_K, D_ATT, 0, t_len).reshape(bd, t_len, H_ATT, HEAD_DIM))
        outs_s[1].append(cols_s(C_V, D_ATT, 0, t_len).reshape(bd, t_len, H_ATT, HEAD_DIM))
        outs_s[2].append(logf_s.reshape(bd, SAMPLE_ROWS, LANES)[:, :t_len, :H_ATT])
        outs_s[3].append(ssd_st_s.reshape(bd, H_SSD, HEAD_DIM, SSD_STATE))
        outs_s[4].append(cols_s(C_XS, D_SSD + D_BC, t_len - (CONV_W - 1), t_len))
        outs_s[5].append(lru_st_s.reshape(bd, D_LRU))
        outs_s[6].append(cols_s(C_U, D_LRU, t_len - (CONV_W - 1), t_len))

    y_prompt = x[:tp].reshape(bp, seq, D_MODEL)
    y_sample = x[tp:].reshape(bd, SAMPLE_ROWS, D_MODEL)[:, :t_len]
    sp = [jnp.stack(o) for o in outs_p]
    ss = [jnp.stack(o) for o in outs_s]
    return (y_prompt, y_sample, sp[0], sp[1], sp[2], ss[0], ss[1], ss[2], sp[3], sp[4], ss[3], ss[4],
            sp[5], sp[6], ss[5], ss[6])
```

```python
import functools
import math

import numpy as np
import jax
import jax.numpy as jnp
from jax import lax
from jax.experimental import pallas as pl
from jax.experimental.pallas import tpu as pltpu

F32 = jnp.float32
BF16 = jnp.bfloat16

D_MODEL = 2048
HEAD_DIM = 64
D_ATT = 512
D_SSD = 1024
D_LRU = 512
H_ATT = 8
H_SSD = 16
SSD_GROUPS = 2
SSD_STATE = 128
D_BC = 2 * SSD_GROUPS * SSD_STATE
CONV_W = 4
LRU_BLOCKS = 8
LRU_BW = 64
LRU_C = 8.0
D_FF = 5504
PAGE = 128
RMS_EPS = 1e-6

LANES = 128
SUBLANES = 8
VMEM_LIMIT = 56 * 1024 * 1024

CHUNK = 128
SAMPLE_ROWS = 8
FF_TILE = 512
PAGES_PER_STEP = 32

C_Z = 0
C_XS = 1024
C_BC = 2048
C_Q = 2560
C_K = 3072
C_V = 3584
C_G = 4096
C_U = 4608
C_F = 5120
C_DT = 5248
N_PROJ = 5376
PROJ_TILE = 768


def _cparams(sem):
    return pltpu.CompilerParams(dimension_semantics=sem, vmem_limit_bytes=VMEM_LIMIT)


def _rms(x, g):
    return x * lax.rsqrt(jnp.mean(x * x, axis=-1, keepdims=True) + RMS_EPS) * g


def _silu(x):
    return x * jax.nn.sigmoid(x)


def _softplus(x):
    return jnp.maximum(x, 0.0) + jnp.log1p(jnp.exp(-jnp.abs(x)))


def _split3(a):
    a1 = a.astype(BF16)
    r1 = a - a1.astype(F32)
    a2 = r1.astype(BF16)
    r2 = r1 - a2.astype(F32)
    return a1, a2, r2.astype(BF16)


def _dot_sel_r(a, sel):
    a1, a2, a3 = _split3(a)
    d = lambda p: jnp.dot(p, sel, preferred_element_type=F32)
    return d(a1) + d(a2) + d(a3)


def _dot_sel_l(sel, a):
    a1, a2, a3 = _split3(a)
    d = lambda p: jnp.dot(sel, p, preferred_element_type=F32)
    return d(a1) + d(a2) + d(a3)


def _dot_nt(a, b):
    return lax.dot_general(a, b, (((1,), (1,)), ((), ())), preferred_element_type=F32)


def _tile_dtype(rows):
    return BF16 if rows % (2 * SUBLANES) == 0 else F32


def _iota(shape, dim):
    return lax.broadcasted_iota(jnp.int32, shape, dim)


def _tril_bf16(n):
    return (_iota((n, n), 0) >= _iota((n, n), 1)).astype(BF16)


def _ffn_kernel(*refs, n_main, n_tail, final):
    x_ref, g_ref = refs[0:2]
    main = refs[2:5]
    tails = [refs[5 + 3 * k:8 + 3 * k] for k in range(n_tail)]
    rest = refs[5 + 3 * n_tail:]
    if final:
        fg_ref, o_ref, xn_ref = rest
    else:
        o_ref, xn_ref = rest
    j = pl.program_id(1)

    def part(wg_ref, wu_ref, wd_ref):
        xn = xn_ref[...]
        a = jnp.dot(xn, wg_ref[...], preferred_element_type=F32)
        b = jnp.dot(xn, wu_ref[...], preferred_element_type=F32)
        h = (_silu(a) * b).astype(BF16)
        return jnp.dot(h, wd_ref[...], preferred_element_type=F32)

    @pl.when(j == 0)
    def _():
        xn_ref[...] = _rms(x_ref[...], g_ref[...]).astype(BF16)
        acc = jnp.zeros(o_ref.shape, F32)
        for tail in tails:
            acc = acc + part(*tail)
        o_ref[...] = acc

    o_ref[...] += part(*main)

    @pl.when(j == n_main - 1)
    def _():
        y = x_ref[...] + 0.5 * o_ref[...]
        if final:
            y = _rms(y, fg_ref[...])
        o_ref[...] = y


def _ffn_tail_tiles():
    tiles, start, width = [], (D_FF // FF_TILE) * FF_TILE, FF_TILE // 2
    while start < D_FF:
        if start + width <= D_FF:
            assert width % LANES == 0 and start % width == 0
            tiles.append((width, start // width))
            start += width
        width //= 2
    return tiles


def _ffn(x, g, wg, wu, wd, final_g, layer, tm):
    t = x.shape[0]
    n_main = D_FF // FF_TILE
    tails = _ffn_tail_tiles()
    final = final_g is not None

    def weight_specs(width, col):
        return [pl.BlockSpec((None, D_MODEL, width), lambda i, j: (layer, 0, col(j))),
                pl.BlockSpec((None, D_MODEL, width), lambda i, j: (layer, 0, col(j))),
                pl.BlockSpec((None, width, D_MODEL), lambda i, j: (layer, col(j), 0))]

    in_specs = [
        pl.BlockSpec((tm, D_MODEL), lambda i, j: (i, 0)),
        pl.BlockSpec((None, 1, D_MODEL), lambda i, j: (layer, 0, 0)),
    ] + weight_specs(FF_TILE, lambda j: j)
    args = [x, g, wg, wu, wd]
    for width, blk in tails:
        in_specs += weight_specs(width, lambda j, blk=blk: blk)
        args += [wg, wu, wd]
    if final:
        in_specs.append(pl.BlockSpec((1, D_MODEL), lambda i, j: (0, 0)))
        args.append(final_g)
    return pl.pallas_call(
        functools.partial(_ffn_kernel, n_main=n_main, n_tail=len(tails), final=final),
        grid=(t // tm, n_main),
        in_specs=in_specs,
        out_specs=pl.BlockSpec((tm, D_MODEL), lambda i, j: (i, 0)),
        out_shape=jax.ShapeDtypeStruct((t, D_MODEL), F32),
        scratch_shapes=[pltpu.VMEM((tm, D_MODEL), BF16)],
        compiler_params=_cparams(("parallel", "arbitrary")),
        name="ffn",
    )(*args)


def _inproj_kernel(x_ref, g_ref, w_ref, o_ref, xn_ref):
    @pl.when(pl.program_id(1) == 0)
    def _():
        xn_ref[...] = _rms(x_ref[...], g_ref[...]).astype(BF16)

    o_ref[...] = jnp.dot(xn_ref[...], w_ref[...], preferred_element_type=F32)


def _inproj(x, g, w, layer, tm):
    t = x.shape[0]
    return pl.pallas_call(
        _inproj_kernel,
        grid=(t // tm, N_PROJ // PROJ_TILE),
        in_specs=[
            pl.BlockSpec((tm, D_MODEL), lambda i, j: (i, 0)),
            pl.BlockSpec((None, 1, D_MODEL), lambda i, j: (layer, 0, 0)),
            pl.BlockSpec((None, D_MODEL, PROJ_TILE), lambda i, j: (layer, 0, j)),
        ],
        out_specs=pl.BlockSpec((tm, PROJ_TILE), lambda i, j: (i, j)),
        out_shape=jax.ShapeDtypeStruct((t, N_PROJ), F32),
        scratch_shapes=[pltpu.VMEM((tm, D_MODEL), BF16)],
        compiler_params=_cparams(("parallel", "arbitrary")),
        name="inproj",
    )(x, g, w)


def _outproj_kernel(x_ref, att_ref, ssd_ref, lru_ref, w_ref, o_ref):
    acc = jnp.dot(att_ref[...], w_ref[0:D_ATT, :], preferred_element_type=F32)
    acc += jnp.dot(ssd_ref[...], w_ref[D_ATT:D_ATT + D_SSD, :], preferred_element_type=F32)
    acc += jnp.dot(lru_ref[...], w_ref[D_ATT + D_SSD:, :], preferred_element_type=F32)
    o_ref[...] = x_ref[...] + acc


def _outproj(x, att, ssd, lru, w, layer, tm):
    t = x.shape[0]
    return pl.pallas_call(
        _outproj_kernel,
        grid=(t // tm,),
        in_specs=[
            pl.BlockSpec((tm, D_MODEL), lambda i: (i, 0)),
            pl.BlockSpec((tm, D_ATT), lambda i: (i, 0)),
            pl.BlockSpec((tm, D_SSD), lambda i: (i, 0)),
            pl.BlockSpec((tm, D_LRU), lambda i: (i, 0)),
            pl.BlockSpec((None, D_MODEL, D_MODEL), lambda i: (layer, 0, 0)),
        ],
        out_specs=pl.BlockSpec((tm, D_MODEL), lambda i: (i, 0)),
        out_shape=jax.ShapeDtypeStruct((t, D_MODEL), F32),
        compiler_params=_cparams(("parallel",)),
        name="outproj",
    )(x, att, ssd, lru, w)


def _fox_prep_kernel(f_ref, bf_ref, logf_ref, c_ref, ct_ref, *, n_blk):
    tril = _tril_bf16(CHUNK)
    carry = jnp.zeros((1, LANES), F32)
    for i in range(n_blk):
        rows = slice(i * CHUNK, (i + 1) * CHUNK)
        lf = -_softplus(-(f_ref[rows, :] + bf_ref[...]))
        logf_ref[rows, :] = lf
        cs = _dot_sel_l(tril, lf) + carry
        carry = cs[CHUNK - 1:CHUNK, :]
        c_ref[rows, :] = cs
        ct_ref[0, :, rows] = cs.T[0:SUBLANES, :]


def _fox_prep(proj, bf_pad, layer, n_seq, seq):
    n_blk = seq // CHUNK
    return pl.pallas_call(
        functools.partial(_fox_prep_kernel, n_blk=n_blk),
        grid=(n_seq,),
        in_specs=[
            pl.BlockSpec((seq, LANES), lambda b: (b, C_F // LANES)),
            pl.BlockSpec((None, 1, LANES), lambda b: (layer, 0, 0)),
        ],
        out_specs=[
            pl.BlockSpec((seq, LANES), lambda b: (b, 0)),
            pl.BlockSpec((seq, LANES), lambda b: (b, 0)),
            pl.BlockSpec((1, SUBLANES, seq), lambda b: (b, 0, 0)),
        ],
        out_shape=[
            jax.ShapeDtypeStruct((n_seq * seq, LANES), F32),
            jax.ShapeDtypeStruct((n_seq * seq, LANES), F32),
            jax.ShapeDtypeStruct((n_seq, SUBLANES, seq), F32),
        ],
        compiler_params=_cparams(("parallel",)),
        name="fox_prep",
    )(proj, bf_pad)


def _fox_prompt_kernel(q_ref, k_ref, v_ref, c_ref, ct_ref, o_ref, *, tq, nq):
    hp = pl.program_id(1)
    qi = pl.program_id(2)
    scale = 1.0 / math.sqrt(HEAD_DIM)
    q = q_ref[...]
    c_blk = c_ref[...]
    lane = _iota((tq, LANES), 1)
    upper = lane >= HEAD_DIM
    tri = _iota((tq, tq), 0) >= _iota((tq, tq), 1)
    heads = []
    for e in range(2):
        h = 2 * hp + e
        in_head = upper if e else jnp.logical_not(upper)
        qm = (jnp.where(in_head, q, 0.0) * scale).astype(BF16)
        c_col = jnp.sum(jnp.where(lane == h, c_blk, 0.0), axis=1, keepdims=True)
        heads.append((h, qm, c_col))

    for n in range(1, nq + 1):
        @pl.when(qi == n - 1)
        def _(n=n):
            w = n * tq
            kb = k_ref[0:w, :].astype(BF16)
            v = v_ref[0:w, :]
            upper_w = _iota((w, LANES), 1) >= HEAD_DIM
            outs = []
            for e, (h, qm, c_col) in enumerate(heads):
                c_row = ct_ref[0, pl.ds(h, 1), 0:w]
                s = _dot_nt(qm, kb) + (c_col - c_row)
                diag = jnp.where(tri, s[:, w - tq:], -1e30)
                s = diag if n == 1 else jnp.concatenate([s[:, :w - tq], diag], axis=1)
                p = jnp.exp(s - jnp.max(s, axis=1, keepdims=True)).astype(BF16)
                in_head = upper_w if e else jnp.logical_not(upper_w)
                acc = jnp.dot(p, jnp.where(in_head, v, 1.0).astype(BF16), preferred_element_type=F32)
                den = acc[:, 0:1] if e else acc[:, HEAD_DIM:HEAD_DIM + 1]
                outs.append(acc / den)
            o_ref[...] = jnp.where(upper, outs[1], outs[0]).astype(o_ref.dtype)


def _fox_prompt(proj, c, ct, n_seq, seq, tq):
    nq = seq // tq
    return pl.pallas_call(
        functools.partial(_fox_prompt_kernel, tq=tq, nq=nq),
        grid=(n_seq, H_ATT // 2, nq),
        in_specs=[
            pl.BlockSpec((tq, LANES), lambda b, hp, qi: (b * nq + qi, C_Q // LANES + hp)),
            pl.BlockSpec((seq, LANES), lambda b, hp, qi: (b, C_K // LANES + hp)),
            pl.BlockSpec((seq, LANES), lambda b, hp, qi: (b, C_V // LANES + hp)),
            pl.BlockSpec((tq, LANES), lambda b, hp, qi: (b * nq + qi, 0)),
            pl.BlockSpec((1, SUBLANES, seq), lambda b, hp, qi: (b, 0, 0)),
        ],
        out_specs=pl.BlockSpec((tq, LANES), lambda b, hp, qi: (b * nq + qi, hp)),
        out_shape=jax.ShapeDtypeStruct((n_seq * seq, D_ATT), BF16),
        compiler_params=_cparams(("parallel", "parallel", "arbitrary")),
        name="fox_prompt",
    )(proj, proj, proj, c, ct)


def _fox_sample_kernel(pt_ref, *refs, n_steps, n_valid):
    del pt_ref
    npg = PAGES_PER_STEP
    q_ref, kn_ref, vn_ref, f_ref, bf_ref = refs[0:5]
    kp_refs = refs[5:5 + npg]
    vp_refs = refs[5 + npg:5 + 2 * npg]
    lf_refs = refs[5 + 2 * npg:5 + 3 * npg]
    o_ref, logf_ref = refs[5 + 3 * npg:7 + 3 * npg]
    qbd_ref, m_ref, l_ref, acc_ref, carry_ref, cnt_ref, pad_ref = refs[7 + 3 * npg:]
    j = pl.program_id(1)
    rows = SAMPLE_ROWS * H_ATT
    scale = 1.0 / math.sqrt(HEAD_DIM)
    row_i = _iota((rows, LANES), 0)
    lane_i = _iota((rows, LANES), 1)

    def rep_tokens(x):
        return jnp.concatenate(
            [jnp.broadcast_to(x[t:t + 1, :], (H_ATT, x.shape[1])) for t in range(SAMPLE_ROWS)], axis=0)

    def tile_heads(x):
        return jnp.concatenate([x] * SAMPLE_ROWS, axis=0)

    @pl.when(j == 0)
    def _():
        q = q_ref[...]
        head_of_lane = _iota((rows, D_ATT), 1) // HEAD_DIM
        head_of_row = _iota((rows, D_ATT), 0) % H_ATT
        qbd_ref[...] = (jnp.where(head_of_lane == head_of_row, rep_tokens(q), 0.0) * scale).astype(BF16)
        lf = -_softplus(-(f_ref[...] + bf_ref[...]))
        logf_ref[...] = lf
        ri = _iota((SAMPLE_ROWS, LANES), 0)
        cn = lf
        sh = 1
        while sh < SAMPLE_ROWS:
            cn = cn + jnp.where(ri >= sh, pltpu.roll(cn, sh, axis=0), 0.0)
            sh *= 2
        pad_ref[...] = jnp.zeros_like(pad_ref)
        pad_ref[0:SAMPLE_ROWS, :] = cn
        cnt_ref[...] = pad_ref[...].T[0:SUBLANES, :]
        m_ref[...] = jnp.full_like(m_ref, -1e30)
        l_ref[...] = jnp.zeros_like(l_ref)
        acc_ref[...] = jnp.zeros_like(acc_ref)
        carry_ref[...] = jnp.zeros_like(carry_ref)

    qbd = qbd_ref[...]

    def online(s, v_bf16, v_transposed):
        m = m_ref[...][:, 0:1]
        m_new = jnp.maximum(m, jnp.max(s, axis=1, keepdims=True))
        alpha = jnp.exp(m - m_new)
        p = jnp.exp(s - m_new)
        l_ref[...] = alpha * l_ref[...] + jnp.sum(p, axis=1, keepdims=True)
        pb = p.astype(BF16)
        pv = _dot_nt(pb, v_bf16) if v_transposed else jnp.dot(pb, v_bf16, preferred_element_type=F32)
        acc_ref[...] = alpha * acc_ref[...] + pv
        m_ref[...] = jnp.broadcast_to(m_new, m_ref.shape)

    carry = carry_ref[...][:, 0:1]
    lft = jnp.concatenate([lf_refs[p][0, 0] for p in range(npg)], axis=0)
    later = (_iota((PAGE, PAGE), 0) >= _iota((PAGE, PAGE), 1)).astype(BF16)
    incl = _dot_sel_r(lft, later)
    excl = incl - lft
    s_parts = []
    for p in range(npg):
        hs = slice(p * H_ATT, (p + 1) * H_ATT)
        suffix = excl[hs, :] + carry
        carry = carry + incl[hs, 0:1]
        kt = kp_refs[p][0, 0].astype(BF16)
        s_parts.append(jnp.dot(qbd, kt, preferred_element_type=F32) + tile_heads(suffix))
    carry_ref[...] = jnp.broadcast_to(carry, carry_ref.shape)
    s = jnp.concatenate(s_parts, axis=1)
    vt = jnp.concatenate([vp_refs[p][0, 0].astype(BF16) for p in range(npg)], axis=1)
    online(s, vt, True)

    @pl.when(j == n_steps - 1)
    def _():
        zk = jnp.zeros((PAGE - SAMPLE_ROWS, D_ATT), F32)
        kb = jnp.concatenate([kn_ref[...], zk], axis=0).astype(BF16)
        vb = jnp.concatenate([vn_ref[...], zk], axis=0).astype(BF16)
        s = _dot_nt(qbd, kb) - tile_heads(cnt_ref[...])
        ok = (lane_i <= row_i // H_ATT) & (lane_i < n_valid)
        online(jnp.where(ok, s, -1e30), vb, False)
        out = acc_ref[...] / l_ref[...][:, 0:1]
        head_of_lane = _iota((H_ATT, D_ATT), 1) // HEAD_DIM
        head_of_row = _iota((H_ATT, D_ATT), 0)
        for t in range(SAMPLE_ROWS):
            blk = out[t * H_ATT:(t + 1) * H_ATT, :]
            o_ref[t:t + 1, :] = jnp.sum(jnp.where(head_of_lane == head_of_row, blk, 0.0),
                                        axis=0, keepdims=True).astype(o_ref.dtype)


def _fox_sample(proj, bf_pad, cache_k, cache_v, cache_lf, page_table, layer, row0, n_seq, n_valid):
    npg = PAGES_PER_STEP
    n_pages = page_table.shape[1]
    n_steps = n_pages // npg
    rb0 = row0 // SAMPLE_ROWS
    rows = SAMPLE_ROWS * H_ATT

    def page_map(p):
        return lambda b, j, pt: (layer, pt[b, n_pages - 1 - (j * npg + p)], 0, 0)

    in_specs = [
        pl.BlockSpec((SAMPLE_ROWS, D_ATT), lambda b, j, pt: (rb0 + b, C_Q // D_ATT)),
        pl.BlockSpec((SAMPLE_ROWS, D_ATT), lambda b, j, pt: (rb0 + b, C_K // D_ATT)),
        pl.BlockSpec((SAMPLE_ROWS, D_ATT), lambda b, j, pt: (rb0 + b, C_V // D_ATT)),
        pl.BlockSpec((SAMPLE_ROWS, LANES), lambda b, j, pt: (rb0 + b, C_F // LANES)),
        pl.BlockSpec((None, 1, LANES), lambda b, j, pt: (layer, 0, 0)),
    ]
    in_specs += [pl.BlockSpec((1, 1, D_ATT, PAGE), page_map(p)) for p in range(npg)]
    in_specs += [pl.BlockSpec((1, 1, D_ATT, PAGE), page_map(p)) for p in range(npg)]
    in_specs += [pl.BlockSpec((1, 1, H_ATT, PAGE), page_map(p)) for p in range(npg)]
    grid_spec = pltpu.PrefetchScalarGridSpec(
        num_scalar_prefetch=1,
        grid=(n_seq, n_steps),
        in_specs=in_specs,
        out_specs=[
            pl.BlockSpec((SAMPLE_ROWS, D_ATT), lambda b, j, pt: (b, 0)),
            pl.BlockSpec((SAMPLE_ROWS, LANES), lambda b, j, pt: (b, 0)),
        ],
        scratch_shapes=[
            pltpu.VMEM((rows, D_ATT), BF16),
            pltpu.VMEM((rows, LANES), F32),
            pltpu.VMEM((rows, LANES), F32),
            pltpu.VMEM((rows, D_ATT), F32),
            pltpu.VMEM((SUBLANES, LANES), F32),
            pltpu.VMEM((SUBLANES, LANES), F32),
            pltpu.VMEM((PAGE, LANES), F32),
        ],
    )
    return pl.pallas_call(
        functools.partial(_fox_sample_kernel, n_steps=n_steps, n_valid=n_valid),
        grid_spec=grid_spec,
        out_shape=[
            jax.ShapeDtypeStruct((n_seq * SAMPLE_ROWS, D_ATT), F32),
            jax.ShapeDtypeStruct((n_seq * SAMPLE_ROWS, LANES), F32),
        ],
        compiler_params=_cparams(("parallel", "arbitrary")),
        name="fox_sample",
    )(page_table, proj, proj, proj, proj, bf_pad,
      *([cache_k] * npg), *([cache_v] * npg), *([cache_lf] * npg))


def _conv_from_ext(ext_ref, w_ref, b_ref, rows):
    out = b_ref[...] + ext_ref[SUBLANES:SUBLANES + rows, :] * w_ref[CONV_W - 1:CONV_W, :]
    for j in range(CONV_W - 1):
        off = SUBLANES - (CONV_W - 1) + j
        out = out + ext_ref[off:off + rows, :] * w_ref[j:j + 1, :]
    return out


def _ssd_kernel(z_ref, xs_ref, bc_ref, dt_ref, hx_ref, hbc_ref, h0_ref,
                cwx_ref, cbx_ref, cwbc_ref, cbbc_ref, dtb_ref, alog_ref, dexp_ref, nw_ref, e_ref, et_ref,
                y_ref, st_ref, extx_ref, extbc_ref, state_ref, *, blk_rows, n_valid):
    c = pl.program_id(1)
    r = CHUNK

    @pl.when(c == 0)
    def _():
        extx_ref[...] = jnp.zeros_like(extx_ref)
        extbc_ref[...] = jnp.zeros_like(extbc_ref)
        extx_ref[0:SUBLANES, :] = hx_ref[0]
        extbc_ref[0:SUBLANES, :] = hbc_ref[0]
        state_ref[...] = h0_ref[0, 0]

    extx_ref[SUBLANES:SUBLANES + blk_rows, :] = xs_ref[...]
    extbc_ref[SUBLANES:SUBLANES + blk_rows, :] = bc_ref[...]
    xs = _silu(_conv_from_ext(extx_ref, cwx_ref, cbx_ref, r))
    bc = _silu(_conv_from_ext(extbc_ref, cwbc_ref, cbbc_ref, r))
    if blk_rows == r:
        extx_ref[0:SUBLANES, :] = extx_ref[r:r + SUBLANES, :]
        extbc_ref[0:SUBLANES, :] = extbc_ref[r:r + SUBLANES, :]
        z = z_ref[...]
        dt_raw = dt_ref[...]
    else:
        z = jnp.concatenate([z_ref[...], jnp.zeros((r - blk_rows, D_SSD), F32)], axis=0)
        dt_raw = jnp.concatenate([dt_ref[...], jnp.zeros((r - blk_rows, LANES), F32)], axis=0)

    dt = _softplus(dt_raw + dtb_ref[...])
    if n_valid < r:
        dt = jnp.where(_iota((r, LANES), 0) < n_valid, dt, 0.0)
    a = -jnp.exp(alog_ref[...])
    acs = _dot_sel_l(_tril_bf16(r), dt * a)
    acs_t = acs.T
    e = e_ref[...]
    acs_x = _dot_sel_r(acs, e)
    dt_x = _dot_sel_r(dt, e)
    xdt = xs * dt_x
    xd = xdt * jnp.exp(acs_x[r - 1:r, :] - acs_x)
    state = state_ref[...]
    state_bf = state.astype(BF16)
    xd_t = xd.T.astype(BF16)
    xdt_bf = xdt.astype(BF16)
    tri = _iota((r, r), 0) >= _iota((r, r), 1)
    upper = _iota((r, LANES), 1) >= HEAD_DIM
    gw = D_SSD // SSD_GROUPS
    heads_per_group = H_SSD // SSD_GROUPS
    y_diag, y_off, new_state = [], [], []
    for g in range(SSD_GROUPS):
        bg = bc[:, g * SSD_STATE:(g + 1) * SSD_STATE].astype(BF16)
        cg = bc[:, D_BC // 2 + g * SSD_STATE:D_BC // 2 + (g + 1) * SSD_STATE].astype(BF16)
        cb = _dot_nt(cg, bg)
        y_off.append(_dot_nt(cg, state_bf[g * gw:(g + 1) * gw, :]))
        new_state.append(jnp.dot(xd_t[g * gw:(g + 1) * gw, :], bg, preferred_element_type=F32))
        for pi in range(heads_per_group // 2):
            lane0 = g * gw + pi * LANES
            pair = xdt_bf[:, lane0:lane0 + LANES]
            acc = None
            for hh in range(2):
                h = g * heads_per_group + 2 * pi + hh
                seg = acs[:, h:h + 1] - acs_t[h:h + 1, :]
                lm = jnp.where(tri, jnp.exp(jnp.where(tri, seg, 0.0)), 0.0)
                mh = (cb * lm).astype(BF16)
                in_head = upper if hh else jnp.logical_not(upper)
                d = jnp.dot(mh, jnp.where(in_head, pair, jnp.zeros_like(pair)), preferred_element_type=F32)
                acc = d if acc is None else acc + d
            y_diag.append(acc)
    y_diag = jnp.concatenate(y_diag, axis=1)
    y_off = jnp.concatenate(y_off, axis=1)
    new_state = jnp.concatenate(new_state, axis=0)
    last_col = _dot_sel_l(et_ref[...], acs_t)[:, r - 1:r]
    state_new = jnp.exp(last_col) * state + new_state
    state_ref[...] = state_new
    st_ref[0, 0] = state_new

    y = y_diag + y_off * jnp.exp(acs_x) + xs * dexp_ref[...]
    yg = y * _silu(z)
    parts = []
    for g in range(SSD_GROUPS):
        p = yg[:, g * gw:(g + 1) * gw]
        parts.append(p * lax.rsqrt(jnp.mean(p * p, axis=-1, keepdims=True) + RMS_EPS))
    out = jnp.concatenate(parts, axis=1) * nw_ref[...]
    y_ref[...] = out[0:blk_rows, :].astype(y_ref.dtype)


def _ssd(proj, hist, h0, state_layer, p, layer, row0, n_seq, n_chunks, blk_rows, n_valid):
    rb0 = row0 // blk_rows
    row_map = lambda col: (lambda b, c: (rb0 + b * n_chunks + c, col))
    const = lambda b, c: (0, 0)
    lp = lambda r, n: pl.BlockSpec((None, r, n), lambda b, c: (layer, 0, 0))
    return pl.pallas_call(
        functools.partial(_ssd_kernel, blk_rows=blk_rows, n_valid=n_valid),
        grid=(n_seq, n_chunks),
        in_specs=[
            pl.BlockSpec((blk_rows, D_SSD), row_map(C_Z // D_SSD)),
            pl.BlockSpec((blk_rows, D_SSD), row_map(C_XS // D_SSD)),
            pl.BlockSpec((blk_rows, D_BC), row_map(C_BC // D_BC)),
            pl.BlockSpec((blk_rows, LANES), row_map(C_DT // LANES)),
            pl.BlockSpec((None, 1, SUBLANES, D_SSD), lambda b, c: (state_layer, b, 0, 0)),
            pl.BlockSpec((None, 1, SUBLANES, D_BC), lambda b, c: (state_layer, b, 0, D_SSD // D_BC)),
            pl.BlockSpec((1, 1, D_SSD, SSD_STATE), lambda b, c: (state_layer, b, 0, 0)),
            lp(CONV_W, D_SSD), lp(1, D_SSD), lp(CONV_W, D_BC), lp(1, D_BC),
            lp(1, LANES), lp(1, LANES), lp(1, D_SSD), lp(1, D_SSD),
            pl.BlockSpec((LANES, D_SSD), const),
            pl.BlockSpec((D_SSD, LANES), const),
        ],
        out_specs=[
            pl.BlockSpec((blk_rows, D_SSD), lambda b, c: (b * n_chunks + c, 0)),
            pl.BlockSpec((1, 1, D_SSD, SSD_STATE), lambda b, c: (0, b, 0, 0)),
        ],
        out_shape=[
            jax.ShapeDtypeStruct((n_seq * n_chunks * blk_rows, D_SSD), _tile_dtype(blk_rows)),
            jax.ShapeDtypeStruct((1, n_seq, D_SSD, SSD_STATE), F32),
        ],
        scratch_shapes=[
            pltpu.VMEM((SUBLANES + CHUNK, D_SSD), F32),
            pltpu.VMEM((SUBLANES + CHUNK, D_BC), F32),
            pltpu.VMEM((D_SSD, SSD_STATE), F32),
        ],
        compiler_params=_cparams(("parallel", "arbitrary")),
        name="ssd",
    )(proj, proj, proj, proj, hist, hist, h0,
      p["cwx"], p["cbx"], p["cwbc"], p["cbbc"], p["dtb"], p["alog"], p["dexp"], p["nw"], p["e"], p["et"])


def _lru_kernel(g_ref, u_ref, hist_ref, h0_ref, cw_ref, cb_ref, wa_ref, ba_ref, wx_ref, bx_ref, lam_ref,
                y_ref, st_ref, ext_ref, hprev_ref, *, blk_rows, n_valid):
    c = pl.program_id(1)
    r = blk_rows

    @pl.when(c == 0)
    def _():
        ext_ref[0:SUBLANES, :] = hist_ref[0]
        hprev_ref[...] = h0_ref[0]

    ext_ref[SUBLANES:SUBLANES + r, :] = u_ref[...]
    u = _conv_from_ext(ext_ref, cw_ref, cb_ref, r)
    ext_ref[0:SUBLANES, :] = ext_ref[r:r + SUBLANES, :]
    ub = u.astype(BF16)
    rg = jax.nn.sigmoid(jnp.dot(ub, wa_ref[...], preferred_element_type=F32) + ba_ref[...])
    ig = jax.nn.sigmoid(jnp.dot(ub, wx_ref[...], preferred_element_type=F32) + bx_ref[...])
    log_a = -LRU_C * rg * _softplus(-lam_ref[...])
    a = jnp.exp(log_a)
    x = jnp.sqrt(-jnp.tanh(log_a) * (a * a + 1.0)) * (ig * u)
    row = _iota((r, D_LRU), 0)
    sh = 1
    while sh < r:
        keep = row >= sh
        a_prev = jnp.where(keep, pltpu.roll(a, sh, axis=0), 1.0)
        x_prev = jnp.where(keep, pltpu.roll(x, sh, axis=0), 0.0)
        x = x + a * x_prev
        a = a * a_prev
        sh *= 2
    hs = x + a * hprev_ref[...]
    hprev_ref[...] = hs[r - 1:r, :]
    st_ref[0] = hs[n_valid - 1:n_valid, :]
    y_ref[...] = (hs * jax.nn.gelu(g_ref[...])).astype(y_ref.dtype)


def _lru(proj, hist, h0, state_layer, p, layer, row0, n_seq, n_chunks, blk_rows, n_valid):
    rb0 = row0 // blk_rows
    row_map = lambda col: (lambda b, c: (rb0 + b * n_chunks + c, col))
    lp = lambda r, n: pl.BlockSpec((None, r, n), lambda b, c: (layer, 0, 0))
    return pl.pallas_call(
        functools.partial(_lru_kernel, blk_rows=blk_rows, n_valid=n_valid),
        grid=(n_seq, n_chunks),
        in_specs=[
            pl.BlockSpec((blk_rows, D_LRU), row_map(C_G // D_LRU)),
            pl.BlockSpec((blk_rows, D_LRU), row_map(C_U // D_LRU)),
            pl.BlockSpec((None, 1, SUBLANES, D_LRU), lambda b, c: (state_layer, b, 0, 0)),
            pl.BlockSpec((None, 1, 1, D_LRU), lambda b, c: (state_layer, b, 0, 0)),
            lp(CONV_W, D_LRU), lp(1, D_LRU), lp(D_LRU, D_LRU), lp(1, D_LRU), lp(D_LRU, D_LRU), lp(1, D_LRU),
            lp(1, D_LRU),
        ],
        out_specs=[
            pl.BlockSpec((blk_rows, D_LRU), lambda b, c: (b * n_chunks + c, 0)),
            pl.BlockSpec((1, 1, D_LRU), lambda b, c: (b, 0, 0)),
        ],
        out_shape=[
            jax.ShapeDtypeStruct((n_seq * n_chunks * blk_rows, D_LRU), _tile_dtype(blk_rows)),
            jax.ShapeDtypeStruct((n_seq, 1, D_LRU), F32),
        ],
        scratch_shapes=[
            pltpu.VMEM((SUBLANES + blk_rows, D_LRU), F32),
            pltpu.VMEM((1, D_LRU), F32),
        ],
        compiler_params=_cparams(("parallel", "arbitrary")),
        name="lru",
    )(proj, proj, hist, h0, p["cw"], p["cb"], p["wa"], p["ba"], p["wx"], p["bx"], p["lam"])


def _hist_tile(h):
    return jnp.pad(h, ((0, 0), (0, 0), (SUBLANES - (CONV_W - 1), 0), (0, 0)))


def _block_diag(w):
    eye = jnp.eye(LRU_BLOCKS, dtype=w.dtype)
    return jnp.einsum("lhij,hg->lhigj", w, eye).reshape(w.shape[0], D_LRU, D_LRU)


def _pick_tile(t, cap):
    best = SUBLANES
    for tm in range(SUBLANES, cap + 1, SUBLANES):
        if t % tm == 0:
            best = tm
    return best


def kernel(x_prompt, x_sample, cache_k, cache_v, cache_logf, page_table, state_ssd, state_ssd_conv, state_lru, state_lru_conv, ffn1_norm, ffn1_w_gate, ffn1_w_up, ffn1_w_down, mix_norm, w_in, fox_b_f, ssd_conv_w, ssd_conv_b, ssd_dt_bias, ssd_a_log, ssd_d, ssd_norm, lru_conv_w, lru_conv_b, lru_w_a, lru_b_a, lru_w_x, lru_b_x, lru_lambda, w_out, ffn2_norm, ffn2_w_gate, ffn2_w_up, ffn2_w_down, final_norm):
    bp, seq, _ = x_prompt.shape
    bd, t_len, _ = x_sample.shape
    depth = w_in.shape[0]
    n_pool = cache_k.shape[1]
    assert seq % CHUNK == 0 and t_len <= SAMPLE_ROWS and t_len >= CONV_W - 1
    assert page_table.shape[1] % PAGES_PER_STEP == 0
    tp = bp * seq
    ts = bd * SAMPLE_ROWS
    t = tp + ts
    tm_ffn = _pick_tile(t, 768)
    tm_proj = _pick_tile(t, 1056)
    tq = 256 if seq % 256 == 0 else CHUNK

    xs_pad = jnp.pad(x_sample, ((0, 0), (0, SAMPLE_ROWS - t_len), (0, 0)))
    x = jnp.concatenate([x_prompt.reshape(tp, D_MODEL), xs_pad.reshape(ts, D_MODEL)], axis=0)

    ck = jnp.transpose(cache_k, (0, 1, 3, 4, 2)).reshape(depth, n_pool, D_ATT, PAGE)
    cv = jnp.transpose(cache_v, (0, 1, 3, 4, 2)).reshape(depth, n_pool, D_ATT, PAGE)
    clf = jnp.transpose(cache_logf, (0, 1, 3, 2))
    e_mat = (jnp.arange(LANES)[:, None] == (jnp.arange(D_SSD)[None, :] // HEAD_DIM)).astype(BF16)

    rows3 = lambda v: v.reshape(depth, 1, -1).astype(F32)
    lanes3 = lambda v: jnp.pad(rows3(v), ((0, 0), (0, 0), (0, LANES - v.shape[-1])))
    bf16 = lambda w: w.astype(BF16)
    ffn1 = (rows3(ffn1_norm), bf16(ffn1_w_gate), bf16(ffn1_w_up), bf16(ffn1_w_down))
    ffn2 = (rows3(ffn2_norm), bf16(ffn2_w_gate), bf16(ffn2_w_up), bf16(ffn2_w_down))
    o_f, o_z, o_xbc = 3 * D_ATT, 3 * D_ATT + H_ATT, 3 * D_ATT + H_ATT + D_SSD
    o_dt = o_xbc + D_SSD + D_BC
    o_g = o_dt + H_SSD
    lane_pad = lambda w: jnp.pad(w, ((0, 0), (0, 0), (0, LANES - w.shape[-1])))
    w_in_p = jnp.concatenate([
        w_in[:, :, o_z:o_z + D_SSD], w_in[:, :, o_xbc:o_xbc + D_SSD + D_BC], w_in[:, :, 0:3 * D_ATT],
        w_in[:, :, o_g:o_g + 2 * D_LRU],
        lane_pad(w_in[:, :, o_f:o_f + H_ATT]), lane_pad(w_in[:, :, o_dt:o_dt + H_SSD]),
    ], axis=2).astype(BF16)
    mix_g = rows3(mix_norm)
    w_out_b = w_out.astype(BF16)
    bf_pad = lanes3(fox_b_f)
    ssd_p = dict(
        cwx=ssd_conv_w[:, :, :D_SSD], cbx=rows3(ssd_conv_b[:, :D_SSD]),
        cwbc=ssd_conv_w[:, :, D_SSD:], cbbc=rows3(ssd_conv_b[:, D_SSD:]),
        dtb=lanes3(ssd_dt_bias), alog=lanes3(ssd_a_log),
        dexp=rows3(jnp.repeat(ssd_d, HEAD_DIM, axis=1)), nw=rows3(ssd_norm), e=e_mat, et=e_mat.T)
    lru_p = dict(
        cw=lru_conv_w, cb=rows3(lru_conv_b),
        wa=_block_diag(lru_w_a).astype(BF16), ba=rows3(lru_b_a),
        wx=_block_diag(lru_w_x).astype(BF16), bx=rows3(lru_b_x), lam=rows3(lru_lambda))
    st_ssd = state_ssd.reshape(depth, bd, D_SSD, SSD_STATE)
    hist_ssd = _hist_tile(state_ssd_conv)
    hist_lru = _hist_tile(state_lru_conv)
    st_lru = state_lru.reshape(depth, bd, 1, D_LRU)
    zeros_ssd = jnp.zeros((1, bp, D_SSD, SSD_STATE), F32)
    zeros_hist_ssd = jnp.zeros((1, bp, SUBLANES, D_SSD + D_BC), F32)
    zeros_hist_lru = jnp.zeros((1, bp, SUBLANES, D_LRU), F32)
    zeros_lru = jnp.zeros((1, bp, 1, D_LRU), F32)
    n_chunks = seq // CHUNK

    outs_p = [[] for _ in range(7)]
    outs_s = [[] for _ in range(7)]
    for l in range(depth):
        last = l == depth - 1
        x = _ffn(x, *ffn1, None, l, tm_ffn)
        proj = _inproj(x, mix_g, w_in_p, l, tm_proj)

        logf_p, c_p, ct_p = _fox_prep(proj, bf_pad, l, bp, seq)
        att_p = _fox_prompt(proj, c_p, ct_p, bp, seq, tq)
        ssd_y_p, ssd_st_p = _ssd(proj, zeros_hist_ssd, zeros_ssd, 0, ssd_p, l, 0, bp, n_chunks, CHUNK, CHUNK)
        lru_y_p, lru_st_p = _lru(proj, zeros_hist_lru, zeros_lru, 0, lru_p, l, 0, bp, n_chunks, CHUNK, CHUNK)

        att_s, logf_s = _fox_sample(proj, bf_pad, ck, cv, clf, page_table, l, tp, bd, t_len)
        ssd_y_s, ssd_st_s = _ssd(proj, hist_ssd, st_ssd, l, ssd_p, l, tp, bd, 1, SAMPLE_ROWS, t_len)
        lru_y_s, lru_st_s = _lru(proj, hist_lru, st_lru, l, lru_p, l, tp, bd, 1, SAMPLE_ROWS, t_len)

        att = jnp.concatenate([att_p, att_s.astype(BF16)], axis=0)
        ssd_y = jnp.concatenate([ssd_y_p, ssd_y_s.astype(BF16)], axis=0)
        lru_y = jnp.concatenate([lru_y_p, lru_y_s.astype(BF16)], axis=0)
        x = _outproj(x, att, ssd_y, lru_y, w_out_b, l, tm_ffn)
        x = _ffn(x, *ffn2, final_norm.reshape(1, D_MODEL) if last else None, l, tm_ffn)

        def cols_p(c0, n, r0=0):
            if r0:
                return jnp.stack([lax.slice(proj, (b * seq + r0, c0), ((b + 1) * seq, c0 + n)) for b in range(bp)])
            return lax.slice(proj, (0, c0), (tp, c0 + n)).reshape(bp, seq, n)

        def cols_s(c0, n, r0, r1):
            return lax.slice(proj, (tp, c0), (t, c0 + n)).reshape(bd, SAMPLE_ROWS, n)[:, r0:r1]

        tail = seq - (CONV_W - 1)
        outs_p[0].append(cols_p(C_K, D_ATT).reshape(bp, seq, H_ATT, HEAD_DIM))
        outs_p[1].append(cols_p(C_V, D_ATT).reshape(bp, seq, H_ATT, HEAD_DIM))
        outs_p[2].append(logf_p[:, :H_ATT].reshape(bp, seq, H_ATT))
        outs_p[3].append(ssd_st_p.reshape(bp, H_SSD, HEAD_DIM, SSD_STATE))
        outs_p[4].append(cols_p(C_XS, D_SSD + D_BC, tail))
        outs_p[5].append(lru_st_p.reshape(bp, D_LRU))
        outs_p[6].append(cols_p(C_U, D_LRU, tail))
        outs_s[0].append(cols_s(C_K, D_ATT, 0, t_len).reshape(bd, t_len, H_ATT, HEAD_DIM))
        outs_s[1].append(cols_s(C_V, D_ATT, 0, t_len).reshape(bd, t_len, H_ATT, HEAD_DIM))
        outs_s[2].append(logf_s.reshape(bd, SAMPLE_ROWS, LANES)[:, :t_len, :H_ATT])
        outs_s[3].append(ssd_st_s.reshape(bd, H_SSD, HEAD_DIM, SSD_STATE))
        outs_s[4].append(cols_s(C_XS, D_SSD + D_BC, t_len - (CONV_W - 1), t_len))
        outs_s[5].append(lru_st_s.reshape(bd, D_LRU))
        outs_s[6].append(cols_s(C_U, D_LRU, t_len - (CONV_W - 1), t_len))

    y_prompt = x[:tp].reshape(bp, seq, D_MODEL)
    y_sample = x[tp:].reshape(bd, SAMPLE_ROWS, D_MODEL)[:, :t_len]
    sp = [jnp.stack(o) for o in outs_p]
    ss = [jnp.stack(o) for o in outs_s]
    return (y_prompt, y_sample, sp[0], sp[1], sp[2], ss[0], ss[1], ss[2], sp[3], sp[4], ss[3], ss[4],
            sp[5], sp[6], ss[5], ss[6])
```

```python
import functools
import math

import numpy as np
import jax
import jax.numpy as jnp
from jax import lax
from jax.experimental import pallas as pl
from jax.experimental.pallas import tpu as pltpu

F32 = jnp.float32
BF16 = jnp.bfloat16

D_MODEL = 2048
HEAD_DIM = 64
D_ATT = 512
D_SSD = 1024
D_LRU = 512
H_ATT = 8
H_SSD = 16
SSD_GROUPS = 2
SSD_STATE = 128
D_BC = 2 * SSD_GROUPS * SSD_STATE
CONV_W = 4
LRU_BLOCKS = 8
LRU_BW = 64
LRU_C = 8.0
D_FF = 5504
PAGE = 128
RMS_EPS = 1e-6

LANES = 128
SUBLANES = 8
VMEM_LIMIT = 56 * 1024 * 1024

CHUNK = 128
LRU_ROWS = 256
SAMPLE_ROWS = 8
FF_TILE = 512
PAGES_PER_STEP = 32

C_Z = 0
C_XS = 1024
C_BC = 2048
C_Q = 2560
C_K = 3072
C_V = 3584
C_G = 4096
C_U = 4608
C_F = 5120
C_DT = 5248
N_PROJ = 5376
PROJ_TILE = 1792


def _cparams(sem):
    return pltpu.CompilerParams(dimension_semantics=sem, vmem_limit_bytes=VMEM_LIMIT)


def _rms(x, g):
    return x * lax.rsqrt(jnp.mean(x * x, axis=-1, keepdims=True) + RMS_EPS) * g


def _silu(x):
    return x * jax.nn.sigmoid(x)


def _softplus(x):
    return jnp.maximum(x, 0.0) + jnp.log1p(jnp.exp(-jnp.abs(x)))


def _split3(a):
    a1 = a.astype(BF16)
    r1 = a - a1.astype(F32)
    a2 = r1.astype(BF16)
    r2 = r1 - a2.astype(F32)
    return a1, a2, r2.astype(BF16)


def _dot_sel_r(a, sel):
    a1, a2, a3 = _split3(a)
    d = lambda p: jnp.dot(p, sel, preferred_element_type=F32)
    return d(a1) + d(a2) + d(a3)


def _dot_sel_l(sel, a):
    a1, a2, a3 = _split3(a)
    d = lambda p: jnp.dot(sel, p, preferred_element_type=F32)
    return d(a1) + d(a2) + d(a3)


def _dot_nt(a, b):
    return lax.dot_general(a, b, (((1,), (1,)), ((), ())), preferred_element_type=F32)


def _tile_dtype(rows):
    return BF16 if rows % (2 * SUBLANES) == 0 else F32


def _iota(shape, dim):
    return lax.broadcasted_iota(jnp.int32, shape, dim)


def _tril_bf16(n):
    return (_iota((n, n), 0) >= _iota((n, n), 1)).astype(BF16)


def _ffn_kernel(*refs, n_main, n_tail, final):
    x_ref, g_ref = refs[0:2]
    main = refs[2:5]
    tails = [refs[5 + 3 * k:8 + 3 * k] for k in range(n_tail)]
    rest = refs[5 + 3 * n_tail:]
    if final:
        fg_ref, o_ref, xn_ref = rest
    else:
        o_ref, xn_ref = rest
    j = pl.program_id(1)

    def part(wg_ref, wu_ref, wd_ref):
        xn = xn_ref[...]
        a = jnp.dot(xn, wg_ref[...], preferred_element_type=F32)
        b = jnp.dot(xn, wu_ref[...], preferred_element_type=F32)
        h = (_silu(a) * b).astype(BF16)
        return jnp.dot(h, wd_ref[...], preferred_element_type=F32)

    @pl.when(j == 0)
    def _():
        xn_ref[...] = _rms(x_ref[...], g_ref[...]).astype(BF16)
        acc = jnp.zeros(o_ref.shape, F32)
        for tail in tails:
            acc = acc + part(*tail)
        o_ref[...] = acc

    o_ref[...] += part(*main)

    @pl.when(j == n_main - 1)
    def _():
        y = x_ref[...] + 0.5 * o_ref[...]
        if final:
            y = _rms(y, fg_ref[...])
        o_ref[...] = y


def _ffn_tail_tiles():
    tiles, start, width = [], (D_FF // FF_TILE) * FF_TILE, FF_TILE // 2
    while start < D_FF:
        if start + width <= D_FF:
            assert width % LANES == 0 and start % width == 0
            tiles.append((width, start // width))
            start += width
        width //= 2
    return tiles


def _ffn(x, g, wg, wu, wd, final_g, layer, tm):
    t = x.shape[0]
    n_main = D_FF // FF_TILE
    tails = _ffn_tail_tiles()
    final = final_g is not None

    def weight_specs(width, col):
        return [pl.BlockSpec((None, D_MODEL, width), lambda i, j: (layer, 0, col(j))),
                pl.BlockSpec((None, D_MODEL, width), lambda i, j: (layer, 0, col(j))),
                pl.BlockSpec((None, width, D_MODEL), lambda i, j: (layer, col(j), 0))]

    in_specs = [
        pl.BlockSpec((tm, D_MODEL), lambda i, j: (i, 0)),
        pl.BlockSpec((None, 1, D_MODEL), lambda i, j: (layer, 0, 0)),
    ] + weight_specs(FF_TILE, lambda j: j)
    args = [x, g, wg, wu, wd]
    for width, blk in tails:
        in_specs += weight_specs(width, lambda j, blk=blk: blk)
        args += [wg, wu, wd]
    if final:
        in_specs.append(pl.BlockSpec((1, D_MODEL), lambda i, j: (0, 0)))
        args.append(final_g)
    return pl.pallas_call(
        functools.partial(_ffn_kernel, n_main=n_main, n_tail=len(tails), final=final),
        grid=(t // tm, n_main),
        in_specs=in_specs,
        out_specs=pl.BlockSpec((tm, D_MODEL), lambda i, j: (i, 0)),
        out_shape=jax.ShapeDtypeStruct((t, D_MODEL), F32),
        scratch_shapes=[pltpu.VMEM((tm, D_MODEL), BF16)],
        compiler_params=_cparams(("parallel", "arbitrary")),
        name="ffn",
    )(*args)


def _inproj_kernel(x_ref, g_ref, w_ref, o_ref, xn_ref):
    @pl.when(pl.program_id(1) == 0)
    def _():
        xn_ref[...] = _rms(x_ref[...], g_ref[...]).astype(BF16)

    o_ref[...] = jnp.dot(xn_ref[...], w_ref[...], preferred_element_type=F32)


def _inproj(x, g, w, layer, tm):
    t = x.shape[0]
    return pl.pallas_call(
        _inproj_kernel,
        grid=(t // tm, N_PROJ // PROJ_TILE),
        in_specs=[
            pl.BlockSpec((tm, D_MODEL), lambda i, j: (i, 0)),
            pl.BlockSpec((None, 1, D_MODEL), lambda i, j: (layer, 0, 0)),
            pl.BlockSpec((None, D_MODEL, PROJ_TILE), lambda i, j: (layer, 0, j)),
        ],
        out_specs=pl.BlockSpec((tm, PROJ_TILE), lambda i, j: (i, j)),
        out_shape=jax.ShapeDtypeStruct((t, N_PROJ), F32),
        scratch_shapes=[pltpu.VMEM((tm, D_MODEL), BF16)],
        compiler_params=_cparams(("parallel", "arbitrary")),
        name="inproj",
    )(x, g, w)


def _outproj_kernel(x_ref, att_p_ref, ssd_p_ref, lru_p_ref, att_s_ref, ssd_s_ref, lru_s_ref, w_ref, o_ref, *, n_p):
    def mix(att, ssd, lru):
        acc = jnp.dot(att, w_ref[0:D_ATT, :], preferred_element_type=F32)
        acc += jnp.dot(ssd, w_ref[D_ATT:D_ATT + D_SSD, :], preferred_element_type=F32)
        acc += jnp.dot(lru, w_ref[D_ATT + D_SSD:, :], preferred_element_type=F32)
        o_ref[...] = x_ref[...] + acc

    is_prompt = pl.program_id(0) < n_p

    @pl.when(is_prompt)
    def _():
        mix(att_p_ref[...], ssd_p_ref[...], lru_p_ref[...])

    @pl.when(jnp.logical_not(is_prompt))
    def _():
        mix(att_s_ref[...].astype(BF16), ssd_s_ref[...].astype(BF16), lru_s_ref[...].astype(BF16))


def _outproj(x, mixed_p, mixed_s, w, layer, tm, nq):
    t = x.shape[0]
    att_p, ssd_p, lru_p = mixed_p
    att_s, ssd_s, lru_s = mixed_s
    n_p = ssd_p.shape[0] // tm
    assert ssd_p.shape[0] % tm == 0 and ssd_s.shape[0] % tm == 0 and att_p.shape[1] * 2 == ssd_p.shape[0]
    p_blk = lambda i: jnp.minimum(i, n_p - 1)
    s_blk = lambda i: jnp.maximum(i - n_p, 0)
    return pl.pallas_call(
        functools.partial(_outproj_kernel, n_p=n_p),
        grid=(t // tm,),
        in_specs=[
            pl.BlockSpec((tm, D_MODEL), lambda i: (i, 0)),
            pl.BlockSpec((None, tm, D_ATT), lambda i: (*_att_block(p_blk(i), nq), 0)),
            pl.BlockSpec((tm, D_SSD), lambda i: (p_blk(i), 0)),
            pl.BlockSpec((tm, D_LRU), lambda i: (p_blk(i), 0)),
            pl.BlockSpec((tm, D_ATT), lambda i: (s_blk(i), 0)),
            pl.BlockSpec((tm, D_SSD), lambda i: (s_blk(i), 0)),
            pl.BlockSpec((tm, D_LRU), lambda i: (s_blk(i), 0)),
            pl.BlockSpec((None, D_MODEL, D_MODEL), lambda i: (layer, 0, 0)),
        ],
        out_specs=pl.BlockSpec((tm, D_MODEL), lambda i: (i, 0)),
        out_shape=jax.ShapeDtypeStruct((t, D_MODEL), F32),
        compiler_params=_cparams(("parallel",)),
        name="outproj",
    )(x, att_p, ssd_p, lru_p, att_s, ssd_s, lru_s, w)


def _fox_prep_kernel(f_ref, bf_ref, logf_ref, c_ref, ct_ref, *, n_blk):
    tril = _tril_bf16(CHUNK)
    carry = jnp.zeros((1, LANES), F32)
    for i in range(n_blk):
        rows = slice(i * CHUNK, (i + 1) * CHUNK)
        lf = -_softplus(-(f_ref[rows, :] + bf_ref[...]))
        logf_ref[rows, :] = lf
        cs = _dot_sel_l(tril, lf) + carry
        carry = cs[CHUNK - 1:CHUNK, :]
        c_ref[rows, :] = cs
        ct_ref[0, :, rows] = cs.T[0:SUBLANES, :]


def _fox_prep(proj, bf_pad, layer, n_seq, seq):
    n_blk = seq // CHUNK
    return pl.pallas_call(
        functools.partial(_fox_prep_kernel, n_blk=n_blk),
        grid=(n_seq,),
        in_specs=[
            pl.BlockSpec((seq, LANES), lambda b: (b, C_F // LANES)),
            pl.BlockSpec((None, 1, LANES), lambda b: (layer, 0, 0)),
        ],
        out_specs=[
            pl.BlockSpec((seq, LANES), lambda b: (b, 0)),
            pl.BlockSpec((seq, LANES), lambda b: (b, 0)),
            pl.BlockSpec((1, SUBLANES, seq), lambda b: (b, 0, 0)),
        ],
        out_shape=[
            jax.ShapeDtypeStruct((n_seq * seq, LANES), F32),
            jax.ShapeDtypeStruct((n_seq * seq, LANES), F32),
            jax.ShapeDtypeStruct((n_seq, SUBLANES, seq), F32),
        ],
        compiler_params=_cparams(("parallel",)),
        name="fox_prep",
    )(proj, bf_pad)


def _fox_prompt_kernel(qa_ref, qb_ref, k_ref, v_ref, ca_ref, cb_ref, ct_ref, o_ref, *, tq, nq):
    hp = pl.program_id(1)
    pair = pl.program_id(2)
    scale = 1.0 / math.sqrt(HEAD_DIM)
    lane = _iota((tq, LANES), 1)
    upper = lane >= HEAD_DIM
    tri = _iota((tq, tq), 0) >= _iota((tq, tq), 1)

    def attend(q, c_blk, n):
        w = n * tq
        kb = k_ref[0:w, :].astype(BF16)
        v = v_ref[0:w, :]
        upper_w = _iota((w, LANES), 1) >= HEAD_DIM
        outs = []
        for e in range(2):
            h = 2 * hp + e
            in_head = upper if e else jnp.logical_not(upper)
            qm = (jnp.where(in_head, q, 0.0) * scale).astype(BF16)
            c_col = jnp.sum(jnp.where(lane == h, c_blk, 0.0), axis=1, keepdims=True)
            c_row = ct_ref[0, pl.ds(h, 1), 0:w]
            s = _dot_nt(qm, kb) + (c_col - c_row)
            diag = jnp.where(tri, s[:, w - tq:], -1e30)
            s = diag if n == 1 else jnp.concatenate([s[:, :w - tq], diag], axis=1)
            p = jnp.exp(s - jnp.max(s, axis=1, keepdims=True)).astype(BF16)
            in_head_w = upper_w if e else jnp.logical_not(upper_w)
            acc = jnp.dot(p, jnp.where(in_head_w, v, 1.0).astype(BF16), preferred_element_type=F32)
            den = acc[:, 0:1] if e else acc[:, HEAD_DIM:HEAD_DIM + 1]
            outs.append(acc / den)
        return jnp.where(upper, outs[1], outs[0])

    for p in range(nq // 2):
        @pl.when(pair == p)
        def _(p=p):
            o_ref[0] = attend(qa_ref[...], ca_ref[...], p + 1).astype(o_ref.dtype)
            o_ref[1] = attend(qb_ref[...], cb_ref[...], nq - p).astype(o_ref.dtype)


def _fox_prompt(proj, c, ct, n_seq, seq, tq):
    nq = seq // tq
    half = nq // 2
    a_blk = lambda b, p: b * nq + p
    b_blk = lambda b, p: b * nq + (nq - 1 - p)
    return pl.pallas_call(
        functools.partial(_fox_prompt_kernel, tq=tq, nq=nq),
        grid=(n_seq, H_ATT // 2, half),
        in_specs=[
            pl.BlockSpec((tq, LANES), lambda b, hp, p: (a_blk(b, p), C_Q // LANES + hp)),
            pl.BlockSpec((tq, LANES), lambda b, hp, p: (b_blk(b, p), C_Q // LANES + hp)),
            pl.BlockSpec((seq, LANES), lambda b, hp, p: (b, C_K // LANES + hp)),
            pl.BlockSpec((seq, LANES), lambda b, hp, p: (b, C_V // LANES + hp)),
            pl.BlockSpec((tq, LANES), lambda b, hp, p: (a_blk(b, p), 0)),
            pl.BlockSpec((tq, LANES), lambda b, hp, p: (b_blk(b, p), 0)),
            pl.BlockSpec((1, SUBLANES, seq), lambda b, hp, p: (b, 0, 0)),
        ],
        out_specs=pl.BlockSpec((2, tq, LANES), lambda b, hp, p: (0, b * half + p, hp)),
        out_shape=jax.ShapeDtypeStruct((2, n_seq * half * tq, D_ATT), BF16),
        compiler_params=_cparams(("parallel", "parallel", "arbitrary")),
        name="fox_prompt",
    )(proj, proj, proj, proj, c, c, ct)


def _att_block(r, nq):
    half = nq // 2
    b, qb = r // nq, r % nq
    second = qb >= half
    return jnp.where(second, 1, 0), b * half + jnp.where(second, nq - 1 - qb, qb)


def _fox_sample_kernel(pt_ref, *refs, n_steps, n_valid):
    del pt_ref
    npg = PAGES_PER_STEP
    q_ref, kn_ref, vn_ref, f_ref, bf_ref = refs[0:5]
    kp_refs = refs[5:5 + npg]
    vp_refs = refs[5 + npg:5 + 2 * npg]
    lf_refs = refs[5 + 2 * npg:5 + 3 * npg]
    o_ref, logf_ref = refs[5 + 3 * npg:7 + 3 * npg]
    qbd_ref, m_ref, l_ref, acc_ref, carry_ref, cnt_ref, pad_ref = refs[7 + 3 * npg:]
    j = pl.program_id(1)
    rows = SAMPLE_ROWS * H_ATT
    scale = 1.0 / math.sqrt(HEAD_DIM)
    row_i = _iota((rows, LANES), 0)
    lane_i = _iota((rows, LANES), 1)

    def rep_tokens(x):
        return jnp.concatenate(
            [jnp.broadcast_to(x[t:t + 1, :], (H_ATT, x.shape[1])) for t in range(SAMPLE_ROWS)], axis=0)

    def tile_heads(x):
        return jnp.concatenate([x] * SAMPLE_ROWS, axis=0)

    @pl.when(j == 0)
    def _():
        q = q_ref[...]
        head_of_lane = _iota((rows, D_ATT), 1) // HEAD_DIM
        head_of_row = _iota((rows, D_ATT), 0) % H_ATT
        qbd_ref[...] = (jnp.where(head_of_lane == head_of_row, rep_tokens(q), 0.0) * scale).astype(BF16)
        lf = -_softplus(-(f_ref[...] + bf_ref[...]))
        logf_ref[...] = lf
        ri = _iota((SAMPLE_ROWS, LANES), 0)
        cn = lf
        sh = 1
        while sh < SAMPLE_ROWS:
            cn = cn + jnp.where(ri >= sh, pltpu.roll(cn, sh, axis=0), 0.0)
            sh *= 2
        pad_ref[...] = jnp.zeros_like(pad_ref)
        pad_ref[0:SAMPLE_ROWS, :] = cn
        cnt_ref[...] = pad_ref[...].T[0:SUBLANES, :]
        m_ref[...] = jnp.full_like(m_ref, -1e30)
        l_ref[...] = jnp.zeros_like(l_ref)
        acc_ref[...] = jnp.zeros_like(acc_ref)
        carry_ref[...] = jnp.zeros_like(carry_ref)

    qbd = qbd_ref[...]

    def online(s, v_bf16, v_transposed):
        m = m_ref[...][:, 0:1]
        m_new = jnp.maximum(m, jnp.max(s, axis=1, keepdims=True))
        alpha = jnp.exp(m - m_new)
        p = jnp.exp(s - m_new)
        l_ref[...] = alpha * l_ref[...] + jnp.sum(p, axis=1, keepdims=True)
        pb = p.astype(BF16)
        pv = _dot_nt(pb, v_bf16) if v_transposed else jnp.dot(pb, v_bf16, preferred_element_type=F32)
        acc_ref[...] = alpha * acc_ref[...] + pv
        m_ref[...] = jnp.broadcast_to(m_new, m_ref.shape)

    carry = carry_ref[...][:, 0:1]
    lft = jnp.concatenate([lf_refs[p][0, 0] for p in range(npg)], axis=0)
    later = (_iota((PAGE, PAGE), 0) >= _iota((PAGE, PAGE), 1)).astype(BF16)
    incl = _dot_sel_r(lft, later)
    excl = incl - lft
    s_parts = []
    for p in range(npg):
        hs = slice(p * H_ATT, (p + 1) * H_ATT)
        suffix = excl[hs, :] + carry
        carry = carry + incl[hs, 0:1]
        kt = kp_refs[p][0, 0].astype(BF16)
        s_parts.append(jnp.dot(qbd, kt, preferred_element_type=F32) + tile_heads(suffix))
    carry_ref[...] = jnp.broadcast_to(carry, carry_ref.shape)
    s = jnp.concatenate(s_parts, axis=1)
    vt = jnp.concatenate([vp_refs[p][0, 0].astype(BF16) for p in range(npg)], axis=1)
    online(s, vt, True)

    @pl.when(j == n_steps - 1)
    def _():
        zk = jnp.zeros((PAGE - SAMPLE_ROWS, D_ATT), F32)
        kb = jnp.concatenate([kn_ref[...], zk], axis=0).astype(BF16)
        vb = jnp.concatenate([vn_ref[...], zk], axis=0).astype(BF16)
        s = _dot_nt(qbd, kb) - tile_heads(cnt_ref[...])
        ok = (lane_i <= row_i // H_ATT) & (lane_i < n_valid)
        online(jnp.where(ok, s, -1e30), vb, False)
        out = acc_ref[...] / l_ref[...][:, 0:1]
        head_of_lane = _iota((H_ATT, D_ATT), 1) // HEAD_DIM
        head_of_row = _iota((H_ATT, D_ATT), 0)
        for t in range(SAMPLE_ROWS):
            blk = out[t * H_ATT:(t + 1) * H_ATT, :]
            o_ref[t:t + 1, :] = jnp.sum(jnp.where(head_of_lane == head_of_row, blk, 0.0),
                                        axis=0, keepdims=True).astype(o_ref.dtype)


def _fox_sample(proj, bf_pad, cache_k, cache_v, cache_lf, page_table, layer, row0, n_seq, n_valid):
    npg = PAGES_PER_STEP
    n_pages = page_table.shape[1]
    n_steps = n_pages // npg
    rb0 = row0 // SAMPLE_ROWS
    rows = SAMPLE_ROWS * H_ATT

    def page_map(p):
        return lambda b, j, pt: (layer, pt[b, n_pages - 1 - (j * npg + p)], 0, 0)

    in_specs = [
        pl.BlockSpec((SAMPLE_ROWS, D_ATT), lambda b, j, pt: (rb0 + b, C_Q // D_ATT)),
        pl.BlockSpec((SAMPLE_ROWS, D_ATT), lambda b, j, pt: (rb0 + b, C_K // D_ATT)),
        pl.BlockSpec((SAMPLE_ROWS, D_ATT), lambda b, j, pt: (rb0 + b, C_V // D_ATT)),
        pl.BlockSpec((SAMPLE_ROWS, LANES), lambda b, j, pt: (rb0 + b, C_F // LANES)),
        pl.BlockSpec((None, 1, LANES), lambda b, j, pt: (layer, 0, 0)),
    ]
    in_specs += [pl.BlockSpec((1, 1, D_ATT, PAGE), page_map(p)) for p in range(npg)]
    in_specs += [pl.BlockSpec((1, 1, D_ATT, PAGE), page_map(p)) for p in range(npg)]
    in_specs += [pl.BlockSpec((1, 1, H_ATT, PAGE), page_map(p)) for p in range(npg)]
    grid_spec = pltpu.PrefetchScalarGridSpec(
        num_scalar_prefetch=1,
        grid=(n_seq, n_steps),
        in_specs=in_specs,
        out_specs=[
            pl.BlockSpec((SAMPLE_ROWS, D_ATT), lambda b, j, pt: (b, 0)),
            pl.BlockSpec((SAMPLE_ROWS, LANES), lambda b, j, pt: (b, 0)),
        ],
        scratch_shapes=[
            pltpu.VMEM((rows, D_ATT), BF16),
            pltpu.VMEM((rows, LANES), F32),
            pltpu.VMEM((rows, LANES), F32),
            pltpu.VMEM((rows, D_ATT), F32),
            pltpu.VMEM((SUBLANES, LANES), F32),
            pltpu.VMEM((SUBLANES, LANES), F32),
            pltpu.VMEM((PAGE, LANES), F32),
        ],
    )
    return pl.pallas_call(
        functools.partial(_fox_sample_kernel, n_steps=n_steps, n_valid=n_valid),
        grid_spec=grid_spec,
        out_shape=[
            jax.ShapeDtypeStruct((n_seq * SAMPLE_ROWS, D_ATT), F32),
            jax.ShapeDtypeStruct((n_seq * SAMPLE_ROWS, LANES), F32),
        ],
        compiler_params=_cparams(("parallel", "arbitrary")),
        name="fox_sample",
    )(page_table, proj, proj, proj, proj, bf_pad,
      *([cache_k] * npg), *([cache_v] * npg), *([cache_lf] * npg))


def _conv_from_ext(ext_ref, w_ref, b_ref, rows):
    out = b_ref[...] + ext_ref[SUBLANES:SUBLANES + rows, :] * w_ref[CONV_W - 1:CONV_W, :]
    for j in range(CONV_W - 1):
        off = SUBLANES - (CONV_W - 1) + j
        out = out + ext_ref[off:off + rows, :] * w_ref[j:j + 1, :]
    return out


def _ssd_kernel(z_ref, xs_ref, bc_ref, dt_ref, hx_ref, hbc_ref, h0_ref,
                cwx_ref, cbx_ref, cwbc_ref, cbbc_ref, dtb_ref, alog_ref, dexp_ref, nw_ref, e_ref, et_ref,
                y_ref, st_ref, extx_ref, extbc_ref, state_ref, *, blk_rows, n_valid):
    c = pl.program_id(1)
    r = CHUNK

    @pl.when(c == 0)
    def _():
        extx_ref[...] = jnp.zeros_like(extx_ref)
        extbc_ref[...] = jnp.zeros_like(extbc_ref)
        extx_ref[0:SUBLANES, :] = hx_ref[0]
        extbc_ref[0:SUBLANES, :] = hbc_ref[0]
        state_ref[...] = h0_ref[0, 0]

    extx_ref[SUBLANES:SUBLANES + blk_rows, :] = xs_ref[...]
    extbc_ref[SUBLANES:SUBLANES + blk_rows, :] = bc_ref[...]
    xs = _silu(_conv_from_ext(extx_ref, cwx_ref, cbx_ref, r))
    bc = _silu(_conv_from_ext(extbc_ref, cwbc_ref, cbbc_ref, r))
    if blk_rows == r:
        extx_ref[0:SUBLANES, :] = extx_ref[r:r + SUBLANES, :]
        extbc_ref[0:SUBLANES, :] = extbc_ref[r:r + SUBLANES, :]
        z = z_ref[...]
        dt_raw = dt_ref[...]
    else:
        z = jnp.concatenate([z_ref[...], jnp.zeros((r - blk_rows, D_SSD), F32)], axis=0)
        dt_raw = jnp.concatenate([dt_ref[...], jnp.zeros((r - blk_rows, LANES), F32)], axis=0)

    dt = _softplus(dt_raw + dtb_ref[...])
    if n_valid < r:
        dt = jnp.where(_iota((r, LANES), 0) < n_valid, dt, 0.0)
    a = -jnp.exp(alog_ref[...])
    acs = _dot_sel_l(_tril_bf16(r), dt * a)
    acs_t = acs.T
    e = e_ref[...]
    acs_x = _dot_sel_r(acs, e)
    dt_x = _dot_sel_r(dt, e)
    xdt = xs * dt_x
    xd = xdt * jnp.exp(acs_x[r - 1:r, :] - acs_x)
    state = state_ref[...]
    state_bf = state.astype(BF16)
    xd_t = xd.T.astype(BF16)
    xdt_bf = xdt.astype(BF16)
    tri = _iota((r, r), 0) >= _iota((r, r), 1)
    upper = _iota((r, LANES), 1) >= HEAD_DIM
    gw = D_SSD // SSD_GROUPS
    heads_per_group = H_SSD // SSD_GROUPS
    y_diag, y_off, new_state = [], [], []
    for g in range(SSD_GROUPS):
        bg = bc[:, g * SSD_STATE:(g + 1) * SSD_STATE].astype(BF16)
        cg = bc[:, D_BC // 2 + g * SSD_STATE:D_BC // 2 + (g + 1) * SSD_STATE].astype(BF16)
        cb = _dot_nt(cg, bg)
        y_off.append(_dot_nt(cg, state_bf[g * gw:(g + 1) * gw, :]))
        new_state.append(jnp.dot(xd_t[g * gw:(g + 1) * gw, :], bg, preferred_element_type=F32))
        for pi in range(heads_per_group // 2):
            lane0 = g * gw + pi * LANES
            pair = xdt_bf[:, lane0:lane0 + LANES]
            acc = None
            for hh in range(2):
                h = g * heads_per_group + 2 * pi + hh
                seg = acs[:, h:h + 1] - acs_t[h:h + 1, :]
                lm = jnp.where(tri, jnp.exp(seg), 0.0)
                mh = (cb * lm).astype(BF16)
                in_head = upper if hh else jnp.logical_not(upper)
                d = jnp.dot(mh, jnp.where(in_head, pair, jnp.zeros_like(pair)), preferred_element_type=F32)
                acc = d if acc is None else acc + d
            y_diag.append(acc)
    y_diag = jnp.concatenate(y_diag, axis=1)
    y_off = jnp.concatenate(y_off, axis=1)
    new_state = jnp.concatenate(new_state, axis=0)
    last_col = _dot_sel_l(et_ref[...], acs_t)[:, r - 1:r]
    state_new = jnp.exp(last_col) * state + new_state
    state_ref[...] = state_new
    st_ref[0, 0] = state_new

    y = y_diag + y_off * jnp.exp(acs_x) + xs * dexp_ref[...]
    yg = y * _silu(z)
    parts = []
    for g in range(SSD_GROUPS):
        p = yg[:, g * gw:(g + 1) * gw]
        parts.append(p * lax.rsqrt(jnp.mean(p * p, axis=-1, keepdims=True) + RMS_EPS))
    out = jnp.concatenate(parts, axis=1) * nw_ref[...]
    y_ref[...] = out[0:blk_rows, :].astype(y_ref.dtype)


def _ssd(proj, hist, h0, state_layer, p, layer, row0, n_seq, n_chunks, blk_rows, n_valid):
    rb0 = row0 // blk_rows
    row_map = lambda col: (lambda b, c: (rb0 + b * n_chunks + c, col))
    const = lambda b, c: (0, 0)
    lp = lambda r, n: pl.BlockSpec((None, r, n), lambda b, c: (layer, 0, 0))
    return pl.pallas_call(
        functools.partial(_ssd_kernel, blk_rows=blk_rows, n_valid=n_valid),
        grid=(n_seq, n_chunks),
        in_specs=[
            pl.BlockSpec((blk_rows, D_SSD), row_map(C_Z // D_SSD)),
            pl.BlockSpec((blk_rows, D_SSD), row_map(C_XS // D_SSD)),
            pl.BlockSpec((blk_rows, D_BC), row_map(C_BC // D_BC)),
            pl.BlockSpec((blk_rows, LANES), row_map(C_DT // LANES)),
            pl.BlockSpec((None, 1, SUBLANES, D_SSD), lambda b, c: (state_layer, b, 0, 0)),
            pl.BlockSpec((None, 1, SUBLANES, D_BC), lambda b, c: (state_layer, b, 0, D_SSD // D_BC)),
            pl.BlockSpec((1, 1, D_SSD, SSD_STATE), lambda b, c: (state_layer, b, 0, 0)),
            lp(CONV_W, D_SSD), lp(1, D_SSD), lp(CONV_W, D_BC), lp(1, D_BC),
            lp(1, LANES), lp(1, LANES), lp(1, D_SSD), lp(1, D_SSD),
            pl.BlockSpec((LANES, D_SSD), const),
            pl.BlockSpec((D_SSD, LANES), const),
        ],
        out_specs=[
            pl.BlockSpec((blk_rows, D_SSD), lambda b, c: (b * n_chunks + c, 0)),
            pl.BlockSpec((1, 1, D_SSD, SSD_STATE), lambda b, c: (0, b, 0, 0)),
        ],
        out_shape=[
            jax.ShapeDtypeStruct((n_seq * n_chunks * blk_rows, D_SSD), _tile_dtype(blk_rows)),
            jax.ShapeDtypeStruct((1, n_seq, D_SSD, SSD_STATE), F32),
        ],
        scratch_shapes=[
            pltpu.VMEM((SUBLANES + CHUNK, D_SSD), F32),
            pltpu.VMEM((SUBLANES + CHUNK, D_BC), F32),
            pltpu.VMEM((D_SSD, SSD_STATE), F32),
        ],
        compiler_params=_cparams(("parallel", "arbitrary")),
        name="ssd",
    )(proj, proj, proj, proj, hist, hist, h0,
      p["cwx"], p["cbx"], p["cwbc"], p["cbbc"], p["dtb"], p["alog"], p["dexp"], p["nw"], p["e"], p["et"])


def _lru_kernel(g_ref, u_ref, hist_ref, h0_ref, cw_ref, cb_ref, wa_ref, ba_ref, wx_ref, bx_ref, lam_ref,
                y_ref, st_ref, ext_ref, hprev_ref, *, blk_rows, n_valid):
    c = pl.program_id(1)
    r = blk_rows

    @pl.when(c == 0)
    def _():
        ext_ref[0:SUBLANES, :] = hist_ref[0]
        hprev_ref[...] = h0_ref[0]

    ext_ref[SUBLANES:SUBLANES + r, :] = u_ref[...]
    u = _conv_from_ext(ext_ref, cw_ref, cb_ref, r)
    ext_ref[0:SUBLANES, :] = ext_ref[r:r + SUBLANES, :]
    ub = u.astype(BF16)
    rg = jax.nn.sigmoid(jnp.dot(ub, wa_ref[...], preferred_element_type=F32) + ba_ref[...])
    ig = jax.nn.sigmoid(jnp.dot(ub, wx_ref[...], preferred_element_type=F32) + bx_ref[...])
    log_a = -LRU_C * rg * _softplus(-lam_ref[...])
    a = jnp.exp(log_a)
    x = jnp.sqrt(-jnp.tanh(log_a) * (a * a + 1.0)) * (ig * u)
    row = _iota((r, D_LRU), 0)
    sh = 1
    while sh < r:
        keep = row >= sh
        a_prev = jnp.where(keep, pltpu.roll(a, sh, axis=0), 1.0)
        x_prev = jnp.where(keep, pltpu.roll(x, sh, axis=0), 0.0)
        x = x + a * x_prev
        a = a * a_prev
        sh *= 2
    hs = x + a * hprev_ref[...]
    hprev_ref[...] = hs[r - 1:r, :]
    st_ref[0] = hs[n_valid - 1:n_valid, :]
    y_ref[...] = (hs * jax.nn.gelu(g_ref[...])).astype(y_ref.dtype)


def _lru(proj, hist, h0, state_layer, p, layer, row0, n_seq, n_chunks, blk_rows, n_valid):
    rb0 = row0 // blk_rows
    row_map = lambda col: (lambda b, c: (rb0 + b * n_chunks + c, col))
    lp = lambda r, n: pl.BlockSpec((None, r, n), lambda b, c: (layer, 0, 0))
    return pl.pallas_call(
        functools.partial(_lru_kernel, blk_rows=blk_rows, n_valid=n_valid),
        grid=(n_seq, n_chunks),
        in_specs=[
            pl.BlockSpec((blk_rows, D_LRU), row_map(C_G // D_LRU)),
            pl.BlockSpec((blk_rows, D_LRU), row_map(C_U // D_LRU)),
            pl.BlockSpec((None, 1, SUBLANES, D_LRU), lambda b, c: (state_layer, b, 0, 0)),
            pl.BlockSpec((None, 1, 1, D_LRU), lambda b, c: (state_layer, b, 0, 0)),
            lp(CONV_W, D_LRU), lp(1, D_LRU), lp(D_LRU, D_LRU), lp(1, D_LRU), lp(D_LRU, D_LRU), lp(1, D_LRU),
            lp(1, D_LRU),
        ],
        out_specs=[
            pl.BlockSpec((blk_rows, D_LRU), lambda b, c: (b * n_chunks + c, 0)),
            pl.BlockSpec((1, 1, D_LRU), lambda b, c: (b, 0, 0)),
        ],
        out_shape=[
            jax.ShapeDtypeStruct((n_seq * n_chunks * blk_rows, D_LRU), _tile_dtype(blk_rows)),
            jax.ShapeDtypeStruct((n_seq, 1, D_LRU), F32),
        ],
        scratch_shapes=[
            pltpu.VMEM((SUBLANES + blk_rows, D_LRU), F32),
            pltpu.VMEM((1, D_LRU), F32),
        ],
        compiler_params=_cparams(("parallel", "arbitrary")),
        name="lru",
    )(proj, proj, hist, h0, p["cw"], p["cb"], p["wa"], p["ba"], p["wx"], p["bx"], p["lam"])


def _hist_tile(h):
    return jnp.pad(h, ((0, 0), (0, 0), (SUBLANES - (CONV_W - 1), 0), (0, 0)))


def _block_diag(w):
    eye = jnp.eye(LRU_BLOCKS, dtype=w.dtype)
    return jnp.einsum("lhij,hg->lhigj", w, eye).reshape(w.shape[0], D_LRU, D_LRU)


def _pick_tile(t, cap):
    best = SUBLANES
    for tm in range(SUBLANES, cap + 1, SUBLANES):
        if t % tm == 0:
            best = tm
    return best


def kernel(x_prompt, x_sample, cache_k, cache_v, cache_logf, page_table, state_ssd, state_ssd_conv, state_lru, state_lru_conv, ffn1_norm, ffn1_w_gate, ffn1_w_up, ffn1_w_down, mix_norm, w_in, fox_b_f, ssd_conv_w, ssd_conv_b, ssd_dt_bias, ssd_a_log, ssd_d, ssd_norm, lru_conv_w, lru_conv_b, lru_w_a, lru_b_a, lru_w_x, lru_b_x, lru_lambda, w_out, ffn2_norm, ffn2_w_gate, ffn2_w_up, ffn2_w_down, final_norm):
    bp, seq, _ = x_prompt.shape
    bd, t_len, _ = x_sample.shape
    depth = w_in.shape[0]
    n_pool = cache_k.shape[1]
    assert seq % 512 == 0 and t_len <= SAMPLE_ROWS and t_len >= CONV_W - 1
    assert page_table.shape[1] % PAGES_PER_STEP == 0
    tp = bp * seq
    ts = bd * SAMPLE_ROWS
    t = tp + ts
    tm_ffn = _pick_tile(t, 768)
    tm_proj = _pick_tile(t, 1056)
    tq = 256 if seq % 256 == 0 else CHUNK

    xs_pad = jnp.pad(x_sample, ((0, 0), (0, SAMPLE_ROWS - t_len), (0, 0)))
    x = jnp.concatenate([x_prompt.reshape(tp, D_MODEL), xs_pad.reshape(ts, D_MODEL)], axis=0)

    ck = jnp.transpose(cache_k, (0, 1, 3, 4, 2)).reshape(depth, n_pool, D_ATT, PAGE)
    cv = jnp.transpose(cache_v, (0, 1, 3, 4, 2)).reshape(depth, n_pool, D_ATT, PAGE)
    clf = jnp.transpose(cache_logf, (0, 1, 3, 2))
    e_mat = (jnp.arange(LANES)[:, None] == (jnp.arange(D_SSD)[None, :] // HEAD_DIM)).astype(BF16)

    rows3 = lambda v: v.reshape(depth, 1, -1).astype(F32)
    lanes3 = lambda v: jnp.pad(rows3(v), ((0, 0), (0, 0), (0, LANES - v.shape[-1])))
    bf16 = lambda w: w.astype(BF16)
    ffn1 = (rows3(ffn1_norm), bf16(ffn1_w_gate), bf16(ffn1_w_up), bf16(ffn1_w_down))
    ffn2 = (rows3(ffn2_norm), bf16(ffn2_w_gate), bf16(ffn2_w_up), bf16(ffn2_w_down))
    o_f, o_z, o_xbc = 3 * D_ATT, 3 * D_ATT + H_ATT, 3 * D_ATT + H_ATT + D_SSD
    o_dt = o_xbc + D_SSD + D_BC
    o_g = o_dt + H_SSD
    lane_pad = lambda w: jnp.pad(w, ((0, 0), (0, 0), (0, LANES - w.shape[-1])))
    w_in_p = jnp.concatenate([
        w_in[:, :, o_z:o_z + D_SSD], w_in[:, :, o_xbc:o_xbc + D_SSD + D_BC], w_in[:, :, 0:3 * D_ATT],
        w_in[:, :, o_g:o_g + 2 * D_LRU],
        lane_pad(w_in[:, :, o_f:o_f + H_ATT]), lane_pad(w_in[:, :, o_dt:o_dt + H_SSD]),
    ], axis=2).astype(BF16)
    mix_g = rows3(mix_norm)
    w_out_b = w_out.astype(BF16)
    bf_pad = lanes3(fox_b_f)
    ssd_p = dict(
        cwx=ssd_conv_w[:, :, :D_SSD], cbx=rows3(ssd_conv_b[:, :D_SSD]),
        cwbc=ssd_conv_w[:, :, D_SSD:], cbbc=rows3(ssd_conv_b[:, D_SSD:]),
        dtb=lanes3(ssd_dt_bias), alog=lanes3(ssd_a_log),
        dexp=rows3(jnp.repeat(ssd_d, HEAD_DIM, axis=1)), nw=rows3(ssd_norm), e=e_mat, et=e_mat.T)
    lru_p = dict(
        cw=lru_conv_w, cb=rows3(lru_conv_b),
        wa=_block_diag(lru_w_a).astype(BF16), ba=rows3(lru_b_a),
        wx=_block_diag(lru_w_x).astype(BF16), bx=rows3(lru_b_x), lam=rows3(lru_lambda))
    st_ssd = state_ssd.reshape(depth, bd, D_SSD, SSD_STATE)
    hist_ssd = _hist_tile(state_ssd_conv)
    hist_lru = _hist_tile(state_lru_conv)
    st_lru = state_lru.reshape(depth, bd, 1, D_LRU)
    zeros_ssd = jnp.zeros((1, bp, D_SSD, SSD_STATE), F32)
    zeros_hist_ssd = jnp.zeros((1, bp, SUBLANES, D_SSD + D_BC), F32)
    zeros_hist_lru = jnp.zeros((1, bp, SUBLANES, D_LRU), F32)
    zeros_lru = jnp.zeros((1, bp, 1, D_LRU), F32)
    n_chunks = seq // CHUNK

    outs_p = [[] for _ in range(7)]
    outs_s = [[] for _ in range(7)]
    for l in range(depth):
        last = l == depth - 1
        x = _ffn(x, *ffn1, None, l, tm_ffn)
        proj = _inproj(x, mix_g, w_in_p, l, tm_proj)

        logf_p, c_p, ct_p = _fox_prep(proj, bf_pad, l, bp, seq)
        att_p = _fox_prompt(proj, c_p, ct_p, bp, seq, tq)
        ssd_y_p, ssd_st_p = _ssd(proj, zeros_hist_ssd, zeros_ssd, 0, ssd_p, l, 0, bp, n_chunks, CHUNK, CHUNK)
        lru_y_p, lru_st_p = _lru(proj, zeros_hist_lru, zeros_lru, 0, lru_p, l, 0, bp, seq // LRU_ROWS, LRU_ROWS,
                                 LRU_ROWS)

        att_s, logf_s = _fox_sample(proj, bf_pad, ck, cv, clf, page_table, l, tp, bd, t_len)
        ssd_y_s, ssd_st_s = _ssd(proj, hist_ssd, st_ssd, l, ssd_p, l, tp, bd, 1, SAMPLE_ROWS, t_len)
        lru_y_s, lru_st_s = _lru(proj, hist_lru, st_lru, l, lru_p, l, tp, bd, 1, SAMPLE_ROWS, t_len)

        x = _outproj(x, (att_p, ssd_y_p, lru_y_p), (att_s, ssd_y_s, lru_y_s), w_out_b, l, tq, seq // tq)
        x = _ffn(x, *ffn2, final_norm.reshape(1, D_MODEL) if last else None, l, tm_ffn)

        def cols_p(c0, n, r0=0):
            if r0:
                return jnp.stack([lax.slice(proj, (b * seq + r0, c0), ((b + 1) * seq, c0 + n)) for b in range(bp)])
            return lax.slice(proj, (0, c0), (tp, c0 + n)).reshape(bp, seq, n)

        def cols_s(c0, n, r0, r1):
            return lax.slice(proj, (tp, c0), (t, c0 + n)).reshape(bd, SAMPLE_ROWS, n)[:, r0:r1]

        tail = seq - (CONV_W - 1)
        outs_p[0].append(cols_p(C_K, D_ATT).reshape(bp, seq, H_ATT, HEAD_DIM))
        outs_p[1].append(cols_p(C_V, D_ATT).reshape(bp, seq, H_ATT, HEAD_DIM))
        outs_p[2].append(logf_p[:, :H_ATT].reshape(bp, seq, H_ATT))
        outs_p[3].append(ssd_st_p.reshape(bp, H_SSD, HEAD_DIM, SSD_STATE))
        outs_p[4].append(cols_p(C_XS, D_SSD + D_BC, tail))
        outs_p[5].append(lru_st_p.reshape(bp, D_LRU))
        outs_p[6].append(cols_p(C_U, D_LRU, tail))
        outs_s[0].append(cols_s(C_K, D_ATT, 0, t_len).reshape(bd, t_len, H_ATT, HEAD_DIM))
        outs_s[1].append(cols_s(C_V, D_ATT, 0, t_len).reshape(bd, t_len, H_ATT, HEAD_DIM))
        outs_s[2].append(logf_s.reshape(bd, SAMPLE_ROWS, LANES)[:, :t_len, :H_ATT])
        outs_s[3].append(ssd_st_s.reshape(bd, H_SSD, HEAD_DIM, SSD_STATE))
        outs_s[4].append(cols_s(C_XS, D_SSD + D_BC, t_len - (CONV_W - 1), t_len))
        outs_s[5].append(lru_st_s.reshape(bd, D_LRU))
        outs_s[6].append(cols_s(C_U, D_LRU, t_len - (CONV_W - 1), t_len))

    y_prompt = x[:tp].reshape(bp, seq, D_MODEL)
    y_sample = x[tp:].reshape(bd, SAMPLE_ROWS, D_MODEL)[:, :t_len]
    sp = [jnp.stack(o) for o in outs_p]
    ss = [jnp.stack(o) for o in outs_s]
    return (y_prompt, y_sample, sp[0], sp[1], sp[2], ss[0], ss[1], ss[2], sp[3], sp[4], ss[3], ss[4],
            sp[5], sp[6], ss[5], ss[6])
```

```python
import functools
import math

import numpy as np
import jax
import jax.numpy as jnp
from jax import lax
from jax.experimental import pallas as pl
from jax.experimental.pallas import tpu as pltpu

F32 = jnp.float32
BF16 = jnp.bfloat16

D_MODEL = 2048
HEAD_DIM = 64
D_ATT = 512
D_SSD = 1024
D_LRU = 512
H_ATT = 8
H_SSD = 16
SSD_GROUPS = 2
SSD_STATE = 128
D_BC = 2 * SSD_GROUPS * SSD_STATE
CONV_W = 4
LRU_BLOCKS = 8
LRU_BW = 64
LRU_C = 8.0
D_FF = 5504
PAGE = 128
RMS_EPS = 1e-6

LANES = 128
SUBLANES = 8
VMEM_LIMIT = 56 * 1024 * 1024

CHUNK = 128
LRU_ROWS = 256
SAMPLE_ROWS = 8
FF_TILE = 512
PAGES_PER_STEP = 32

C_Z = 0
C_XS = 1024
C_BC = 2048
C_Q = 2560
C_K = 3072
C_V = 3584
C_G = 4096
C_U = 4608
C_F = 5120
C_DT = 5248
N_PROJ = 5376
PROJ_TILE = 1792


def _cparams(sem):
    return pltpu.CompilerParams(dimension_semantics=sem, vmem_limit_bytes=VMEM_LIMIT)


def _rms(x, g):
    return x * lax.rsqrt(jnp.mean(x * x, axis=-1, keepdims=True) + RMS_EPS) * g


def _silu(x):
    return x * jax.nn.sigmoid(x)


def _softplus(x):
    return jnp.maximum(x, 0.0) + jnp.log1p(jnp.exp(-jnp.abs(x)))


def _split3(a):
    a1 = a.astype(BF16)
    r1 = a - a1.astype(F32)
    a2 = r1.astype(BF16)
    r2 = r1 - a2.astype(F32)
    return a1, a2, r2.astype(BF16)


def _dot_sel_r(a, sel):
    a1, a2, a3 = _split3(a)
    d = lambda p: jnp.dot(p, sel, preferred_element_type=F32)
    return d(a1) + d(a2) + d(a3)


def _dot_sel_l(sel, a):
    a1, a2, a3 = _split3(a)
    d = lambda p: jnp.dot(sel, p, preferred_element_type=F32)
    return d(a1) + d(a2) + d(a3)


def _dot_nt(a, b):
    return lax.dot_general(a, b, (((1,), (1,)), ((), ())), preferred_element_type=F32)


def _tile_dtype(rows):
    return BF16 if rows % (2 * SUBLANES) == 0 else F32


def _iota(shape, dim):
    return lax.broadcasted_iota(jnp.int32, shape, dim)


def _tril_bf16(n):
    return (_iota((n, n), 0) >= _iota((n, n), 1)).astype(BF16)


def _ffn_kernel(*refs, n_main, n_tail, final):
    x_ref, g_ref = refs[0:2]
    main = refs[2:5]
    tails = [refs[5 + 3 * k:8 + 3 * k] for k in range(n_tail)]
    rest = refs[5 + 3 * n_tail:]
    if final:
        fg_ref, o_ref, xn_ref = rest
    else:
        o_ref, xn_ref = rest
    j = pl.program_id(1)

    def part(wg_ref, wu_ref, wd_ref):
        xn = xn_ref[...]
        a = jnp.dot(xn, wg_ref[...], preferred_element_type=F32)
        b = jnp.dot(xn, wu_ref[...], preferred_element_type=F32)
        h = (_silu(a) * b).astype(BF16)
        return jnp.dot(h, wd_ref[...], preferred_element_type=F32)

    @pl.when(j == 0)
    def _():
        xn_ref[...] = _rms(x_ref[...], g_ref[...]).astype(BF16)
        acc = jnp.zeros(o_ref.shape, F32)
        for tail in tails:
            acc = acc + part(*tail)
        o_ref[...] = acc

    o_ref[...] += part(*main)

    @pl.when(j == n_main - 1)
    def _():
        y = x_ref[...] + 0.5 * o_ref[...]
        if final:
            y = _rms(y, fg_ref[...])
        o_ref[...] = y


def _ffn_tail_tiles():
    tiles, start, width = [], (D_FF // FF_TILE) * FF_TILE, FF_TILE // 2
    while start < D_FF:
        if start + width <= D_FF:
            assert width % LANES == 0 and start % width == 0
            tiles.append((width, start // width))
            start += width
        width //= 2
    return tiles


def _ffn(x, g, wg, wu, wd, final_g, layer, tm):
    t = x.shape[0]
    n_main = D_FF // FF_TILE
    tails = _ffn_tail_tiles()
    final = final_g is not None

    def weight_specs(width, col):
        return [pl.BlockSpec((None, D_MODEL, width), lambda i, j: (layer, 0, col(j))),
                pl.BlockSpec((None, D_MODEL, width), lambda i, j: (layer, 0, col(j))),
                pl.BlockSpec((None, width, D_MODEL), lambda i, j: (layer, col(j), 0))]

    in_specs = [
        pl.BlockSpec((tm, D_MODEL), lambda i, j: (i, 0)),
        pl.BlockSpec((None, 1, D_MODEL), lambda i, j: (layer, 0, 0)),
    ] + weight_specs(FF_TILE, lambda j: j)
    args = [x, g, wg, wu, wd]
    for width, blk in tails:
        in_specs += weight_specs(width, lambda j, blk=blk: blk)
        args += [wg, wu, wd]
    if final:
        in_specs.append(pl.BlockSpec((1, D_MODEL), lambda i, j: (0, 0)))
        args.append(final_g)
    return pl.pallas_call(
        functools.partial(_ffn_kernel, n_main=n_main, n_tail=len(tails), final=final),
        grid=(t // tm, n_main),
        in_specs=in_specs,
        out_specs=pl.BlockSpec((tm, D_MODEL), lambda i, j: (i, 0)),
        out_shape=jax.ShapeDtypeStruct((t, D_MODEL), F32),
        scratch_shapes=[pltpu.VMEM((tm, D_MODEL), BF16)],
        compiler_params=_cparams(("parallel", "arbitrary")),
        name="ffn",
    )(*args)


def _inproj_kernel(x_ref, g_ref, w_ref, o_ref, xn_ref):
    @pl.when(pl.program_id(1) == 0)
    def _():
        xn_ref[...] = _rms(x_ref[...], g_ref[...]).astype(BF16)

    o_ref[...] = jnp.dot(xn_ref[...], w_ref[...], preferred_element_type=F32)


def _inproj(x, g, w, layer, tm):
    t = x.shape[0]
    return pl.pallas_call(
        _inproj_kernel,
        grid=(t // tm, N_PROJ // PROJ_TILE),
        in_specs=[
            pl.BlockSpec((tm, D_MODEL), lambda i, j: (i, 0)),
            pl.BlockSpec((None, 1, D_MODEL), lambda i, j: (layer, 0, 0)),
            pl.BlockSpec((None, D_MODEL, PROJ_TILE), lambda i, j: (layer, 0, j)),
        ],
        out_specs=pl.BlockSpec((tm, PROJ_TILE), lambda i, j: (i, j)),
        out_shape=jax.ShapeDtypeStruct((t, N_PROJ), F32),
        scratch_shapes=[pltpu.VMEM((tm, D_MODEL), BF16)],
        compiler_params=_cparams(("parallel", "arbitrary")),
        name="inproj",
    )(x, g, w)


def _outproj_kernel(x_ref, att_p_ref, ssd_p_ref, lru_p_ref, att_s_ref, ssd_s_ref, lru_s_ref, w_ref, o_ref, *, n_p):
    def mix(att, ssd, lru):
        acc = jnp.dot(att, w_ref[0:D_ATT, :], preferred_element_type=F32)
        acc += jnp.dot(ssd, w_ref[D_ATT:D_ATT + D_SSD, :], preferred_element_type=F32)
        acc += jnp.dot(lru, w_ref[D_ATT + D_SSD:, :], preferred_element_type=F32)
        o_ref[...] = x_ref[...] + acc

    is_prompt = pl.program_id(0) < n_p

    @pl.when(is_prompt)
    def _():
        mix(att_p_ref[...], ssd_p_ref[...], lru_p_ref[...])

    @pl.when(jnp.logical_not(is_prompt))
    def _():
        mix(att_s_ref[...].astype(BF16), ssd_s_ref[...].astype(BF16), lru_s_ref[...].astype(BF16))


def _outproj(x, mixed_p, mixed_s, w, layer, tm, nq):
    t = x.shape[0]
    att_p, ssd_p, lru_p = mixed_p
    att_s, ssd_s, lru_s = mixed_s
    n_p = ssd_p.shape[0] // tm
    assert ssd_p.shape[0] % tm == 0 and ssd_s.shape[0] % tm == 0 and att_p.shape[1] * 2 == ssd_p.shape[0]
    p_blk = lambda i: jnp.minimum(i, n_p - 1)
    s_blk = lambda i: jnp.maximum(i - n_p, 0)
    return pl.pallas_call(
        functools.partial(_outproj_kernel, n_p=n_p),
        grid=(t // tm,),
        in_specs=[
            pl.BlockSpec((tm, D_MODEL), lambda i: (i, 0)),
            pl.BlockSpec((None, tm, D_ATT), lambda i: (*_att_block(p_blk(i), nq), 0)),
            pl.BlockSpec((tm, D_SSD), lambda i: (p_blk(i), 0)),
            pl.BlockSpec((tm, D_LRU), lambda i: (p_blk(i), 0)),
            pl.BlockSpec((tm, D_ATT), lambda i: (s_blk(i), 0)),
            pl.BlockSpec((tm, D_SSD), lambda i: (s_blk(i), 0)),
            pl.BlockSpec((tm, D_LRU), lambda i: (s_blk(i), 0)),
            pl.BlockSpec((None, D_MODEL, D_MODEL), lambda i: (layer, 0, 0)),
        ],
        out_specs=pl.BlockSpec((tm, D_MODEL), lambda i: (i, 0)),
        out_shape=jax.ShapeDtypeStruct((t, D_MODEL), F32),
        compiler_params=_cparams(("parallel",)),
        name="outproj",
    )(x, att_p, ssd_p, lru_p, att_s, ssd_s, lru_s, w)


def _fox_prep_kernel(f_ref, bf_ref, logf_ref, c_ref, ct_ref, *, n_blk):
    tril = _tril_bf16(CHUNK)
    carry = jnp.zeros((1, LANES), F32)
    for i in range(n_blk):
        rows = slice(i * CHUNK, (i + 1) * CHUNK)
        lf = -_softplus(-(f_ref[rows, :] + bf_ref[...]))
        logf_ref[rows, :] = lf
        cs = _dot_sel_l(tril, lf) + carry
        carry = cs[CHUNK - 1:CHUNK, :]
        c_ref[rows, :] = cs
        ct_ref[0, :, rows] = cs.T[0:SUBLANES, :]


def _fox_prep(proj, bf_pad, layer, n_seq, seq):
    n_blk = seq // CHUNK
    return pl.pallas_call(
        functools.partial(_fox_prep_kernel, n_blk=n_blk),
        grid=(n_seq,),
        in_specs=[
            pl.BlockSpec((seq, LANES), lambda b: (b, C_F // LANES)),
            pl.BlockSpec((None, 1, LANES), lambda b: (layer, 0, 0)),
        ],
        out_specs=[
            pl.BlockSpec((seq, LANES), lambda b: (b, 0)),
            pl.BlockSpec((seq, LANES), lambda b: (b, 0)),
            pl.BlockSpec((1, SUBLANES, seq), lambda b: (b, 0, 0)),
        ],
        out_shape=[
            jax.ShapeDtypeStruct((n_seq * seq, LANES), F32),
            jax.ShapeDtypeStruct((n_seq * seq, LANES), F32),
            jax.ShapeDtypeStruct((n_seq, SUBLANES, seq), F32),
        ],
        compiler_params=_cparams(("parallel",)),
        name="fox_prep",
    )(proj, bf_pad)


def _fox_prompt_kernel(qa_ref, qb_ref, k_ref, v_ref, ca_ref, cb_ref, ct_ref, o_ref, *, tq, nq):
    hp = pl.program_id(1)
    pair = pl.program_id(2)
    scale = 1.0 / math.sqrt(HEAD_DIM)
    lane = _iota((tq, LANES), 1)
    upper = lane >= HEAD_DIM
    tri = _iota((tq, tq), 0) >= _iota((tq, tq), 1)

    def attend(q, c_blk, n):
        w = n * tq
        kb = k_ref[0:w, :].astype(BF16)
        v = v_ref[0:w, :]
        upper_w = _iota((w, LANES), 1) >= HEAD_DIM
        outs = []
        for e in range(2):
            h = 2 * hp + e
            in_head = upper if e else jnp.logical_not(upper)
            qm = (jnp.where(in_head, q, 0.0) * scale).astype(BF16)
            c_col = jnp.sum(jnp.where(lane == h, c_blk, 0.0), axis=1, keepdims=True)
            c_row = ct_ref[0, pl.ds(h, 1), 0:w]
            s = _dot_nt(qm, kb) + (c_col - c_row)
            diag = jnp.where(tri, s[:, w - tq:], -1e30)
            s = diag if n == 1 else jnp.concatenate([s[:, :w - tq], diag], axis=1)
            p = jnp.exp(s - jnp.max(s, axis=1, keepdims=True)).astype(BF16)
            in_head_w = upper_w if e else jnp.logical_not(upper_w)
            acc = jnp.dot(p, jnp.where(in_head_w, v, 1.0).astype(BF16), preferred_element_type=F32)
            den = acc[:, 0:1] if e else acc[:, HEAD_DIM:HEAD_DIM + 1]
            outs.append(acc / den)
        return jnp.where(upper, outs[1], outs[0])

    for p in range(nq // 2):
        @pl.when(pair == p)
        def _(p=p):
            o_ref[0] = attend(qa_ref[...], ca_ref[...], p + 1).astype(o_ref.dtype)
            o_ref[1] = attend(qb_ref[...], cb_ref[...], nq - p).astype(o_ref.dtype)


def _fox_prompt(proj, c, ct, n_seq, seq, tq):
    nq = seq // tq
    half = nq // 2
    a_blk = lambda b, p: b * nq + p
    b_blk = lambda b, p: b * nq + (nq - 1 - p)
    return pl.pallas_call(
        functools.partial(_fox_prompt_kernel, tq=tq, nq=nq),
        grid=(n_seq, H_ATT // 2, half),
        in_specs=[
            pl.BlockSpec((tq, LANES), lambda b, hp, p: (a_blk(b, p), C_Q // LANES + hp)),
            pl.BlockSpec((tq, LANES), lambda b, hp, p: (b_blk(b, p), C_Q // LANES + hp)),
            pl.BlockSpec((seq, LANES), lambda b, hp, p: (b, C_K // LANES + hp)),
            pl.BlockSpec((seq, LANES), lambda b, hp, p: (b, C_V // LANES + hp)),
            pl.BlockSpec((tq, LANES), lambda b, hp, p: (a_blk(b, p), 0)),
            pl.BlockSpec((tq, LANES), lambda b, hp, p: (b_blk(b, p), 0)),
            pl.BlockSpec((1, SUBLANES, seq), lambda b, hp, p: (b, 0, 0)),
        ],
        out_specs=pl.BlockSpec((2, tq, LANES), lambda b, hp, p: (0, b * half + p, hp)),
        out_shape=jax.ShapeDtypeStruct((2, n_seq * half * tq, D_ATT), BF16),
        compiler_params=_cparams(("parallel", "parallel", "arbitrary")),
        name="fox_prompt",
    )(proj, proj, proj, proj, c, c, ct)


def _kv_transpose_kernel(*refs, depth):
    kt_ref, vt_ref = refs[depth:]
    for i in range(depth):
        @pl.when(pl.program_id(0) == i)
        def _(i=i):
            kv = refs[i][...]
            kt_ref[0, 0] = kv[:, :D_ATT].T
            vt_ref[0, 0] = kv[:, D_ATT:].T


def _kv_transpose(projs, n_seq, seq, tile):
    depth = len(projs)
    nj = seq // tile
    assert C_V == C_K + D_ATT and C_K % (2 * D_ATT) == 0

    def rows(i):
        return lambda l, b, j: (jnp.where(l == i, b * nj + j, 0), C_K // (2 * D_ATT))

    out = jax.ShapeDtypeStruct((depth, n_seq, D_ATT, seq), F32)
    return pl.pallas_call(
        functools.partial(_kv_transpose_kernel, depth=depth),
        grid=(depth, n_seq, nj),
        in_specs=[pl.BlockSpec((tile, 2 * D_ATT), rows(i)) for i in range(depth)],
        out_specs=[pl.BlockSpec((1, 1, D_ATT, tile), lambda l, b, j: (l, b, 0, j))] * 2,
        out_shape=[out, out],
        compiler_params=_cparams(("arbitrary", "arbitrary", "arbitrary")),
        name="kv_transpose",
    )(*projs)


def _att_block(r, nq):
    half = nq // 2
    b, qb = r // nq, r % nq
    second = qb >= half
    return jnp.where(second, 1, 0), b * half + jnp.where(second, nq - 1 - qb, qb)


def _fox_sample_kernel(pt_ref, *refs, n_steps, n_valid):
    del pt_ref
    npg = PAGES_PER_STEP
    q_ref, kn_ref, vn_ref, f_ref, bf_ref = refs[0:5]
    kp_refs = refs[5:5 + npg]
    vp_refs = refs[5 + npg:5 + 2 * npg]
    lf_refs = refs[5 + 2 * npg:5 + 3 * npg]
    o_ref, logf_ref = refs[5 + 3 * npg:7 + 3 * npg]
    qbd_ref, m_ref, l_ref, acc_ref, carry_ref, cnt_ref, pad_ref = refs[7 + 3 * npg:]
    j = pl.program_id(1)
    rows = SAMPLE_ROWS * H_ATT
    scale = 1.0 / math.sqrt(HEAD_DIM)
    row_i = _iota((rows, LANES), 0)
    lane_i = _iota((rows, LANES), 1)

    def rep_tokens(x):
        return jnp.concatenate(
            [jnp.broadcast_to(x[t:t + 1, :], (H_ATT, x.shape[1])) for t in range(SAMPLE_ROWS)], axis=0)

    def tile_heads(x):
        return jnp.concatenate([x] * SAMPLE_ROWS, axis=0)

    @pl.when(j == 0)
    def _():
        q = q_ref[...]
        head_of_lane = _iota((rows, D_ATT), 1) // HEAD_DIM
        head_of_row = _iota((rows, D_ATT), 0) % H_ATT
        qbd_ref[...] = (jnp.where(head_of_lane == head_of_row, rep_tokens(q), 0.0) * scale).astype(BF16)
        lf = -_softplus(-(f_ref[...] + bf_ref[...]))
        logf_ref[...] = lf
        ri = _iota((SAMPLE_ROWS, LANES), 0)
        cn = lf
        sh = 1
        while sh < SAMPLE_ROWS:
            cn = cn + jnp.where(ri >= sh, pltpu.roll(cn, sh, axis=0), 0.0)
            sh *= 2
        pad_ref[...] = jnp.zeros_like(pad_ref)
        pad_ref[0:SAMPLE_ROWS, :] = cn
        cnt_ref[...] = pad_ref[...].T[0:SUBLANES, :]
        m_ref[...] = jnp.full_like(m_ref, -1e30)
        l_ref[...] = jnp.zeros_like(l_ref)
        acc_ref[...] = jnp.zeros_like(acc_ref)
        carry_ref[...] = jnp.zeros_like(carry_ref)

    qbd = qbd_ref[...]

    def online(s, v_bf16, v_transposed):
        m = m_ref[...][:, 0:1]
        m_new = jnp.maximum(m, jnp.max(s, axis=1, keepdims=True))
        alpha = jnp.exp(m - m_new)
        p = jnp.exp(s - m_new)
        l_ref[...] = alpha * l_ref[...] + jnp.sum(p, axis=1, keepdims=True)
        pb = p.astype(BF16)
        pv = _dot_nt(pb, v_bf16) if v_transposed else jnp.dot(pb, v_bf16, preferred_element_type=F32)
        acc_ref[...] = alpha * acc_ref[...] + pv
        m_ref[...] = jnp.broadcast_to(m_new, m_ref.shape)

    carry = carry_ref[...][:, 0:1]
    lft = jnp.concatenate([lf_refs[p][0, 0] for p in range(npg)], axis=0)
    later = (_iota((PAGE, PAGE), 0) >= _iota((PAGE, PAGE), 1)).astype(BF16)
    incl = _dot_sel_r(lft, later)
    excl = incl - lft
    s_parts = []
    for p in range(npg):
        hs = slice(p * H_ATT, (p + 1) * H_ATT)
        suffix = excl[hs, :] + carry
        carry = carry + incl[hs, 0:1]
        kt = kp_refs[p][0, 0].astype(BF16)
        s_parts.append(jnp.dot(qbd, kt, preferred_element_type=F32) + tile_heads(suffix))
    carry_ref[...] = jnp.broadcast_to(carry, carry_ref.shape)
    s = jnp.concatenate(s_parts, axis=1)
    vt = jnp.concatenate([vp_refs[p][0, 0].astype(BF16) for p in range(npg)], axis=1)
    online(s, vt, True)

    @pl.when(j == n_steps - 1)
    def _():
        zk = jnp.zeros((PAGE - SAMPLE_ROWS, D_ATT), F32)
        kb = jnp.concatenate([kn_ref[...], zk], axis=0).astype(BF16)
        vb = jnp.concatenate([vn_ref[...], zk], axis=0).astype(BF16)
        s = _dot_nt(qbd, kb) - tile_heads(cnt_ref[...])
        ok = (lane_i <= row_i // H_ATT) & (lane_i < n_valid)
        online(jnp.where(ok, s, -1e30), vb, False)
        out = acc_ref[...] / l_ref[...][:, 0:1]
        head_of_lane = _iota((H_ATT, D_ATT), 1) // HEAD_DIM
        head_of_row = _iota((H_ATT, D_ATT), 0)
        for t in range(SAMPLE_ROWS):
            blk = out[t * H_ATT:(t + 1) * H_ATT, :]
            o_ref[t:t + 1, :] = jnp.sum(jnp.where(head_of_lane == head_of_row, blk, 0.0),
                                        axis=0, keepdims=True).astype(o_ref.dtype)


def _fox_sample(proj, bf_pad, cache_k, cache_v, cache_lf, page_table, layer, row0, n_seq, n_valid):
    npg = PAGES_PER_STEP
    n_pages = page_table.shape[1]
    n_steps = n_pages // npg
    rb0 = row0 // SAMPLE_ROWS
    rows = SAMPLE_ROWS * H_ATT

    def page_map(p):
        return lambda b, j, pt: (layer, pt[b, n_pages - 1 - (j * npg + p)], 0, 0)

    in_specs = [
        pl.BlockSpec((SAMPLE_ROWS, D_ATT), lambda b, j, pt: (rb0 + b, C_Q // D_ATT)),
        pl.BlockSpec((SAMPLE_ROWS, D_ATT), lambda b, j, pt: (rb0 + b, C_K // D_ATT)),
        pl.BlockSpec((SAMPLE_ROWS, D_ATT), lambda b, j, pt: (rb0 + b, C_V // D_ATT)),
        pl.BlockSpec((SAMPLE_ROWS, LANES), lambda b, j, pt: (rb0 + b, C_F // LANES)),
        pl.BlockSpec((None, 1, LANES), lambda b, j, pt: (layer, 0, 0)),
    ]
    in_specs += [pl.BlockSpec((1, 1, D_ATT, PAGE), page_map(p)) for p in range(npg)]
    in_specs += [pl.BlockSpec((1, 1, D_ATT, PAGE), page_map(p)) for p in range(npg)]
    in_specs += [pl.BlockSpec((1, 1, H_ATT, PAGE), page_map(p)) for p in range(npg)]
    grid_spec = pltpu.PrefetchScalarGridSpec(
        num_scalar_prefetch=1,
        grid=(n_seq, n_steps),
        in_specs=in_specs,
        out_specs=[
            pl.BlockSpec((SAMPLE_ROWS, D_ATT), lambda b, j, pt: (b, 0)),
            pl.BlockSpec((SAMPLE_ROWS, LANES), lambda b, j, pt: (b, 0)),
        ],
        scratch_shapes=[
            pltpu.VMEM((rows, D_ATT), BF16),
            pltpu.VMEM((rows, LANES), F32),
            pltpu.VMEM((rows, LANES), F32),
            pltpu.VMEM((rows, D_ATT), F32),
            pltpu.VMEM((SUBLANES, LANES), F32),
            pltpu.VMEM((SUBLANES, LANES), F32),
            pltpu.VMEM((PAGE, LANES), F32),
        ],
    )
    return pl.pallas_call(
        functools.partial(_fox_sample_kernel, n_steps=n_steps, n_valid=n_valid),
        grid_spec=grid_spec,
        out_shape=[
            jax.ShapeDtypeStruct((n_seq * SAMPLE_ROWS, D_ATT), F32),
            jax.ShapeDtypeStruct((n_seq * SAMPLE_ROWS, LANES), F32),
        ],
        compiler_params=_cparams(("parallel", "arbitrary")),
        name="fox_sample",
    )(page_table, proj, proj, proj, proj, bf_pad,
      *([cache_k] * npg), *([cache_v] * npg), *([cache_lf] * npg))


def _conv_from_ext(ext_ref, w_ref, b_ref, rows):
    out = b_ref[...] + ext_ref[SUBLANES:SUBLANES + rows, :] * w_ref[CONV_W - 1:CONV_W, :]
    for j in range(CONV_W - 1):
        off = SUBLANES - (CONV_W - 1) + j
        out = out + ext_ref[off:off + rows, :] * w_ref[j:j + 1, :]
    return out


def _ssd_kernel(z_ref, xs_ref, bc_ref, dt_ref, hx_ref, hbc_ref, h0_ref,
                cwx_ref, cbx_ref, cwbc_ref, cbbc_ref, dtb_ref, alog_ref, dexp_ref, nw_ref, e_ref, et_ref,
                y_ref, st_ref, extx_ref, extbc_ref, state_ref, *, blk_rows, n_valid):
    c = pl.program_id(1)
    r = CHUNK

    @pl.when(c == 0)
    def _():
        extx_ref[...] = jnp.zeros_like(extx_ref)
        extbc_ref[...] = jnp.zeros_like(extbc_ref)
        extx_ref[0:SUBLANES, :] = hx_ref[0]
        extbc_ref[0:SUBLANES, :] = hbc_ref[0]
        state_ref[...] = h0_ref[0, 0]

    extx_ref[SUBLANES:SUBLANES + blk_rows, :] = xs_ref[...]
    extbc_ref[SUBLANES:SUBLANES + blk_rows, :] = bc_ref[...]
    xs = _silu(_conv_from_ext(extx_ref, cwx_ref, cbx_ref, r))
    bc = _silu(_conv_from_ext(extbc_ref, cwbc_ref, cbbc_ref, r))
    if blk_rows == r:
        extx_ref[0:SUBLANES, :] = extx_ref[r:r + SUBLANES, :]
        extbc_ref[0:SUBLANES, :] = extbc_ref[r:r + SUBLANES, :]
        z = z_ref[...]
        dt_raw = dt_ref[...]
    else:
        z = jnp.concatenate([z_ref[...], jnp.zeros((r - blk_rows, D_SSD), F32)], axis=0)
        dt_raw = jnp.concatenate([dt_ref[...], jnp.zeros((r - blk_rows, LANES), F32)], axis=0)

    dt = _softplus(dt_raw + dtb_ref[...])
    if n_valid < r:
        dt = jnp.where(_iota((r, LANES), 0) < n_valid, dt, 0.0)
    a = -jnp.exp(alog_ref[...])
    acs = _dot_sel_l(_tril_bf16(r), dt * a)
    acs_t = acs.T
    e = e_ref[...]
    acs_x = _dot_sel_r(acs, e)
    dt_x = _dot_sel_r(dt, e)
    xdt = xs * dt_x
    xd = xdt * jnp.exp(acs_x[r - 1:r, :] - acs_x)
    state = state_ref[...]
    state_bf = state.astype(BF16)
    xd_t = xd.T.astype(BF16)
    xdt_bf = xdt.astype(BF16)
    tri = _iota((r, r), 0) >= _iota((r, r), 1)
    upper = _iota((r, LANES), 1) >= HEAD_DIM
    gw = D_SSD // SSD_GROUPS
    heads_per_group = H_SSD // SSD_GROUPS
    y_diag, y_off, new_state = [], [], []
    for g in range(SSD_GROUPS):
        bg = bc[:, g * SSD_STATE:(g + 1) * SSD_STATE].astype(BF16)
        cg = bc[:, D_BC // 2 + g * SSD_STATE:D_BC // 2 + (g + 1) * SSD_STATE].astype(BF16)
        cb = _dot_nt(cg, bg)
        y_off.append(_dot_nt(cg, state_bf[g * gw:(g + 1) * gw, :]))
        new_state.append(jnp.dot(xd_t[g * gw:(g + 1) * gw, :], bg, preferred_element_type=F32))
        for pi in range(heads_per_group // 2):
            lane0 = g * gw + pi * LANES
            pair = xdt_bf[:, lane0:lane0 + LANES]
            acc = None
            for hh in range(2):
                h = g * heads_per_group + 2 * pi + hh
                seg = acs[:, h:h + 1] - acs_t[h:h + 1, :]
                lm = jnp.where(tri, jnp.exp(seg), 0.0)
                mh = (cb * lm).astype(BF16)
                in_head = upper if hh else jnp.logical_not(upper)
                d = jnp.dot(mh, jnp.where(in_head, pair, jnp.zeros_like(pair)), preferred_element_type=F32)
                acc = d if acc is None else acc + d
            y_diag.append(acc)
    y_diag = jnp.concatenate(y_diag, axis=1)
    y_off = jnp.concatenate(y_off, axis=1)
    new_state = jnp.concatenate(new_state, axis=0)
    last_col = _dot_sel_l(et_ref[...], acs_t)[:, r - 1:r]
    state_new = jnp.exp(last_col) * state + new_state
    state_ref[...] = state_new
    st_ref[0, 0] = state_new

    y = y_diag + y_off * jnp.exp(acs_x) + xs * dexp_ref[...]
    yg = y * _silu(z)
    parts = []
    for g in range(SSD_GROUPS):
        p = yg[:, g * gw:(g + 1) * gw]
        parts.append(p * lax.rsqrt(jnp.mean(p * p, axis=-1, keepdims=True) + RMS_EPS))
    out = jnp.concatenate(parts, axis=1) * nw_ref[...]
    y_ref[...] = out[0:blk_rows, :].astype(y_ref.dtype)


def _ssd(proj, hist, h0, state_layer, p, layer, row0, n_seq, n_chunks, blk_rows, n_valid):
    rb0 = row0 // blk_rows
    row_map = lambda col: (lambda b, c: (rb0 + b * n_chunks + c, col))
    const = lambda b, c: (0, 0)
    lp = lambda r, n: pl.BlockSpec((None, r, n), lambda b, c: (layer, 0, 0))
    return pl.pallas_call(
        functools.partial(_ssd_kernel, blk_rows=blk_rows, n_valid=n_valid),
        grid=(n_seq, n_chunks),
        in_specs=[
            pl.BlockSpec((blk_rows, D_SSD), row_map(C_Z // D_SSD)),
            pl.BlockSpec((blk_rows, D_SSD), row_map(C_XS // D_SSD)),
            pl.BlockSpec((blk_rows, D_BC), row_map(C_BC // D_BC)),
            pl.BlockSpec((blk_rows, LANES), row_map(C_DT // LANES)),
            pl.BlockSpec((None, 1, SUBLANES, D_SSD), lambda b, c: (state_layer, b, 0, 0)),
            pl.BlockSpec((None, 1, SUBLANES, D_BC), lambda b, c: (state_layer, b, 0, D_SSD // D_BC)),
            pl.BlockSpec((1, 1, D_SSD, SSD_STATE), lambda b, c: (state_layer, b, 0, 0)),
            lp(CONV_W, D_SSD), lp(1, D_SSD), lp(CONV_W, D_BC), lp(1, D_BC),
            lp(1, LANES), lp(1, LANES), lp(1, D_SSD), lp(1, D_SSD),
            pl.BlockSpec((LANES, D_SSD), const),
            pl.BlockSpec((D_SSD, LANES), const),
        ],
        out_specs=[
            pl.BlockSpec((blk_rows, D_SSD), lambda b, c: (b * n_chunks + c, 0)),
            pl.BlockSpec((1, 1, D_SSD, SSD_STATE), lambda b, c: (0, b, 0, 0)),
        ],
        out_shape=[
            jax.ShapeDtypeStruct((n_seq * n_chunks * blk_rows, D_SSD), _tile_dtype(blk_rows)),
            jax.ShapeDtypeStruct((1, n_seq, D_SSD, SSD_STATE), F32),
        ],
        scratch_shapes=[
            pltpu.VMEM((SUBLANES + CHUNK, D_SSD), F32),
            pltpu.VMEM((SUBLANES + CHUNK, D_BC), F32),
            pltpu.VMEM((D_SSD, SSD_STATE), F32),
        ],
        compiler_params=_cparams(("parallel", "arbitrary")),
        name="ssd",
    )(proj, proj, proj, proj, hist, hist, h0,
      p["cwx"], p["cbx"], p["cwbc"], p["cbbc"], p["dtb"], p["alog"], p["dexp"], p["nw"], p["e"], p["et"])


def _lru_kernel(g_ref, u_ref, hist_ref, h0_ref, cw_ref, cb_ref, wa_ref, ba_ref, wx_ref, bx_ref, lam_ref,
                y_ref, st_ref, ext_ref, hprev_ref, *, blk_rows, n_valid):
    c = pl.program_id(1)
    r = blk_rows

    @pl.when(c == 0)
    def _():
        ext_ref[0:SUBLANES, :] = hist_ref[0]
        hprev_ref[...] = h0_ref[0]

    ext_ref[SUBLANES:SUBLANES + r, :] = u_ref[...]
    u = _conv_from_ext(ext_ref, cw_ref, cb_ref, r)
    ext_ref[0:SUBLANES, :] = ext_ref[r:r + SUBLANES, :]
    a, x = _lru_gates(u, wa_ref, ba_ref, wx_ref, bx_ref, lam_ref)
    a, x = _lru_scan(a, x, r)
    hs = x + a * hprev_ref[...]
    hprev_ref[...] = hs[r - 1:r, :]
    st_ref[0] = hs[n_valid - 1:n_valid, :]
    y_ref[...] = (hs * jax.nn.gelu(g_ref[...])).astype(y_ref.dtype)


def _lru_gates(u, wa_ref, ba_ref, wx_ref, bx_ref, lam_ref):
    ub = u.astype(BF16)
    rg = jax.nn.sigmoid(jnp.dot(ub, wa_ref[...], preferred_element_type=F32) + ba_ref[...])
    ig = jax.nn.sigmoid(jnp.dot(ub, wx_ref[...], preferred_element_type=F32) + bx_ref[...])
    log_a = -LRU_C * rg * _softplus(-lam_ref[...])
    a = jnp.exp(log_a)
    return a, jnp.sqrt(-jnp.tanh(log_a) * (a * a + 1.0)) * (ig * u)


def _lru_scan(a, x, seg):
    rows = a.shape[0]
    pos = _iota(a.shape, 0) % seg if seg < rows else _iota(a.shape, 0)
    sh = 1
    while sh < seg:
        keep = pos >= sh
        a_prev = jnp.where(keep, pltpu.roll(a, sh, axis=0), 1.0)
        x_prev = jnp.where(keep, pltpu.roll(x, sh, axis=0), 0.0)
        x = x + a * x_prev
        a = a * a_prev
        sh *= 2
    return a, x


def _lru_sample_kernel(g_ref, u_ref, hist_ref, h0_ref, cw_ref, cb_ref, wa_ref, ba_ref, wx_ref, bx_ref, lam_ref,
                       y_ref, st_ref, *, n_seq, n_valid):
    rs = SAMPLE_ROWS
    ext = jnp.concatenate([hist_ref[...], u_ref[...].reshape(n_seq, rs, D_LRU)], axis=1)
    ext = ext.reshape(n_seq * 2 * rs, D_LRU)
    conv = cb_ref[...] + ext * cw_ref[CONV_W - 1:CONV_W, :]
    for k in range(1, CONV_W):
        conv = conv + pltpu.roll(ext, k, axis=0) * cw_ref[CONV_W - 1 - k:CONV_W - k, :]
    u = conv.reshape(n_seq, 2 * rs, D_LRU)[:, rs:, :].reshape(n_seq * rs, D_LRU)
    a, x = _lru_gates(u, wa_ref, ba_ref, wx_ref, bx_ref, lam_ref)
    h0 = jnp.broadcast_to(h0_ref[...], (n_seq, rs, D_LRU)).reshape(n_seq * rs, D_LRU)
    first = _iota(a.shape, 0) % rs == 0
    x = x + jnp.where(first, a * h0, 0.0)
    _, hs = _lru_scan(a, x, rs)
    st_ref[...] = hs.reshape(n_seq, rs, D_LRU)[:, n_valid - 1:n_valid, :]
    y_ref[...] = hs * jax.nn.gelu(g_ref[...])


def _lru_sample(proj, hist, h0, p, layer, row0, n_seq, n_valid):
    rows = n_seq * SAMPLE_ROWS
    assert row0 % rows == 0
    lp = lambda r, n: pl.BlockSpec((None, r, n), lambda i: (layer, 0, 0))
    return pl.pallas_call(
        functools.partial(_lru_sample_kernel, n_seq=n_seq, n_valid=n_valid),
        grid=(1,),
        in_specs=[
            pl.BlockSpec((rows, D_LRU), lambda i: (row0 // rows, C_G // D_LRU)),
            pl.BlockSpec((rows, D_LRU), lambda i: (row0 // rows, C_U // D_LRU)),
            pl.BlockSpec((None, n_seq, SUBLANES, D_LRU), lambda i: (layer, 0, 0, 0)),
            pl.BlockSpec((None, n_seq, 1, D_LRU), lambda i: (layer, 0, 0, 0)),
            lp(CONV_W, D_LRU), lp(1, D_LRU), lp(D_LRU, D_LRU), lp(1, D_LRU), lp(D_LRU, D_LRU), lp(1, D_LRU),
            lp(1, D_LRU),
        ],
        out_specs=[
            pl.BlockSpec((rows, D_LRU), lambda i: (0, 0)),
            pl.BlockSpec((n_seq, 1, D_LRU), lambda i: (0, 0, 0)),
        ],
        out_shape=[
            jax.ShapeDtypeStruct((rows, D_LRU), F32),
            jax.ShapeDtypeStruct((n_seq, 1, D_LRU), F32),
        ],
        compiler_params=_cparams(("arbitrary",)),
        name="lru_sample",
    )(proj, proj, hist, h0, p["cw"], p["cb"], p["wa"], p["ba"], p["wx"], p["bx"], p["lam"])


def _lru(proj, hist, h0, state_layer, p, layer, row0, n_seq, n_chunks, blk_rows, n_valid):
    rb0 = row0 // blk_rows
    row_map = lambda col: (lambda b, c: (rb0 + b * n_chunks + c, col))
    lp = lambda r, n: pl.BlockSpec((None, r, n), lambda b, c: (layer, 0, 0))
    return pl.pallas_call(
        functools.partial(_lru_kernel, blk_rows=blk_rows, n_valid=n_valid),
        grid=(n_seq, n_chunks),
        in_specs=[
            pl.BlockSpec((blk_rows, D_LRU), row_map(C_G // D_LRU)),
            pl.BlockSpec((blk_rows, D_LRU), row_map(C_U // D_LRU)),
            pl.BlockSpec((None, 1, SUBLANES, D_LRU), lambda b, c: (state_layer, b, 0, 0)),
            pl.BlockSpec((None, 1, 1, D_LRU), lambda b, c: (state_layer, b, 0, 0)),
            lp(CONV_W, D_LRU), lp(1, D_LRU), lp(D_LRU, D_LRU), lp(1, D_LRU), lp(D_LRU, D_LRU), lp(1, D_LRU),
            lp(1, D_LRU),
        ],
        out_specs=[
            pl.BlockSpec((blk_rows, D_LRU), lambda b, c: (b * n_chunks + c, 0)),
            pl.BlockSpec((1, 1, D_LRU), lambda b, c: (b, 0, 0)),
        ],
        out_shape=[
            jax.ShapeDtypeStruct((n_seq * n_chunks * blk_rows, D_LRU), _tile_dtype(blk_rows)),
            jax.ShapeDtypeStruct((n_seq, 1, D_LRU), F32),
        ],
        scratch_shapes=[
            pltpu.VMEM((SUBLANES + blk_rows, D_LRU), F32),
            pltpu.VMEM((1, D_LRU), F32),
        ],
        compiler_params=_cparams(("parallel", "arbitrary")),
        name="lru",
    )(proj, proj, hist, h0, p["cw"], p["cb"], p["wa"], p["ba"], p["wx"], p["bx"], p["lam"])


def _hist_tile(h):
    return jnp.pad(h, ((0, 0), (0, 0), (SUBLANES - (CONV_W - 1), 0), (0, 0)))


def _block_diag(w):
    eye = jnp.eye(LRU_BLOCKS, dtype=w.dtype)
    return jnp.einsum("lhij,hg->lhigj", w, eye).reshape(w.shape[0], D_LRU, D_LRU)


def _pick_tile(t, cap):
    best = SUBLANES
    for tm in range(SUBLANES, cap + 1, SUBLANES):
        if t % tm == 0:
            best = tm
    return best


def kernel(x_prompt, x_sample, cache_k, cache_v, cache_logf, page_table, state_ssd, state_ssd_conv, state_lru, state_lru_conv, ffn1_norm, ffn1_w_gate, ffn1_w_up, ffn1_w_down, mix_norm, w_in, fox_b_f, ssd_conv_w, ssd_conv_b, ssd_dt_bias, ssd_a_log, ssd_d, ssd_norm, lru_conv_w, lru_conv_b, lru_w_a, lru_b_a, lru_w_x, lru_b_x, lru_lambda, w_out, ffn2_norm, ffn2_w_gate, ffn2_w_up, ffn2_w_down, final_norm):
    bp, seq, _ = x_prompt.shape
    bd, t_len, _ = x_sample.shape
    depth = w_in.shape[0]
    n_pool = cache_k.shape[1]
    assert seq % 512 == 0 and t_len <= SAMPLE_ROWS and t_len >= CONV_W - 1
    assert page_table.shape[1] % PAGES_PER_STEP == 0
    tp = bp * seq
    ts = bd * SAMPLE_ROWS
    t = tp + ts
    tm_ffn = _pick_tile(t, 768)
    tm_proj = _pick_tile(t, 1056)
    tq = 256 if seq % 256 == 0 else CHUNK

    xs_pad = jnp.pad(x_sample, ((0, 0), (0, SAMPLE_ROWS - t_len), (0, 0)))
    x = jnp.concatenate([x_prompt.reshape(tp, D_MODEL), xs_pad.reshape(ts, D_MODEL)], axis=0)

    ck = jnp.transpose(cache_k, (0, 1, 3, 4, 2)).reshape(depth, n_pool, D_ATT, PAGE)
    cv = jnp.transpose(cache_v, (0, 1, 3, 4, 2)).reshape(depth, n_pool, D_ATT, PAGE)
    clf = jnp.transpose(cache_logf, (0, 1, 3, 2))
    e_mat = (jnp.arange(LANES)[:, None] == (jnp.arange(D_SSD)[None, :] // HEAD_DIM)).astype(BF16)

    rows3 = lambda v: v.reshape(depth, 1, -1).astype(F32)
    lanes3 = lambda v: jnp.pad(rows3(v), ((0, 0), (0, 0), (0, LANES - v.shape[-1])))
    bf16 = lambda w: w.astype(BF16)
    ffn1 = (rows3(ffn1_norm), bf16(ffn1_w_gate), bf16(ffn1_w_up), bf16(ffn1_w_down))
    ffn2 = (rows3(ffn2_norm), bf16(ffn2_w_gate), bf16(ffn2_w_up), bf16(ffn2_w_down))
    o_f, o_z, o_xbc = 3 * D_ATT, 3 * D_ATT + H_ATT, 3 * D_ATT + H_ATT + D_SSD
    o_dt = o_xbc + D_SSD + D_BC
    o_g = o_dt + H_SSD
    lane_pad = lambda w: jnp.pad(w, ((0, 0), (0, 0), (0, LANES - w.shape[-1])))
    w_in_p = jnp.concatenate([
        w_in[:, :, o_z:o_z + D_SSD], w_in[:, :, o_xbc:o_xbc + D_SSD + D_BC], w_in[:, :, 0:3 * D_ATT],
        w_in[:, :, o_g:o_g + 2 * D_LRU],
        lane_pad(w_in[:, :, o_f:o_f + H_ATT]), lane_pad(w_in[:, :, o_dt:o_dt + H_SSD]),
    ], axis=2).astype(BF16)
    mix_g = rows3(mix_norm)
    w_out_b = w_out.astype(BF16)
    bf_pad = lanes3(fox_b_f)
    ssd_p = dict(
        cwx=ssd_conv_w[:, :, :D_SSD], cbx=rows3(ssd_conv_b[:, :D_SSD]),
        cwbc=ssd_conv_w[:, :, D_SSD:], cbbc=rows3(ssd_conv_b[:, D_SSD:]),
        dtb=lanes3(ssd_dt_bias), alog=lanes3(ssd_a_log),
        dexp=rows3(jnp.repeat(ssd_d, HEAD_DIM, axis=1)), nw=rows3(ssd_norm), e=e_mat, et=e_mat.T)
    lru_p = dict(
        cw=lru_conv_w, cb=rows3(lru_conv_b),
        wa=_block_diag(lru_w_a).astype(BF16), ba=rows3(lru_b_a),
        wx=_block_diag(lru_w_x).astype(BF16), bx=rows3(lru_b_x), lam=rows3(lru_lambda))
    st_ssd = state_ssd.reshape(depth, bd, D_SSD, SSD_STATE)
    hist_ssd = _hist_tile(state_ssd_conv)
    hist_lru = _hist_tile(state_lru_conv)
    st_lru = state_lru.reshape(depth, bd, 1, D_LRU)
    zeros_ssd = jnp.zeros((1, bp, D_SSD, SSD_STATE), F32)
    zeros_hist_ssd = jnp.zeros((1, bp, SUBLANES, D_SSD + D_BC), F32)
    zeros_hist_lru = jnp.zeros((1, bp, SUBLANES, D_LRU), F32)
    zeros_lru = jnp.zeros((1, bp, 1, D_LRU), F32)
    n_chunks = seq // CHUNK

    outs_p = [[] for _ in range(7)]
    outs_s = [[] for _ in range(7)]
    projs = []
    for l in range(depth):
        last = l == depth - 1
        x = _ffn(x, *ffn1, None, l, tm_ffn)
        proj = _inproj(x, mix_g, w_in_p, l, tm_proj)

        logf_p, c_p, ct_p = _fox_prep(proj, bf_pad, l, bp, seq)
        att_p = _fox_prompt(proj, c_p, ct_p, bp, seq, tq)
        ssd_y_p, ssd_st_p = _ssd(proj, zeros_hist_ssd, zeros_ssd, 0, ssd_p, l, 0, bp, n_chunks, CHUNK, CHUNK)
        lru_y_p, lru_st_p = _lru(proj, zeros_hist_lru, zeros_lru, 0, lru_p, l, 0, bp, seq // LRU_ROWS, LRU_ROWS,
                                 LRU_ROWS)

        att_s, logf_s = _fox_sample(proj, bf_pad, ck, cv, clf, page_table, l, tp, bd, t_len)
        ssd_y_s, ssd_st_s = _ssd(proj, hist_ssd, st_ssd, l, ssd_p, l, tp, bd, 1, SAMPLE_ROWS, t_len)
        lru_y_s, lru_st_s = _lru_sample(proj, hist_lru, st_lru, lru_p, l, tp, bd, t_len)

        x = _outproj(x, (att_p, ssd_y_p, lru_y_p), (att_s, ssd_y_s, lru_y_s), w_out_b, l, tq, seq // tq)
        x = _ffn(x, *ffn2, final_norm.reshape(1, D_MODEL) if last else None, l, tm_ffn)

        def cols_p(c0, n, r0=0):
            if r0:
                return jnp.stack([lax.slice(proj, (b * seq + r0, c0), ((b + 1) * seq, c0 + n)) for b in range(bp)])
            return lax.slice(proj, (0, c0), (tp, c0 + n)).reshape(bp, seq, n)

        def cols_s(c0, n, r0, r1):
            return lax.slice(proj, (tp, c0), (t, c0 + n)).reshape(bd, SAMPLE_ROWS, n)[:, r0:r1]

        tail = seq - (CONV_W - 1)
        projs.append(proj)
        outs_p[2].append(logf_p[:, :H_ATT].reshape(bp, seq, H_ATT))
        outs_p[3].append(ssd_st_p.reshape(bp, H_SSD, HEAD_DIM, SSD_STATE))
        outs_p[4].append(cols_p(C_XS, D_SSD + D_BC, tail))
        outs_p[5].append(lru_st_p.reshape(bp, D_LRU))
        outs_p[6].append(cols_p(C_U, D_LRU, tail))
        outs_s[0].append(cols_s(C_K, D_ATT, 0, t_len).reshape(bd, t_len, H_ATT, HEAD_DIM))
        outs_s[1].append(cols_s(C_V, D_ATT, 0, t_len).reshape(bd, t_len, H_ATT, HEAD_DIM))
        outs_s[2].append(logf_s.reshape(bd, SAMPLE_ROWS, LANES)[:, :t_len, :H_ATT])
        outs_s[3].append(ssd_st_s.reshape(bd, H_SSD, HEAD_DIM, SSD_STATE))
        outs_s[4].append(cols_s(C_XS, D_SSD + D_BC, t_len - (CONV_W - 1), t_len))
        outs_s[5].append(lru_st_s.reshape(bd, D_LRU))
        outs_s[6].append(cols_s(C_U, D_LRU, t_len - (CONV_W - 1), t_len))

    y_prompt = x[:tp].reshape(bp, seq, D_MODEL)
    y_sample = x[tp:].reshape(bd, SAMPLE_ROWS, D_MODEL)[:, :t_len]
    k_t, v_t = _kv_transpose(projs, bp, seq, 2 * tq)
    heads_last = lambda a: jnp.transpose(a.reshape(depth, bp, H_ATT, HEAD_DIM, seq), (0, 1, 4, 2, 3))
    sp = [jnp.stack(o) for o in outs_p[2:]]
    ss = [jnp.stack(o) for o in outs_s]
    return (y_prompt, y_sample, heads_last(k_t), heads_last(v_t), sp[0], ss[0], ss[1], ss[2], sp[1], sp[2],
            ss[3], ss[4], sp[3], sp[4], ss[5], ss[6])
```

```python
import functools
import math

import numpy as np
import jax
import jax.numpy as jnp
from jax import lax
from jax.experimental import pallas as pl
from jax.experimental.pallas import tpu as pltpu

F32 = jnp.float32
BF16 = jnp.bfloat16

D_MODEL = 2048
HEAD_DIM = 64
D_ATT = 512
D_SSD = 1024
D_LRU = 512
H_ATT = 8
H_SSD = 16
SSD_GROUPS = 2
SSD_STATE = 128
D_BC = 2 * SSD_GROUPS * SSD_STATE
CONV_W = 4
LRU_BLOCKS = 8
LRU_BW = 64
LRU_C = 8.0
D_FF = 5504
PAGE = 128
RMS_EPS = 1e-6

LANES = 128
SUBLANES = 8
VMEM_LIMIT = 56 * 1024 * 1024

CHUNK = 128
LRU_ROWS = 256
SSD_MIN_ROWS = 32
SAMPLE_ROWS = 8
FF_TILE = 512
PAGES_PER_STEP = 32

C_Z = 0
C_XS = 1024
C_BC = 2048
C_Q = 2560
C_K = 3072
C_V = 3584
C_G = 4096
C_U = 4608
C_F = 5120
C_DT = 5248
N_PROJ = 5376
PROJ_TILE = 1792


def _cparams(sem):
    return pltpu.CompilerParams(dimension_semantics=sem, vmem_limit_bytes=VMEM_LIMIT)


def _rms(x, g):
    return x * lax.rsqrt(jnp.mean(x * x, axis=-1, keepdims=True) + RMS_EPS) * g


def _silu(x):
    return x * (0.5 * jnp.tanh(0.5 * x) + 0.5)


def _softplus(x):
    return jnp.maximum(x, 0.0) + jnp.log1p(jnp.exp(-jnp.abs(x)))


def _split3(a):
    a1 = a.astype(BF16)
    r1 = a - a1.astype(F32)
    a2 = r1.astype(BF16)
    r2 = r1 - a2.astype(F32)
    return a1, a2, r2.astype(BF16)


def _dot_sel_r(a, sel):
    a1, a2, a3 = _split3(a)
    d = lambda p: jnp.dot(p, sel, preferred_element_type=F32)
    return d(a1) + d(a2) + d(a3)


def _dot_sel_l(sel, a):
    a1, a2, a3 = _split3(a)
    d = lambda p: jnp.dot(sel, p, preferred_element_type=F32)
    return d(a1) + d(a2) + d(a3)


def _dot_nt(a, b):
    return lax.dot_general(a, b, (((1,), (1,)), ((), ())), preferred_element_type=F32)


def _tile_dtype(rows):
    return BF16 if rows % (2 * SUBLANES) == 0 else F32


def _iota(shape, dim):
    return lax.broadcasted_iota(jnp.int32, shape, dim)


def _tril_bf16(n):
    return (_iota((n, n), 0) >= _iota((n, n), 1)).astype(BF16)


def _ffn_kernel(*refs, n_main, n_tail, final):
    x_ref, g_ref = refs[0:2]
    main = refs[2:5]
    tails = [refs[5 + 3 * k:8 + 3 * k] for k in range(n_tail)]
    rest = refs[5 + 3 * n_tail:]
    if final:
        fg_ref, o_ref, xn_ref = rest
    else:
        o_ref, xn_ref = rest
    j = pl.program_id(1)

    def part(wg_ref, wu_ref, wd_ref):
        xn = xn_ref[...]
        a = jnp.dot(xn, wg_ref[...], preferred_element_type=F32)
        b = jnp.dot(xn, wu_ref[...], preferred_element_type=F32)
        h = (_silu(a) * b).astype(BF16)
        return jnp.dot(h, wd_ref[...], preferred_element_type=F32)

    @pl.when(j == 0)
    def _():
        xn_ref[...] = _rms(x_ref[...], g_ref[...]).astype(BF16)
        acc = jnp.zeros(o_ref.shape, F32)
        for tail in tails:
            acc = acc + part(*tail)
        o_ref[...] = acc

    o_ref[...] += part(*main)

    @pl.when(j == n_main - 1)
    def _():
        y = x_ref[...] + 0.5 * o_ref[...]
        if final:
            y = _rms(y, fg_ref[...])
        o_ref[...] = y


def _ffn_tail_tiles():
    tiles, start, width = [], (D_FF // FF_TILE) * FF_TILE, FF_TILE // 2
    while start < D_FF:
        if start + width <= D_FF:
            assert width % LANES == 0 and start % width == 0
            tiles.append((width, start // width))
            start += width
        width //= 2
    return tiles


def _ffn(x, g, wg, wu, wd, final_g, layer, tm):
    t = x.shape[0]
    n_main = D_FF // FF_TILE
    tails = _ffn_tail_tiles()
    final = final_g is not None

    def weight_specs(width, col):
        return [pl.BlockSpec((None, D_MODEL, width), lambda i, j: (layer, 0, col(j))),
                pl.BlockSpec((None, D_MODEL, width), lambda i, j: (layer, 0, col(j))),
                pl.BlockSpec((None, width, D_MODEL), lambda i, j: (layer, col(j), 0))]

    in_specs = [
        pl.BlockSpec((tm, D_MODEL), lambda i, j: (i, 0)),
        pl.BlockSpec((None, 1, D_MODEL), lambda i, j: (layer, 0, 0)),
    ] + weight_specs(FF_TILE, lambda j: j)
    args = [x, g, wg, wu, wd]
    for width, blk in tails:
        in_specs += weight_specs(width, lambda j, blk=blk: blk)
        args += [wg, wu, wd]
    if final:
        in_specs.append(pl.BlockSpec((1, D_MODEL), lambda i, j: (0, 0)))
        args.append(final_g)
    return pl.pallas_call(
        functools.partial(_ffn_kernel, n_main=n_main, n_tail=len(tails), final=final),
        grid=(t // tm, n_main),
        in_specs=in_specs,
        out_specs=pl.BlockSpec((tm, D_MODEL), lambda i, j: (i, 0)),
        out_shape=jax.ShapeDtypeStruct((t, D_MODEL), F32),
        scratch_shapes=[pltpu.VMEM((tm, D_MODEL), BF16)],
        compiler_params=_cparams(("parallel", "arbitrary")),
        name="ffn",
    )(*args)


def _inproj_kernel(x_ref, g_ref, w_ref, o_ref, xn_ref):
    @pl.when(pl.program_id(1) == 0)
    def _():
        xn_ref[...] = _rms(x_ref[...], g_ref[...]).astype(BF16)

    o_ref[...] = jnp.dot(xn_ref[...], w_ref[...], preferred_element_type=F32)


def _inproj(x, g, w, layer, tm):
    t = x.shape[0]
    return pl.pallas_call(
        _inproj_kernel,
        grid=(t // tm, N_PROJ // PROJ_TILE),
        in_specs=[
            pl.BlockSpec((tm, D_MODEL), lambda i, j: (i, 0)),
            pl.BlockSpec((None, 1, D_MODEL), lambda i, j: (layer, 0, 0)),
            pl.BlockSpec((None, D_MODEL, PROJ_TILE), lambda i, j: (layer, 0, j)),
        ],
        out_specs=pl.BlockSpec((tm, PROJ_TILE), lambda i, j: (i, j)),
        out_shape=jax.ShapeDtypeStruct((t, N_PROJ), F32),
        scratch_shapes=[pltpu.VMEM((tm, D_MODEL), BF16)],
        compiler_params=_cparams(("parallel", "arbitrary")),
        name="inproj",
    )(x, g, w)


def _outproj_kernel(x_ref, att_p_ref, ssd_p_ref, lru_p_ref, att_s_ref, ssd_s_ref, lru_s_ref, w_ref, o_ref, *, n_p):
    def mix(att, ssd, lru):
        acc = jnp.dot(att, w_ref[0:D_ATT, :], preferred_element_type=F32)
        acc += jnp.dot(ssd, w_ref[D_ATT:D_ATT + D_SSD, :], preferred_element_type=F32)
        acc += jnp.dot(lru, w_ref[D_ATT + D_SSD:, :], preferred_element_type=F32)
        o_ref[...] = x_ref[...] + acc

    is_prompt = pl.program_id(0) < n_p

    @pl.when(is_prompt)
    def _():
        mix(att_p_ref[...], ssd_p_ref[...], lru_p_ref[...])

    @pl.when(jnp.logical_not(is_prompt))
    def _():
        mix(att_s_ref[...].astype(BF16), ssd_s_ref[...].astype(BF16), lru_s_ref[...].astype(BF16))


def _outproj(x, mixed_p, mixed_s, w, layer, tm, nq):
    t = x.shape[0]
    att_p, ssd_p, lru_p = mixed_p
    att_s, ssd_s, lru_s = mixed_s
    n_p = ssd_p.shape[0] // tm
    assert ssd_p.shape[0] % tm == 0 and ssd_s.shape[0] % tm == 0 and att_p.shape[1] * 2 == ssd_p.shape[0]
    p_blk = lambda i: jnp.minimum(i, n_p - 1)
    s_blk = lambda i: jnp.maximum(i - n_p, 0)
    return pl.pallas_call(
        functools.partial(_outproj_kernel, n_p=n_p),
        grid=(t // tm,),
        in_specs=[
            pl.BlockSpec((tm, D_MODEL), lambda i: (i, 0)),
            pl.BlockSpec((None, tm, D_ATT), lambda i: (*_att_block(p_blk(i), nq), 0)),
            pl.BlockSpec((tm, D_SSD), lambda i: (p_blk(i), 0)),
            pl.BlockSpec((tm, D_LRU), lambda i: (p_blk(i), 0)),
            pl.BlockSpec((tm, D_ATT), lambda i: (s_blk(i), 0)),
            pl.BlockSpec((tm, D_SSD), lambda i: (s_blk(i), 0)),
            pl.BlockSpec((tm, D_LRU), lambda i: (s_blk(i), 0)),
            pl.BlockSpec((None, D_MODEL, D_MODEL), lambda i: (layer, 0, 0)),
        ],
        out_specs=pl.BlockSpec((tm, D_MODEL), lambda i: (i, 0)),
        out_shape=jax.ShapeDtypeStruct((t, D_MODEL), F32),
        compiler_params=_cparams(("parallel",)),
        name="outproj",
    )(x, att_p, ssd_p, lru_p, att_s, ssd_s, lru_s, w)


def _fox_prep_kernel(f_ref, bf_ref, logf_ref, c_ref, ct_ref, *, n_blk):
    tril = _tril_bf16(CHUNK)
    carry = jnp.zeros((1, LANES), F32)
    for i in range(n_blk):
        rows = slice(i * CHUNK, (i + 1) * CHUNK)
        lf = -_softplus(-(f_ref[rows, :] + bf_ref[...]))
        logf_ref[rows, :] = lf
        cs = _dot_sel_l(tril, lf) + carry
        carry = cs[CHUNK - 1:CHUNK, :]
        c_ref[rows, :] = cs
        ct_ref[0, :, rows] = cs.T[0:SUBLANES, :]


def _fox_prep(proj, bf_pad, layer, n_seq, seq):
    n_blk = seq // CHUNK
    return pl.pallas_call(
        functools.partial(_fox_prep_kernel, n_blk=n_blk),
        grid=(n_seq,),
        in_specs=[
            pl.BlockSpec((seq, LANES), lambda b: (b, C_F // LANES)),
            pl.BlockSpec((None, 1, LANES), lambda b: (layer, 0, 0)),
        ],
        out_specs=[
            pl.BlockSpec((seq, LANES), lambda b: (b, 0)),
            pl.BlockSpec((seq, LANES), lambda b: (b, 0)),
            pl.BlockSpec((1, SUBLANES, seq), lambda b: (b, 0, 0)),
        ],
        out_shape=[
            jax.ShapeDtypeStruct((n_seq * seq, LANES), F32),
            jax.ShapeDtypeStruct((n_seq * seq, LANES), F32),
            jax.ShapeDtypeStruct((n_seq, SUBLANES, seq), F32),
        ],
        compiler_params=_cparams(("parallel",)),
        name="fox_prep",
    )(proj, bf_pad)


def _fox_prompt_kernel(qa_ref, qb_ref, k_ref, v_ref, ca_ref, cb_ref, ct_ref, o_ref, *, tq, nq):
    hp = pl.program_id(1)
    pair = pl.program_id(2)
    scale = 1.0 / math.sqrt(HEAD_DIM)
    lane = _iota((tq, LANES), 1)
    upper = lane >= HEAD_DIM
    tri = _iota((tq, tq), 0) >= _iota((tq, tq), 1)

    def attend(q, c_blk, n):
        w = n * tq
        kb = k_ref[0:w, :].astype(BF16)
        v = v_ref[0:w, :]
        upper_w = _iota((w, LANES), 1) >= HEAD_DIM
        outs = []
        for e in range(2):
            h = 2 * hp + e
            in_head = upper if e else jnp.logical_not(upper)
            qm = (jnp.where(in_head, q, 0.0) * scale).astype(BF16)
            c_col = jnp.sum(jnp.where(lane == h, c_blk, 0.0), axis=1, keepdims=True)
            c_row = ct_ref[0, pl.ds(h, 1), 0:w]
            s = _dot_nt(qm, kb) + (c_col - c_row)
            diag = jnp.where(tri, s[:, w - tq:], -1e30)
            s = diag if n == 1 else jnp.concatenate([s[:, :w - tq], diag], axis=1)
            p = jnp.exp(s - jnp.max(s, axis=1, keepdims=True)).astype(BF16)
            in_head_w = upper_w if e else jnp.logical_not(upper_w)
            acc = jnp.dot(p, jnp.where(in_head_w, v, 1.0).astype(BF16), preferred_element_type=F32)
            den = acc[:, 0:1] if e else acc[:, HEAD_DIM:HEAD_DIM + 1]
            outs.append(acc / den)
        return jnp.where(upper, outs[1], outs[0])

    for p in range(nq // 2):
        @pl.when(pair == p)
        def _(p=p):
            o_ref[0] = attend(qa_ref[...], ca_ref[...], p + 1).astype(o_ref.dtype)
            o_ref[1] = attend(qb_ref[...], cb_ref[...], nq - p).astype(o_ref.dtype)


def _fox_prompt(proj, c, ct, n_seq, seq, tq):
    nq = seq // tq
    half = nq // 2
    a_blk = lambda b, p: b * nq + p
    b_blk = lambda b, p: b * nq + (nq - 1 - p)
    return pl.pallas_call(
        functools.partial(_fox_prompt_kernel, tq=tq, nq=nq),
        grid=(n_seq, H_ATT // 2, half),
        in_specs=[
            pl.BlockSpec((tq, LANES), lambda b, hp, p: (a_blk(b, p), C_Q // LANES + hp)),
            pl.BlockSpec((tq, LANES), lambda b, hp, p: (b_blk(b, p), C_Q // LANES + hp)),
            pl.BlockSpec((seq, LANES), lambda b, hp, p: (b, C_K // LANES + hp)),
            pl.BlockSpec((seq, LANES), lambda b, hp, p: (b, C_V // LANES + hp)),
            pl.BlockSpec((tq, LANES), lambda b, hp, p: (a_blk(b, p), 0)),
            pl.BlockSpec((tq, LANES), lambda b, hp, p: (b_blk(b, p), 0)),
            pl.BlockSpec((1, SUBLANES, seq), lambda b, hp, p: (b, 0, 0)),
        ],
        out_specs=pl.BlockSpec((2, tq, LANES), lambda b, hp, p: (0, b * half + p, hp)),
        out_shape=jax.ShapeDtypeStruct((2, n_seq * half * tq, D_ATT), BF16),
        compiler_params=_cparams(("parallel", "parallel", "arbitrary")),
        name="fox_prompt",
    )(proj, proj, proj, proj, c, c, ct)


def _kv_transpose_kernel(*refs, depth):
    kt_ref, vt_ref = refs[depth:]
    for i in range(depth):
        @pl.when(pl.program_id(0) == i)
        def _(i=i):
            kv = refs[i][...]
            kt_ref[0, 0] = kv[:, :D_ATT].T
            vt_ref[0, 0] = kv[:, D_ATT:].T


def _kv_transpose(projs, n_seq, seq, tile):
    depth = len(projs)
    nj = seq // tile
    assert C_V == C_K + D_ATT and C_K % (2 * D_ATT) == 0

    def rows(i):
        return lambda l, b, j: (jnp.where(l == i, b * nj + j, 0), C_K // (2 * D_ATT))

    out = jax.ShapeDtypeStruct((depth, n_seq, D_ATT, seq), F32)
    return pl.pallas_call(
        functools.partial(_kv_transpose_kernel, depth=depth),
        grid=(depth, n_seq, nj),
        in_specs=[pl.BlockSpec((tile, 2 * D_ATT), rows(i)) for i in range(depth)],
        out_specs=[pl.BlockSpec((1, 1, D_ATT, tile), lambda l, b, j: (l, b, 0, j))] * 2,
        out_shape=[out, out],
        compiler_params=_cparams(("arbitrary", "arbitrary", "arbitrary")),
        name="kv_transpose",
    )(*projs)


def _att_block(r, nq):
    half = nq // 2
    b, qb = r // nq, r % nq
    second = qb >= half
    return jnp.where(second, 1, 0), b * half + jnp.where(second, nq - 1 - qb, qb)


def _fox_sample_kernel(pt_ref, *refs, n_steps, n_valid):
    del pt_ref
    npg = PAGES_PER_STEP
    q_ref, kn_ref, vn_ref, f_ref, bf_ref = refs[0:5]
    kp_refs = refs[5:5 + npg]
    vp_refs = refs[5 + npg:5 + 2 * npg]
    lf_refs = refs[5 + 2 * npg:5 + 3 * npg]
    o_ref, logf_ref = refs[5 + 3 * npg:7 + 3 * npg]
    qbd_ref, m_ref, l_ref, acc_ref, carry_ref, cnt_ref, pad_ref = refs[7 + 3 * npg:]
    j = pl.program_id(1)
    rows = SAMPLE_ROWS * H_ATT
    scale = 1.0 / math.sqrt(HEAD_DIM)
    row_i = _iota((rows, LANES), 0)
    lane_i = _iota((rows, LANES), 1)

    def rep_tokens(x):
        return jnp.concatenate(
            [jnp.broadcast_to(x[t:t + 1, :], (H_ATT, x.shape[1])) for t in range(SAMPLE_ROWS)], axis=0)

    def tile_heads(x):
        return jnp.concatenate([x] * SAMPLE_ROWS, axis=0)

    @pl.when(j == 0)
    def _():
        q = q_ref[...]
        head_of_lane = _iota((rows, D_ATT), 1) // HEAD_DIM
        head_of_row = _iota((rows, D_ATT), 0) % H_ATT
        qbd_ref[...] = (jnp.where(head_of_lane == head_of_row, rep_tokens(q), 0.0) * scale).astype(BF16)
        lf = -_softplus(-(f_ref[...] + bf_ref[...]))
        logf_ref[...] = lf
        ri = _iota((SAMPLE_ROWS, LANES), 0)
        cn = lf
        sh = 1
        while sh < SAMPLE_ROWS:
            cn = cn + jnp.where(ri >= sh, pltpu.roll(cn, sh, axis=0), 0.0)
            sh *= 2
        pad_ref[...] = jnp.zeros_like(pad_ref)
        pad_ref[0:SAMPLE_ROWS, :] = cn
        cnt_ref[...] = pad_ref[...].T[0:SUBLANES, :]
        m_ref[...] = jnp.full_like(m_ref, -1e30)
        l_ref[...] = jnp.zeros_like(l_ref)
        acc_ref[...] = jnp.zeros_like(acc_ref)
        carry_ref[...] = jnp.zeros_like(carry_ref)

    qbd = qbd_ref[...]

    def online(s, v_bf16, v_transposed):
        m = m_ref[...][:, 0:1]
        m_new = jnp.maximum(m, jnp.max(s, axis=1, keepdims=True))
        alpha = jnp.exp(m - m_new)
        p = jnp.exp(s - m_new)
        l_ref[...] = alpha * l_ref[...] + jnp.sum(p, axis=1, keepdims=True)
        pb = p.astype(BF16)
        pv = _dot_nt(pb, v_bf16) if v_transposed else jnp.dot(pb, v_bf16, preferred_element_type=F32)
        acc_ref[...] = alpha * acc_ref[...] + pv
        m_ref[...] = jnp.broadcast_to(m_new, m_ref.shape)

    carry = carry_ref[...][:, 0:1]
    lft = jnp.concatenate([lf_refs[p][0, 0] for p in range(npg)], axis=0)
    later = (_iota((PAGE, PAGE), 0) >= _iota((PAGE, PAGE), 1)).astype(BF16)
    incl = _dot_sel_r(lft, later)
    excl = incl - lft
    s_parts = []
    for p in range(npg):
        hs = slice(p * H_ATT, (p + 1) * H_ATT)
        suffix = excl[hs, :] + carry
        carry = carry + incl[hs, 0:1]
        kt = kp_refs[p][0, 0].astype(BF16)
        s_parts.append(jnp.dot(qbd, kt, preferred_element_type=F32) + tile_heads(suffix))
    carry_ref[...] = jnp.broadcast_to(carry, carry_ref.shape)
    s = jnp.concatenate(s_parts, axis=1)
    vt = jnp.concatenate([vp_refs[p][0, 0].astype(BF16) for p in range(npg)], axis=1)
    online(s, vt, True)

    @pl.when(j == n_steps - 1)
    def _():
        zk = jnp.zeros((PAGE - SAMPLE_ROWS, D_ATT), F32)
        kb = jnp.concatenate([kn_ref[...], zk], axis=0).astype(BF16)
        vb = jnp.concatenate([vn_ref[...], zk], axis=0).astype(BF16)
        s = _dot_nt(qbd, kb) - tile_heads(cnt_ref[...])
        ok = (lane_i <= row_i // H_ATT) & (lane_i < n_valid)
        online(jnp.where(ok, s, -1e30), vb, False)
        out = acc_ref[...] / l_ref[...][:, 0:1]
        head_of_lane = _iota((H_ATT, D_ATT), 1) // HEAD_DIM
        head_of_row = _iota((H_ATT, D_ATT), 0)
        for t in range(SAMPLE_ROWS):
            blk = out[t * H_ATT:(t + 1) * H_ATT, :]
            o_ref[t:t + 1, :] = jnp.sum(jnp.where(head_of_lane == head_of_row, blk, 0.0),
                                        axis=0, keepdims=True).astype(o_ref.dtype)


def _fox_sample(proj, bf_pad, cache_k, cache_v, cache_lf, page_table, layer, row0, n_seq, n_valid):
    npg = PAGES_PER_STEP
    n_pages = page_table.shape[1]
    n_steps = n_pages // npg
    rb0 = row0 // SAMPLE_ROWS
    rows = SAMPLE_ROWS * H_ATT

    def page_map(p):
        return lambda b, j, pt: (layer, pt[b, n_pages - 1 - (j * npg + p)], 0, 0)

    in_specs = [
        pl.BlockSpec((SAMPLE_ROWS, D_ATT), lambda b, j, pt: (rb0 + b, C_Q // D_ATT)),
        pl.BlockSpec((SAMPLE_ROWS, D_ATT), lambda b, j, pt: (rb0 + b, C_K // D_ATT)),
        pl.BlockSpec((SAMPLE_ROWS, D_ATT), lambda b, j, pt: (rb0 + b, C_V // D_ATT)),
        pl.BlockSpec((SAMPLE_ROWS, LANES), lambda b, j, pt: (rb0 + b, C_F // LANES)),
        pl.BlockSpec((None, 1, LANES), lambda b, j, pt: (layer, 0, 0)),
    ]
    in_specs += [pl.BlockSpec((1, 1, D_ATT, PAGE), page_map(p)) for p in range(npg)]
    in_specs += [pl.BlockSpec((1, 1, D_ATT, PAGE), page_map(p)) for p in range(npg)]
    in_specs += [pl.BlockSpec((1, 1, H_ATT, PAGE), page_map(p)) for p in range(npg)]
    grid_spec = pltpu.PrefetchScalarGridSpec(
        num_scalar_prefetch=1,
        grid=(n_seq, n_steps),
        in_specs=in_specs,
        out_specs=[
            pl.BlockSpec((SAMPLE_ROWS, D_ATT), lambda b, j, pt: (b, 0)),
            pl.BlockSpec((SAMPLE_ROWS, LANES), lambda b, j, pt: (b, 0)),
        ],
        scratch_shapes=[
            pltpu.VMEM((rows, D_ATT), BF16),
            pltpu.VMEM((rows, LANES), F32),
            pltpu.VMEM((rows, LANES), F32),
            pltpu.VMEM((rows, D_ATT), F32),
            pltpu.VMEM((SUBLANES, LANES), F32),
            pltpu.VMEM((SUBLANES, LANES), F32),
            pltpu.VMEM((PAGE, LANES), F32),
        ],
    )
    return pl.pallas_call(
        functools.partial(_fox_sample_kernel, n_steps=n_steps, n_valid=n_valid),
        grid_spec=grid_spec,
        out_shape=[
            jax.ShapeDtypeStruct((n_seq * SAMPLE_ROWS, D_ATT), F32),
            jax.ShapeDtypeStruct((n_seq * SAMPLE_ROWS, LANES), F32),
        ],
        compiler_params=_cparams(("parallel", "arbitrary")),
        name="fox_sample",
    )(page_table, proj, proj, proj, proj, bf_pad,
      *([cache_k] * npg), *([cache_v] * npg), *([cache_lf] * npg))


def _conv_from_ext(ext_ref, w_ref, b_ref, rows):
    out = b_ref[...] + ext_ref[SUBLANES:SUBLANES + rows, :] * w_ref[CONV_W - 1:CONV_W, :]
    for j in range(CONV_W - 1):
        off = SUBLANES - (CONV_W - 1) + j
        out = out + ext_ref[off:off + rows, :] * w_ref[j:j + 1, :]
    return out


def _ssd_kernel(z_ref, xs_ref, bc_ref, dt_ref, hx_ref, hbc_ref, h0_ref,
                cwx_ref, cbx_ref, cwbc_ref, cbbc_ref, dtb_ref, alog_ref, dexp_ref, nw_ref, e_ref, et_ref,
                y_ref, st_ref, extx_ref, extbc_ref, state_ref, *, blk_rows, n_valid, work_rows):
    c = pl.program_id(1)
    r = work_rows

    @pl.when(c == 0)
    def _():
        extx_ref[...] = jnp.zeros_like(extx_ref)
        extbc_ref[...] = jnp.zeros_like(extbc_ref)
        extx_ref[0:SUBLANES, :] = hx_ref[0]
        extbc_ref[0:SUBLANES, :] = hbc_ref[0]
        state_ref[...] = h0_ref[0, 0]

    extx_ref[SUBLANES:SUBLANES + blk_rows, :] = xs_ref[...]
    extbc_ref[SUBLANES:SUBLANES + blk_rows, :] = bc_ref[...]
    xs = _silu(_conv_from_ext(extx_ref, cwx_ref, cbx_ref, r))
    bc = _silu(_conv_from_ext(extbc_ref, cwbc_ref, cbbc_ref, r))
    if blk_rows == r:
        extx_ref[0:SUBLANES, :] = extx_ref[r:r + SUBLANES, :]
        extbc_ref[0:SUBLANES, :] = extbc_ref[r:r + SUBLANES, :]
        z = z_ref[...]
        dt_raw = dt_ref[...]
    else:
        z = jnp.concatenate([z_ref[...], jnp.zeros((r - blk_rows, D_SSD), F32)], axis=0)
        dt_raw = jnp.concatenate([dt_ref[...], jnp.zeros((r - blk_rows, LANES), F32)], axis=0)

    dt = _softplus(dt_raw + dtb_ref[...])
    if n_valid < r:
        dt = jnp.where(_iota((r, LANES), 0) < n_valid, dt, 0.0)
    a = -jnp.exp(alog_ref[...])
    acs = _dot_sel_l(_tril_bf16(r), dt * a)
    acs_t = acs.T
    e = e_ref[...]
    acs_x = _dot_sel_r(acs, e)
    dt_x = _dot_sel_r(dt, e)
    xdt = xs * dt_x
    xd = xdt * jnp.exp(acs_x[r - 1:r, :] - acs_x)
    state = state_ref[...]
    state_bf = state.astype(BF16)
    xd_t = xd.T.astype(BF16)
    xdt_bf = xdt.astype(BF16)
    tri = _iota((r, r), 0) >= _iota((r, r), 1)
    upper = _iota((r, LANES), 1) >= HEAD_DIM
    gw = D_SSD // SSD_GROUPS
    heads_per_group = H_SSD // SSD_GROUPS
    y_diag, y_off, new_state = [], [], []
    for g in range(SSD_GROUPS):
        bg = bc[:, g * SSD_STATE:(g + 1) * SSD_STATE].astype(BF16)
        cg = bc[:, D_BC // 2 + g * SSD_STATE:D_BC // 2 + (g + 1) * SSD_STATE].astype(BF16)
        cb = _dot_nt(cg, bg)
        y_off.append(_dot_nt(cg, state_bf[g * gw:(g + 1) * gw, :]))
        new_state.append(jnp.dot(xd_t[g * gw:(g + 1) * gw, :], bg, preferred_element_type=F32))
        for pi in range(heads_per_group // 2):
            lane0 = g * gw + pi * LANES
            pair = xdt_bf[:, lane0:lane0 + LANES]
            acc = None
            for hh in range(2):
                h = g * heads_per_group + 2 * pi + hh
                seg = acs[:, h:h + 1] - acs_t[h:h + 1, :]
                lm = jnp.where(tri, jnp.exp(seg), 0.0)
                mh = (cb * lm).astype(BF16)
                in_head = upper if hh else jnp.logical_not(upper)
                d = jnp.dot(mh, jnp.where(in_head, pair, jnp.zeros_like(pair)), preferred_element_type=F32)
                acc = d if acc is None else acc + d
            y_diag.append(acc)
    y_diag = jnp.concatenate(y_diag, axis=1)
    y_off = jnp.concatenate(y_off, axis=1)
    new_state = jnp.concatenate(new_state, axis=0)
    last_col = _dot_sel_l(et_ref[...], acs_t)[:, r - 1:r]
    state_new = jnp.exp(last_col) * state + new_state
    state_ref[...] = state_new
    st_ref[0, 0] = state_new

    y = y_diag + y_off * jnp.exp(acs_x) + xs * dexp_ref[...]
    yg = y * _silu(z)
    parts = []
    for g in range(SSD_GROUPS):
        p = yg[:, g * gw:(g + 1) * gw]
        parts.append(p * lax.rsqrt(jnp.mean(p * p, axis=-1, keepdims=True) + RMS_EPS))
    out = jnp.concatenate(parts, axis=1) * nw_ref[...]
    y_ref[...] = out[0:blk_rows, :].astype(y_ref.dtype)


def _ssd(proj, hist, h0, state_layer, p, layer, row0, n_seq, n_chunks, blk_rows, n_valid):
    rb0 = row0 // blk_rows
    work_rows = max(blk_rows, SSD_MIN_ROWS)
    row_map = lambda col: (lambda b, c: (rb0 + b * n_chunks + c, col))
    const = lambda b, c: (0, 0)
    lp = lambda r, n: pl.BlockSpec((None, r, n), lambda b, c: (layer, 0, 0))
    return pl.pallas_call(
        functools.partial(_ssd_kernel, blk_rows=blk_rows, n_valid=n_valid, work_rows=work_rows),
        grid=(n_seq, n_chunks),
        in_specs=[
            pl.BlockSpec((blk_rows, D_SSD), row_map(C_Z // D_SSD)),
            pl.BlockSpec((blk_rows, D_SSD), row_map(C_XS // D_SSD)),
            pl.BlockSpec((blk_rows, D_BC), row_map(C_BC // D_BC)),
            pl.BlockSpec((blk_rows, LANES), row_map(C_DT // LANES)),
            pl.BlockSpec((None, 1, SUBLANES, D_SSD), lambda b, c: (state_layer, b, 0, 0)),
            pl.BlockSpec((None, 1, SUBLANES, D_BC), lambda b, c: (state_layer, b, 0, D_SSD // D_BC)),
            pl.BlockSpec((1, 1, D_SSD, SSD_STATE), lambda b, c: (state_layer, b, 0, 0)),
            lp(CONV_W, D_SSD), lp(1, D_SSD), lp(CONV_W, D_BC), lp(1, D_BC),
            lp(1, LANES), lp(1, LANES), lp(1, D_SSD), lp(1, D_SSD),
            pl.BlockSpec((LANES, D_SSD), const),
            pl.BlockSpec((D_SSD, LANES), const),
        ],
        out_specs=[
            pl.BlockSpec((blk_rows, D_SSD), lambda b, c: (b * n_chunks + c, 0)),
            pl.BlockSpec((1, 1, D_SSD, SSD_STATE), lambda b, c: (0, b, 0, 0)),
        ],
        out_shape=[
            jax.ShapeDtypeStruct((n_seq * n_chunks * blk_rows, D_SSD), _tile_dtype(blk_rows)),
            jax.ShapeDtypeStruct((1, n_seq, D_SSD, SSD_STATE), F32),
        ],
        scratch_shapes=[
            pltpu.VMEM((SUBLANES + work_rows, D_SSD), F32),
            pltpu.VMEM((SUBLANES + work_rows, D_BC), F32),
            pltpu.VMEM((D_SSD, SSD_STATE), F32),
        ],
        compiler_params=_cparams(("parallel", "arbitrary")),
        name="ssd",
    )(proj, proj, proj, proj, hist, hist, h0,
      p["cwx"], p["cbx"], p["cwbc"], p["cbbc"], p["dtb"], p["alog"], p["dexp"], p["nw"], p["e"], p["et"])


def _lru_kernel(g_ref, u_ref, hist_ref, h0_ref, cw_ref, cb_ref, wa_ref, ba_ref, wx_ref, bx_ref, lam_ref,
                y_ref, st_ref, ext_ref, hprev_ref, *, blk_rows, n_valid):
    c = pl.program_id(1)
    r = blk_rows

    @pl.when(c == 0)
    def _():
        ext_ref[0:SUBLANES, :] = hist_ref[0]
        hprev_ref[...] = h0_ref[0]

    ext_ref[SUBLANES:SUBLANES + r, :] = u_ref[...]
    u = _conv_from_ext(ext_ref, cw_ref, cb_ref, r)
    ext_ref[0:SUBLANES, :] = ext_ref[r:r + SUBLANES, :]
    a, x = _lru_gates(u, wa_ref, ba_ref, wx_ref, bx_ref, lam_ref)
    a, x = _lru_scan(a, x, r)
    hs = x + a * hprev_ref[...]
    hprev_ref[...] = hs[r - 1:r, :]
    st_ref[0] = hs[n_valid - 1:n_valid, :]
    y_ref[...] = (hs * jax.nn.gelu(g_ref[...])).astype(y_ref.dtype)


def _lru_gates(u, wa_ref, ba_ref, wx_ref, bx_ref, lam_ref):
    ub = u.astype(BF16)
    rg = jax.nn.sigmoid(jnp.dot(ub, wa_ref[...], preferred_element_type=F32) + ba_ref[...])
    ig = jax.nn.sigmoid(jnp.dot(ub, wx_ref[...], preferred_element_type=F32) + bx_ref[...])
    log_a = -LRU_C * rg * _softplus(-lam_ref[...])
    a = jnp.exp(log_a)
    return a, jnp.sqrt(-jnp.tanh(log_a) * (a * a + 1.0)) * (ig * u)


def _lru_scan(a, x, seg):
    rows = a.shape[0]
    pos = _iota(a.shape, 0) % seg if seg < rows else _iota(a.shape, 0)
    sh = 1
    while sh < seg:
        keep = pos >= sh
        a_prev = jnp.where(keep, pltpu.roll(a, sh, axis=0), 1.0)
        x_prev = jnp.where(keep, pltpu.roll(x, sh, axis=0), 0.0)
        x = x + a * x_prev
        a = a * a_prev
        sh *= 2
    return a, x


def _lru_sample_kernel(g_ref, u_ref, hist_ref, h0_ref, cw_ref, cb_ref, wa_ref, ba_ref, wx_ref, bx_ref, lam_ref,
                       y_ref, st_ref, *, n_seq, n_valid):
    rs = SAMPLE_ROWS
    ext = jnp.concatenate([hist_ref[...], u_ref[...].reshape(n_seq, rs, D_LRU)], axis=1)
    ext = ext.reshape(n_seq * 2 * rs, D_LRU)
    conv = cb_ref[...] + ext * cw_ref[CONV_W - 1:CONV_W, :]
    for k in range(1, CONV_W):
        conv = conv + pltpu.roll(ext, k, axis=0) * cw_ref[CONV_W - 1 - k:CONV_W - k, :]
    u = conv.reshape(n_seq, 2 * rs, D_LRU)[:, rs:, :].reshape(n_seq * rs, D_LRU)
    a, x = _lru_gates(u, wa_ref, ba_ref, wx_ref, bx_ref, lam_ref)
    h0 = jnp.broadcast_to(h0_ref[...], (n_seq, rs, D_LRU)).reshape(n_seq * rs, D_LRU)
    first = _iota(a.shape, 0) % rs == 0
    x = x + jnp.where(first, a * h0, 0.0)
    _, hs = _lru_scan(a, x, rs)
    st_ref[...] = hs.reshape(n_seq, rs, D_LRU)[:, n_valid - 1:n_valid, :]
    y_ref[...] = hs * jax.nn.gelu(g_ref[...])


def _lru_sample(proj, hist, h0, p, layer, row0, n_seq, n_valid):
    rows = n_seq * SAMPLE_ROWS
    assert row0 % rows == 0
    lp = lambda r, n: pl.BlockSpec((None, r, n), lambda i: (layer, 0, 0))
    return pl.pallas_call(
        functools.partial(_lru_sample_kernel, n_seq=n_seq, n_valid=n_valid),
        grid=(1,),
        in_specs=[
            pl.BlockSpec((rows, D_LRU), lambda i: (row0 // rows, C_G // D_LRU)),
            pl.BlockSpec((rows, D_LRU), lambda i: (row0 // rows, C_U // D_LRU)),
            pl.BlockSpec((None, n_seq, SUBLANES, D_LRU), lambda i: (layer, 0, 0, 0)),
            pl.BlockSpec((None, n_seq, 1, D_LRU), lambda i: (layer, 0, 0, 0)),
            lp(CONV_W, D_LRU), lp(1, D_LRU), lp(D_LRU, D_LRU), lp(1, D_LRU), lp(D_LRU, D_LRU), lp(1, D_LRU),
            lp(1, D_LRU),
        ],
        out_specs=[
            pl.BlockSpec((rows, D_LRU), lambda i: (0, 0)),
            pl.BlockSpec((n_seq, 1, D_LRU), lambda i: (0, 0, 0)),
        ],
        out_shape=[
            jax.ShapeDtypeStruct((rows, D_LRU), F32),
            jax.ShapeDtypeStruct((n_seq, 1, D_LRU), F32),
        ],
        compiler_params=_cparams(("arbitrary",)),
        name="lru_sample",
    )(proj, proj, hist, h0, p["cw"], p["cb"], p["wa"], p["ba"], p["wx"], p["bx"], p["lam"])


def _lru(proj, hist, h0, state_layer, p, layer, row0, n_seq, n_chunks, blk_rows, n_valid):
    rb0 = row0 // blk_rows
    row_map = lambda col: (lambda b, c: (rb0 + b * n_chunks + c, col))
    lp = lambda r, n: pl.BlockSpec((None, r, n), lambda b, c: (layer, 0, 0))
    return pl.pallas_call(
        functools.partial(_lru_kernel, blk_rows=blk_rows, n_valid=n_valid),
        grid=(n_seq, n_chunks),
        in_specs=[
            pl.BlockSpec((blk_rows, D_LRU), row_map(C_G // D_LRU)),
            pl.BlockSpec((blk_rows, D_LRU), row_map(C_U // D_LRU)),
            pl.BlockSpec((None, 1, SUBLANES, D_LRU), lambda b, c: (state_layer, b, 0, 0)),
            pl.BlockSpec((None, 1, 1, D_LRU), lambda b, c: (state_layer, b, 0, 0)),
            lp(CONV_W, D_LRU), lp(1, D_LRU), lp(D_LRU, D_LRU), lp(1, D_LRU), lp(D_LRU, D_LRU), lp(1, D_LRU),
            lp(1, D_LRU),
        ],
        out_specs=[
            pl.BlockSpec((blk_rows, D_LRU), lambda b, c: (b * n_chunks + c, 0)),
            pl.BlockSpec((1, 1, D_LRU), lambda b, c: (b, 0, 0)),
        ],
        out_shape=[
            jax.ShapeDtypeStruct((n_seq * n_chunks * blk_rows, D_LRU), _tile_dtype(blk_rows)),
            jax.ShapeDtypeStruct((n_seq, 1, D_LRU), F32),
        ],
        scratch_shapes=[
            pltpu.VMEM((SUBLANES + blk_rows, D_LRU), F32),
            pltpu.VMEM((1, D_LRU), F32),
        ],
        compiler_params=_cparams(("parallel", "arbitrary")),
        name="lru",
    )(proj, proj, hist, h0, p["cw"], p["cb"], p["wa"], p["ba"], p["wx"], p["bx"], p["lam"])


def _hist_tile(h):
    return jnp.pad(h, ((0, 0), (0, 0), (SUBLANES - (CONV_W - 1), 0), (0, 0)))


def _block_diag(w):
    eye = jnp.eye(LRU_BLOCKS, dtype=w.dtype)
    return jnp.einsum("lhij,hg->lhigj", w, eye).reshape(w.shape[0], D_LRU, D_LRU)


def _pick_tile(t, cap):
    best = SUBLANES
    for tm in range(SUBLANES, cap + 1, SUBLANES):
        if t % tm == 0:
            best = tm
    return best


def kernel(x_prompt, x_sample, cache_k, cache_v, cache_logf, page_table, state_ssd, state_ssd_conv, state_lru, state_lru_conv, ffn1_norm, ffn1_w_gate, ffn1_w_up, ffn1_w_down, mix_norm, w_in, fox_b_f, ssd_conv_w, ssd_conv_b, ssd_dt_bias, ssd_a_log, ssd_d, ssd_norm, lru_conv_w, lru_conv_b, lru_w_a, lru_b_a, lru_w_x, lru_b_x, lru_lambda, w_out, ffn2_norm, ffn2_w_gate, ffn2_w_up, ffn2_w_down, final_norm):
    bp, seq, _ = x_prompt.shape
    bd, t_len, _ = x_sample.shape
    depth = w_in.shape[0]
    n_pool = cache_k.shape[1]
    assert seq % 512 == 0 and t_len <= SAMPLE_ROWS and t_len >= CONV_W - 1
    assert page_table.shape[1] % PAGES_PER_STEP == 0
    tp = bp * seq
    ts = bd * SAMPLE_ROWS
    t = tp + ts
    tm_ffn = _pick_tile(t, 768)
    tm_proj = _pick_tile(t, 1056)
    tq = 256 if seq % 256 == 0 else CHUNK

    xs_pad = jnp.pad(x_sample, ((0, 0), (0, SAMPLE_ROWS - t_len), (0, 0)))
    x = jnp.concatenate([x_prompt.reshape(tp, D_MODEL), xs_pad.reshape(ts, D_MODEL)], axis=0)

    ck = jnp.transpose(cache_k, (0, 1, 3, 4, 2)).reshape(depth, n_pool, D_ATT, PAGE)
    cv = jnp.transpose(cache_v, (0, 1, 3, 4, 2)).reshape(depth, n_pool, D_ATT, PAGE)
    clf = jnp.transpose(cache_logf, (0, 1, 3, 2))
    e_mat = (jnp.arange(LANES)[:, None] == (jnp.arange(D_SSD)[None, :] // HEAD_DIM)).astype(BF16)

    rows3 = lambda v: v.reshape(depth, 1, -1).astype(F32)
    lanes3 = lambda v: jnp.pad(rows3(v), ((0, 0), (0, 0), (0, LANES - v.shape[-1])))
    bf16 = lambda w: w.astype(BF16)
    ffn1 = (rows3(ffn1_norm), bf16(ffn1_w_gate), bf16(ffn1_w_up), bf16(ffn1_w_down))
    ffn2 = (rows3(ffn2_norm), bf16(ffn2_w_gate), bf16(ffn2_w_up), bf16(ffn2_w_down))
    o_f, o_z, o_xbc = 3 * D_ATT, 3 * D_ATT + H_ATT, 3 * D_ATT + H_ATT + D_SSD
    o_dt = o_xbc + D_SSD + D_BC
    o_g = o_dt + H_SSD
    lane_pad = lambda w: jnp.pad(w, ((0, 0), (0, 0), (0, LANES - w.shape[-1])))
    w_in_p = jnp.concatenate([
        w_in[:, :, o_z:o_z + D_SSD], w_in[:, :, o_xbc:o_xbc + D_SSD + D_BC], w_in[:, :, 0:3 * D_ATT],
        w_in[:, :, o_g:o_g + 2 * D_LRU],
        lane_pad(w_in[:, :, o_f:o_f + H_ATT]), lane_pad(w_in[:, :, o_dt:o_dt + H_SSD]),
    ], axis=2).astype(BF16)
    mix_g = rows3(mix_norm)
    w_out_b = w_out.astype(BF16)
    bf_pad = lanes3(fox_b_f)
    ssd_p = dict(
        cwx=ssd_conv_w[:, :, :D_SSD], cbx=rows3(ssd_conv_b[:, :D_SSD]),
        cwbc=ssd_conv_w[:, :, D_SSD:], cbbc=rows3(ssd_conv_b[:, D_SSD:]),
        dtb=lanes3(ssd_dt_bias), alog=lanes3(ssd_a_log),
        dexp=rows3(jnp.repeat(ssd_d, HEAD_DIM, axis=1)), nw=rows3(ssd_norm), e=e_mat, et=e_mat.T)
    lru_p = dict(
        cw=lru_conv_w, cb=rows3(lru_conv_b),
        wa=_block_diag(lru_w_a).astype(BF16), ba=rows3(lru_b_a),
        wx=_block_diag(lru_w_x).astype(BF16), bx=rows3(lru_b_x), lam=rows3(lru_lambda))
    st_ssd = state_ssd.reshape(depth, bd, D_SSD, SSD_STATE)
    hist_ssd = _hist_tile(state_ssd_conv)
    hist_lru = _hist_tile(state_lru_conv)
    st_lru = state_lru.reshape(depth, bd, 1, D_LRU)
    zeros_ssd = jnp.zeros((1, bp, D_SSD, SSD_STATE), F32)
    zeros_hist_ssd = jnp.zeros((1, bp, SUBLANES, D_SSD + D_BC), F32)
    zeros_hist_lru = jnp.zeros((1, bp, SUBLANES, D_LRU), F32)
    zeros_lru = jnp.zeros((1, bp, 1, D_LRU), F32)
    n_chunks = seq // CHUNK

    outs_p = [[] for _ in range(7)]
    outs_s = [[] for _ in range(7)]
    projs = []
    for l in range(depth):
        last = l == depth - 1
        x = _ffn(x, *ffn1, None, l, tm_ffn)
        proj = _inproj(x, mix_g, w_in_p, l, tm_proj)

        logf_p, c_p, ct_p = _fox_prep(proj, bf_pad, l, bp, seq)
        att_p = _fox_prompt(proj, c_p, ct_p, bp, seq, tq)
        ssd_y_p, ssd_st_p = _ssd(proj, zeros_hist_ssd, zeros_ssd, 0, ssd_p, l, 0, bp, n_chunks, CHUNK, CHUNK)
        lru_y_p, lru_st_p = _lru(proj, zeros_hist_lru, zeros_lru, 0, lru_p, l, 0, bp, seq // LRU_ROWS, LRU_ROWS,
                                 LRU_ROWS)

        att_s, logf_s = _fox_sample(proj, bf_pad, ck, cv, clf, page_table, l, tp, bd, t_len)
        ssd_y_s, ssd_st_s = _ssd(proj, hist_ssd, st_ssd, l, ssd_p, l, tp, bd, 1, SAMPLE_ROWS, t_len)
        lru_y_s, lru_st_s = _lru_sample(proj, hist_lru, st_lru, lru_p, l, tp, bd, t_len)

        x = _outproj(x, (att_p, ssd_y_p, lru_y_p), (att_s, ssd_y_s, lru_y_s), w_out_b, l, tq, seq // tq)
        x = _ffn(x, *ffn2, final_norm.reshape(1, D_MODEL) if last else None, l, tm_ffn)

        def cols_p(c0, n, r0=0):
            if r0:
                return jnp.stack([lax.slice(proj, (b * seq + r0, c0), ((b + 1) * seq, c0 + n)) for b in range(bp)])
            return lax.slice(proj, (0, c0), (tp, c0 + n)).reshape(bp, seq, n)

        def cols_s(c0, n, r0, r1):
            return lax.slice(proj, (tp, c0), (t, c0 + n)).reshape(bd, SAMPLE_ROWS, n)[:, r0:r1]

        tail = seq - (CONV_W - 1)
        projs.append(proj)
        outs_p[2].append(logf_p[:, :H_ATT].reshape(bp, seq, H_ATT))
        outs_p[3].append(ssd_st_p.reshape(bp, H_SSD, HEAD_DIM, SSD_STATE))
        outs_p[4].append(cols_p(C_XS, D_SSD + D_BC, tail))
        outs_p[5].append(lru_st_p.reshape(bp, D_LRU))
        outs_p[6].append(cols_p(C_U, D_LRU, tail))
        outs_s[0].append(cols_s(C_K, D_ATT, 0, t_len).reshape(bd, t_len, H_ATT, HEAD_DIM))
        outs_s[1].append(cols_s(C_V, D_ATT, 0, t_len).reshape(bd, t_len, H_ATT, HEAD_DIM))
        outs_s[2].append(logf_s.reshape(bd, SAMPLE_ROWS, LANES)[:, :t_len, :H_ATT])
        outs_s[3].append(ssd_st_s.reshape(bd, H_SSD, HEAD_DIM, SSD_STATE))
        outs_s[4].append(cols_s(C_XS, D_SSD + D_BC, t_len - (CONV_W - 1), t_len))
        outs_s[5].append(lru_st_s.reshape(bd, D_LRU))
        outs_s[6].append(cols_s(C_U, D_LRU, t_len - (CONV_W - 1), t_len))

    y_prompt = x[:tp].reshape(bp, seq, D_MODEL)
    y_sample = x[tp:].reshape(bd, SAMPLE_ROWS, D_MODEL)[:, :t_len]
    k_t, v_t = _kv_transpose(projs, bp, seq, 2 * tq)
    heads_last = lambda a: jnp.transpose(a.reshape(depth, bp, H_ATT, HEAD_DIM, seq), (0, 1, 4, 2, 3))
    sp = [jnp.stack(o) for o in outs_p[2:]]
    ss = [jnp.stack(o) for o in outs_s]
    return (y_prompt, y_sample, heads_last(k_t), heads_last(v_t), sp[0], ss[0], ss[1], ss[2], sp[1], sp[2],
            ss[3], ss[4], sp[3], sp[4], ss[5], ss[6])
```

```python
import functools
import math

import numpy as np
import jax
import jax.numpy as jnp
from jax import lax
from jax.experimental import pallas as pl
from jax.experimental.pallas import tpu as pltpu

F32 = jnp.float32
BF16 = jnp.bfloat16

D_MODEL = 2048
HEAD_DIM = 64
D_ATT = 512
D_SSD = 1024
D_LRU = 512
H_ATT = 8
H_SSD = 16
SSD_GROUPS = 2
SSD_STATE = 128
D_BC = 2 * SSD_GROUPS * SSD_STATE
CONV_W = 4
LRU_BLOCKS = 8
LRU_BW = 64
LRU_C = 8.0
D_FF = 5504
PAGE = 128
RMS_EPS = 1e-6

LANES = 128
SUBLANES = 8
VMEM_LIMIT = 56 * 1024 * 1024

CHUNK = 128
LRU_ROWS = 256
SSD_MIN_ROWS = 32
SAMPLE_ROWS = 8
FF_TILE = 512
PAGES_PER_STEP = 32

C_Z = 0
C_XS = 1024
C_BC = 2048
C_Q = 2560
C_K = 3072
C_V = 3584
C_G = 4096
C_U = 4608
C_F = 5120
C_DT = 5248
N_PROJ = 5376
PROJ_TILE = 1792


def _cparams(sem):
    return pltpu.CompilerParams(dimension_semantics=sem, vmem_limit_bytes=VMEM_LIMIT)


def _rms(x, g):
    return x * lax.rsqrt(jnp.mean(x * x, axis=-1, keepdims=True) + RMS_EPS) * g


def _silu(x):
    return x * (0.5 * jnp.tanh(0.5 * x) + 0.5)


def _softplus(x):
    return jnp.maximum(x, 0.0) + jnp.log1p(jnp.exp(-jnp.abs(x)))


def _split3(a):
    a1 = a.astype(BF16)
    r1 = a - a1.astype(F32)
    a2 = r1.astype(BF16)
    r2 = r1 - a2.astype(F32)
    return a1, a2, r2.astype(BF16)


def _dot_sel_r(a, sel):
    a1, a2, a3 = _split3(a)
    d = lambda p: jnp.dot(p, sel, preferred_element_type=F32)
    return d(a1) + d(a2) + d(a3)


def _dot_sel_l(sel, a):
    a1, a2, a3 = _split3(a)
    d = lambda p: jnp.dot(sel, p, preferred_element_type=F32)
    return d(a1) + d(a2) + d(a3)


def _dot_nt(a, b):
    return lax.dot_general(a, b, (((1,), (1,)), ((), ())), preferred_element_type=F32)


def _tile_dtype(rows):
    return BF16 if rows % (2 * SUBLANES) == 0 else F32


def _iota(shape, dim):
    return lax.broadcasted_iota(jnp.int32, shape, dim)


def _tril_bf16(n):
    return (_iota((n, n), 0) >= _iota((n, n), 1)).astype(BF16)


def _ffn_kernel(*refs, n_main, n_tail, final):
    x_ref, g_ref = refs[0:2]
    main = refs[2:5]
    tails = [refs[5 + 3 * k:8 + 3 * k] for k in range(n_tail)]
    rest = refs[5 + 3 * n_tail:]
    if final:
        fg_ref, o_ref, xn_ref = rest
    else:
        o_ref, xn_ref = rest
    j = pl.program_id(1)

    def part(wg_ref, wu_ref, wd_ref):
        xn = xn_ref[...]
        a = jnp.dot(xn, wg_ref[...], preferred_element_type=F32)
        b = jnp.dot(xn, wu_ref[...], preferred_element_type=F32)
        h = (_silu(a) * b).astype(BF16)
        return jnp.dot(h, wd_ref[...], preferred_element_type=F32)

    @pl.when(j == 0)
    def _():
        xn_ref[...] = _rms(x_ref[...], g_ref[...]).astype(BF16)
        acc = jnp.zeros(o_ref.shape, F32)
        for tail in tails:
            acc = acc + part(*tail)
        o_ref[...] = acc

    o_ref[...] += part(*main)

    @pl.when(j == n_main - 1)
    def _():
        y = x_ref[...] + 0.5 * o_ref[...]
        if final:
            y = _rms(y, fg_ref[...])
        o_ref[...] = y


def _ffn_tail_tiles():
    tiles, start, width = [], (D_FF // FF_TILE) * FF_TILE, FF_TILE // 2
    while start < D_FF:
        if start + width <= D_FF:
            assert width % LANES == 0 and start % width == 0
            tiles.append((width, start // width))
            start += width
        width //= 2
    return tiles


def _ffn(x, g, wg, wu, wd, final_g, layer, tm):
    t = x.shape[0]
    n_main = D_FF // FF_TILE
    tails = _ffn_tail_tiles()
    final = final_g is not None

    def weight_specs(width, col):
        return [pl.BlockSpec((None, D_MODEL, width), lambda i, j: (layer, 0, col(j))),
                pl.BlockSpec((None, D_MODEL, width), lambda i, j: (layer, 0, col(j))),
                pl.BlockSpec((None, width, D_MODEL), lambda i, j: (layer, col(j), 0))]

    in_specs = [
        pl.BlockSpec((tm, D_MODEL), lambda i, j: (i, 0)),
        pl.BlockSpec((None, 1, D_MODEL), lambda i, j: (layer, 0, 0)),
    ] + weight_specs(FF_TILE, lambda j: j)
    args = [x, g, wg, wu, wd]
    for width, blk in tails:
        in_specs += weight_specs(width, lambda j, blk=blk: blk)
        args += [wg, wu, wd]
    if final:
        in_specs.append(pl.BlockSpec((1, D_MODEL), lambda i, j: (0, 0)))
        args.append(final_g)
    return pl.pallas_call(
        functools.partial(_ffn_kernel, n_main=n_main, n_tail=len(tails), final=final),
        grid=(t // tm, n_main),
        in_specs=in_specs,
        out_specs=pl.BlockSpec((tm, D_MODEL), lambda i, j: (i, 0)),
        out_shape=jax.ShapeDtypeStruct((t, D_MODEL), F32),
        scratch_shapes=[pltpu.VMEM((tm, D_MODEL), BF16)],
        compiler_params=_cparams(("parallel", "arbitrary")),
        name="ffn",
    )(*args)


def _inproj_kernel(x_ref, g_ref, w_ref, o_ref, xn_ref):
    @pl.when(pl.program_id(1) == 0)
    def _():
        xn_ref[...] = _rms(x_ref[...], g_ref[...]).astype(BF16)

    o_ref[...] = jnp.dot(xn_ref[...], w_ref[...], preferred_element_type=F32)


def _inproj(x, g, w, layer, tm):
    t = x.shape[0]
    return pl.pallas_call(
        _inproj_kernel,
        grid=(t // tm, N_PROJ // PROJ_TILE),
        in_specs=[
            pl.BlockSpec((tm, D_MODEL), lambda i, j: (i, 0)),
            pl.BlockSpec((None, 1, D_MODEL), lambda i, j: (layer, 0, 0)),
            pl.BlockSpec((None, D_MODEL, PROJ_TILE), lambda i, j: (layer, 0, j)),
        ],
        out_specs=pl.BlockSpec((tm, PROJ_TILE), lambda i, j: (i, j)),
        out_shape=jax.ShapeDtypeStruct((t, N_PROJ), F32),
        scratch_shapes=[pltpu.VMEM((tm, D_MODEL), BF16)],
        compiler_params=_cparams(("parallel", "arbitrary")),
        name="inproj",
    )(x, g, w)


def _outproj_kernel(x_ref, att_p_ref, ssd_p_ref, lru_p_ref, att_s_ref, ssd_s_ref, lru_s_ref, w_ref, o_ref, *, n_p):
    def mix(att, ssd, lru):
        acc = jnp.dot(att, w_ref[0:D_ATT, :], preferred_element_type=F32)
        acc += jnp.dot(ssd, w_ref[D_ATT:D_ATT + D_SSD, :], preferred_element_type=F32)
        acc += jnp.dot(lru, w_ref[D_ATT + D_SSD:, :], preferred_element_type=F32)
        o_ref[...] = x_ref[...] + acc

    is_prompt = pl.program_id(0) < n_p

    @pl.when(is_prompt)
    def _():
        mix(att_p_ref[...], ssd_p_ref[...], lru_p_ref[...])

    @pl.when(jnp.logical_not(is_prompt))
    def _():
        mix(att_s_ref[...].astype(BF16), ssd_s_ref[...].astype(BF16), lru_s_ref[...].astype(BF16))


def _outproj(x, mixed_p, mixed_s, w, layer, tm, nq):
    t = x.shape[0]
    att_p, ssd_p, lru_p = mixed_p
    att_s, ssd_s, lru_s = mixed_s
    n_p = ssd_p.shape[0] // tm
    assert ssd_p.shape[0] % tm == 0 and ssd_s.shape[0] % tm == 0 and att_p.shape[1] * 2 == ssd_p.shape[0]
    p_blk = lambda i: jnp.minimum(i, n_p - 1)
    s_blk = lambda i: jnp.maximum(i - n_p, 0)
    return pl.pallas_call(
        functools.partial(_outproj_kernel, n_p=n_p),
        grid=(t // tm,),
        in_specs=[
            pl.BlockSpec((tm, D_MODEL), lambda i: (i, 0)),
            pl.BlockSpec((None, tm, D_ATT), lambda i: (*_att_block(p_blk(i), nq), 0)),
            pl.BlockSpec((tm, D_SSD), lambda i: (p_blk(i), 0)),
            pl.BlockSpec((tm, D_LRU), lambda i: (p_blk(i), 0)),
            pl.BlockSpec((tm, D_ATT), lambda i: (s_blk(i), 0)),
            pl.BlockSpec((tm, D_SSD), lambda i: (s_blk(i), 0)),
            pl.BlockSpec((tm, D_LRU), lambda i: (s_blk(i), 0)),
            pl.BlockSpec((None, D_MODEL, D_MODEL), lambda i: (layer, 0, 0)),
        ],
        out_specs=pl.BlockSpec((tm, D_MODEL), lambda i: (i, 0)),
        out_shape=jax.ShapeDtypeStruct((t, D_MODEL), F32),
        compiler_params=_cparams(("parallel",)),
        name="outproj",
    )(x, att_p, ssd_p, lru_p, att_s, ssd_s, lru_s, w)


def _fox_prep_kernel(f_ref, bf_ref, logf_ref, ct_ref, *, n_blk):
    tril = _tril_bf16(CHUNK)
    carry = jnp.zeros((1, LANES), F32)
    for i in range(n_blk):
        rows = slice(i * CHUNK, (i + 1) * CHUNK)
        lf = -_softplus(-(f_ref[rows, :] + bf_ref[...]))
        logf_ref[rows, :] = lf
        cs = _dot_sel_l(tril, lf) + carry
        carry = cs[CHUNK - 1:CHUNK, :]
        ct_ref[0, :, rows] = cs.T[0:SUBLANES, :]


def _fox_prep(proj, bf_pad, layer, n_seq, seq):
    n_blk = seq // CHUNK
    return pl.pallas_call(
        functools.partial(_fox_prep_kernel, n_blk=n_blk),
        grid=(n_seq,),
        in_specs=[
            pl.BlockSpec((seq, LANES), lambda b: (b, C_F // LANES)),
            pl.BlockSpec((None, 1, LANES), lambda b: (layer, 0, 0)),
        ],
        out_specs=[
            pl.BlockSpec((seq, LANES), lambda b: (b, 0)),
            pl.BlockSpec((1, SUBLANES, seq), lambda b: (b, 0, 0)),
        ],
        out_shape=[
            jax.ShapeDtypeStruct((n_seq * seq, LANES), F32),
            jax.ShapeDtypeStruct((n_seq, SUBLANES, seq), F32),
        ],
        compiler_params=_cparams(("parallel",)),
        name="fox_prep",
    )(proj, bf_pad)


def _fox_prompt_kernel(qa_ref, qb_ref, k_ref, v_ref, ct_ref, o_ref, *, tq, nq):
    hp = pl.program_id(1)
    pair = pl.program_id(2)
    scale = 1.0 / math.sqrt(HEAD_DIM)
    lane = _iota((tq, LANES), 1)
    upper = lane >= HEAD_DIM
    tri = _iota((tq, tq), 0) >= _iota((tq, tq), 1)

    def attend(q, n):
        w = n * tq
        kb = k_ref[0:w, :].astype(BF16)
        v = v_ref[0:w, :]
        upper_w = _iota((w, LANES), 1) >= HEAD_DIM
        outs = []
        for e in range(2):
            h = 2 * hp + e
            in_head = upper if e else jnp.logical_not(upper)
            qm = (jnp.where(in_head, q, 0.0) * scale).astype(BF16)
            c_row = ct_ref[0, pl.ds(h, 1), 0:w]
            s = _dot_nt(qm, kb) - c_row
            diag = jnp.where(tri, s[:, w - tq:], -1e30)
            s = diag if n == 1 else jnp.concatenate([s[:, :w - tq], diag], axis=1)
            p = jnp.exp(s - jnp.max(s, axis=1, keepdims=True)).astype(BF16)
            in_head_w = upper_w if e else jnp.logical_not(upper_w)
            acc = jnp.dot(p, jnp.where(in_head_w, v, 1.0).astype(BF16), preferred_element_type=F32)
            den = acc[:, 0:1] if e else acc[:, HEAD_DIM:HEAD_DIM + 1]
            outs.append(acc / den)
        return jnp.where(upper, outs[1], outs[0])

    for p in range(nq // 2):
        @pl.when(pair == p)
        def _(p=p):
            o_ref[0] = attend(qa_ref[...], p + 1).astype(o_ref.dtype)
            o_ref[1] = attend(qb_ref[...], nq - p).astype(o_ref.dtype)


def _fox_prompt(proj, ct, n_seq, seq, tq):
    nq = seq // tq
    half = nq // 2
    a_blk = lambda b, p: b * nq + p
    b_blk = lambda b, p: b * nq + (nq - 1 - p)
    return pl.pallas_call(
        functools.partial(_fox_prompt_kernel, tq=tq, nq=nq),
        grid=(n_seq, H_ATT // 2, half),
        in_specs=[
            pl.BlockSpec((tq, LANES), lambda b, hp, p: (a_blk(b, p), C_Q // LANES + hp)),
            pl.BlockSpec((tq, LANES), lambda b, hp, p: (b_blk(b, p), C_Q // LANES + hp)),
            pl.BlockSpec((seq, LANES), lambda b, hp, p: (b, C_K // LANES + hp)),
            pl.BlockSpec((seq, LANES), lambda b, hp, p: (b, C_V // LANES + hp)),
            pl.BlockSpec((1, SUBLANES, seq), lambda b, hp, p: (b, 0, 0)),
        ],
        out_specs=pl.BlockSpec((2, tq, LANES), lambda b, hp, p: (0, b * half + p, hp)),
        out_shape=jax.ShapeDtypeStruct((2, n_seq * half * tq, D_ATT), BF16),
        compiler_params=_cparams(("parallel", "parallel", "arbitrary")),
        name="fox_prompt",
    )(proj, proj, proj, proj, ct)


def _kv_transpose_kernel(*refs, depth):
    kt_ref, vt_ref = refs[depth:]
    for i in range(depth):
        @pl.when(pl.program_id(0) == i)
        def _(i=i):
            kv = refs[i][...]
            kt_ref[0, 0] = kv[:, :D_ATT].T
            vt_ref[0, 0] = kv[:, D_ATT:].T


def _kv_transpose(projs, n_seq, seq, tile):
    depth = len(projs)
    nj = seq // tile
    assert C_V == C_K + D_ATT and C_K % (2 * D_ATT) == 0

    def rows(i):
        return lambda l, b, j: (jnp.where(l == i, b * nj + j, 0), C_K // (2 * D_ATT))

    out = jax.ShapeDtypeStruct((depth, n_seq, D_ATT, seq), F32)
    return pl.pallas_call(
        functools.partial(_kv_transpose_kernel, depth=depth),
        grid=(depth, n_seq, nj),
        in_specs=[pl.BlockSpec((tile, 2 * D_ATT), rows(i)) for i in range(depth)],
        out_specs=[pl.BlockSpec((1, 1, D_ATT, tile), lambda l, b, j: (l, b, 0, j))] * 2,
        out_shape=[out, out],
        compiler_params=_cparams(("arbitrary", "arbitrary", "arbitrary")),
        name="kv_transpose",
    )(*projs)


def _att_block(r, nq):
    half = nq // 2
    b, qb = r // nq, r % nq
    second = qb >= half
    return jnp.where(second, 1, 0), b * half + jnp.where(second, nq - 1 - qb, qb)


def _fox_sample_kernel(pt_ref, *refs, n_steps, n_valid):
    del pt_ref
    npg = PAGES_PER_STEP
    q_ref, kn_ref, vn_ref, f_ref, bf_ref = refs[0:5]
    kp_refs = refs[5:5 + npg]
    vp_refs = refs[5 + npg:5 + 2 * npg]
    lf_refs = refs[5 + 2 * npg:5 + 3 * npg]
    o_ref, logf_ref = refs[5 + 3 * npg:7 + 3 * npg]
    qbd_ref, m_ref, l_ref, acc_ref, carry_ref, cnt_ref, pad_ref = refs[7 + 3 * npg:]
    j = pl.program_id(1)
    rows = SAMPLE_ROWS * H_ATT
    scale = 1.0 / math.sqrt(HEAD_DIM)
    row_i = _iota((rows, LANES), 0)
    lane_i = _iota((rows, LANES), 1)

    def rep_tokens(x):
        return jnp.concatenate(
            [jnp.broadcast_to(x[t:t + 1, :], (H_ATT, x.shape[1])) for t in range(SAMPLE_ROWS)], axis=0)

    def tile_heads(x):
        return jnp.concatenate([x] * SAMPLE_ROWS, axis=0)

    @pl.when(j == 0)
    def _():
        q = q_ref[...]
        head_of_lane = _iota((rows, D_ATT), 1) // HEAD_DIM
        head_of_row = _iota((rows, D_ATT), 0) % H_ATT
        qbd_ref[...] = (jnp.where(head_of_lane == head_of_row, rep_tokens(q), 0.0) * scale).astype(BF16)
        lf = -_softplus(-(f_ref[...] + bf_ref[...]))
        logf_ref[...] = lf
        ri = _iota((SAMPLE_ROWS, LANES), 0)
        cn = lf
        sh = 1
        while sh < SAMPLE_ROWS:
            cn = cn + jnp.where(ri >= sh, pltpu.roll(cn, sh, axis=0), 0.0)
            sh *= 2
        pad_ref[...] = jnp.zeros_like(pad_ref)
        pad_ref[0:SAMPLE_ROWS, :] = cn
        cnt_ref[...] = pad_ref[...].T[0:SUBLANES, :]
        m_ref[...] = jnp.full_like(m_ref, -1e30)
        l_ref[...] = jnp.zeros_like(l_ref)
        acc_ref[...] = jnp.zeros_like(acc_ref)
        carry_ref[...] = jnp.zeros_like(carry_ref)

    qbd = qbd_ref[...]

    def online(s, v_bf16, v_transposed):
        m = m_ref[...][:, 0:1]
        m_new = jnp.maximum(m, jnp.max(s, axis=1, keepdims=True))
        alpha = jnp.exp(m - m_new)
        p = jnp.exp(s - m_new)
        l_ref[...] = alpha * l_ref[...] + jnp.sum(p, axis=1, keepdims=True)
        pb = p.astype(BF16)
        pv = _dot_nt(pb, v_bf16) if v_transposed else jnp.dot(pb, v_bf16, preferred_element_type=F32)
        acc_ref[...] = alpha * acc_ref[...] + pv
        m_ref[...] = jnp.broadcast_to(m_new, m_ref.shape)

    carry = carry_ref[...][:, 0:1]
    lft = jnp.concatenate([lf_refs[p][0, 0] for p in range(npg)], axis=0)
    later = (_iota((PAGE, PAGE), 0) >= _iota((PAGE, PAGE), 1)).astype(BF16)
    incl = _dot_sel_r(lft, later)
    excl = incl - lft
    s_parts = []
    for p in range(npg):
        hs = slice(p * H_ATT, (p + 1) * H_ATT)
        suffix = excl[hs, :] + carry
        carry = carry + incl[hs, 0:1]
        kt = kp_refs[p][0, 0].astype(BF16)
        s_parts.append(jnp.dot(qbd, kt, preferred_element_type=F32) + tile_heads(suffix))
    carry_ref[...] = jnp.broadcast_to(carry, carry_ref.shape)
    s = jnp.concatenate(s_parts, axis=1)
    vt = jnp.concatenate([vp_refs[p][0, 0].astype(BF16) for p in range(npg)], axis=1)
    online(s, vt, True)

    @pl.when(j == n_steps - 1)
    def _():
        zk = jnp.zeros((PAGE - SAMPLE_ROWS, D_ATT), F32)
        kb = jnp.concatenate([kn_ref[...], zk], axis=0).astype(BF16)
        vb = jnp.concatenate([vn_ref[...], zk], axis=0).astype(BF16)
        s = _dot_nt(qbd, kb) - tile_heads(cnt_ref[...])
        ok = (lane_i <= row_i // H_ATT) & (lane_i < n_valid)
        online(jnp.where(ok, s, -1e30), vb, False)
        out = acc_ref[...] / l_ref[...][:, 0:1]
        head_of_lane = _iota((H_ATT, D_ATT), 1) // HEAD_DIM
        head_of_row = _iota((H_ATT, D_ATT), 0)
        for t in range(SAMPLE_ROWS):
            blk = out[t * H_ATT:(t + 1) * H_ATT, :]
            o_ref[t:t + 1, :] = jnp.sum(jnp.where(head_of_lane == head_of_row, blk, 0.0),
                                        axis=0, keepdims=True).astype(o_ref.dtype)


def _fox_sample(proj, bf_pad, cache_k, cache_v, cache_lf, page_table, layer, row0, n_seq, n_valid):
    npg = PAGES_PER_STEP
    n_pages = page_table.shape[1]
    n_steps = n_pages // npg
    rb0 = row0 // SAMPLE_ROWS
    rows = SAMPLE_ROWS * H_ATT

    def page_map(p):
        return lambda b, j, pt: (layer, pt[b, n_pages - 1 - (j * npg + p)], 0, 0)

    in_specs = [
        pl.BlockSpec((SAMPLE_ROWS, D_ATT), lambda b, j, pt: (rb0 + b, C_Q // D_ATT)),
        pl.BlockSpec((SAMPLE_ROWS, D_ATT), lambda b, j, pt: (rb0 + b, C_K // D_ATT)),
        pl.BlockSpec((SAMPLE_ROWS, D_ATT), lambda b, j, pt: (rb0 + b, C_V // D_ATT)),
        pl.BlockSpec((SAMPLE_ROWS, LANES), lambda b, j, pt: (rb0 + b, C_F // LANES)),
        pl.BlockSpec((None, 1, LANES), lambda b, j, pt: (layer, 0, 0)),
    ]
    in_specs += [pl.BlockSpec((1, 1, D_ATT, PAGE), page_map(p)) for p in range(npg)]
    in_specs += [pl.BlockSpec((1, 1, D_ATT, PAGE), page_map(p)) for p in range(npg)]
    in_specs += [pl.BlockSpec((1, 1, H_ATT, PAGE), page_map(p)) for p in range(npg)]
    grid_spec = pltpu.PrefetchScalarGridSpec(
        num_scalar_prefetch=1,
        grid=(n_seq, n_steps),
        in_specs=in_specs,
        out_specs=[
            pl.BlockSpec((SAMPLE_ROWS, D_ATT), lambda b, j, pt: (b, 0)),
            pl.BlockSpec((SAMPLE_ROWS, LANES), lambda b, j, pt: (b, 0)),
        ],
        scratch_shapes=[
            pltpu.VMEM((rows, D_ATT), BF16),
            pltpu.VMEM((rows, LANES), F32),
            pltpu.VMEM((rows, LANES), F32),
            pltpu.VMEM((rows, D_ATT), F32),
            pltpu.VMEM((SUBLANES, LANES), F32),
            pltpu.VMEM((SUBLANES, LANES), F32),
            pltpu.VMEM((PAGE, LANES), F32),
        ],
    )
    return pl.pallas_call(
        functools.partial(_fox_sample_kernel, n_steps=n_steps, n_valid=n_valid),
        grid_spec=grid_spec,
        out_shape=[
            jax.ShapeDtypeStruct((n_seq * SAMPLE_ROWS, D_ATT), F32),
            jax.ShapeDtypeStruct((n_seq * SAMPLE_ROWS, LANES), F32),
        ],
        compiler_params=_cparams(("parallel", "arbitrary")),
        name="fox_sample",
    )(page_table, proj, proj, proj, proj, bf_pad,
      *([cache_k] * npg), *([cache_v] * npg), *([cache_lf] * npg))


def _conv_from_ext(ext_ref, w_ref, b_ref, rows):
    out = b_ref[...] + ext_ref[SUBLANES:SUBLANES + rows, :] * w_ref[CONV_W - 1:CONV_W, :]
    for j in range(CONV_W - 1):
        off = SUBLANES - (CONV_W - 1) + j
        out = out + ext_ref[off:off + rows, :] * w_ref[j:j + 1, :]
    return out


def _ssd_kernel(z_ref, xs_ref, bc_ref, dt_ref, hx_ref, hbc_ref, h0_ref,
                cwx_ref, cbx_ref, cwbc_ref, cbbc_ref, dtb_ref, alog_ref, dexp_ref, nw_ref, e_ref, et_ref,
                y_ref, st_ref, extx_ref, extbc_ref, state_ref, *, blk_rows, n_valid, work_rows):
    c = pl.program_id(1)
    r = work_rows

    @pl.when(c == 0)
    def _():
        extx_ref[...] = jnp.zeros_like(extx_ref)
        extbc_ref[...] = jnp.zeros_like(extbc_ref)
        extx_ref[0:SUBLANES, :] = hx_ref[0]
        extbc_ref[0:SUBLANES, :] = hbc_ref[0]
        state_ref[...] = h0_ref[0, 0]

    extx_ref[SUBLANES:SUBLANES + blk_rows, :] = xs_ref[...]
    extbc_ref[SUBLANES:SUBLANES + blk_rows, :] = bc_ref[...]
    xs = _silu(_conv_from_ext(extx_ref, cwx_ref, cbx_ref, r))
    bc = _silu(_conv_from_ext(extbc_ref, cwbc_ref, cbbc_ref, r))
    if blk_rows == r:
        extx_ref[0:SUBLANES, :] = extx_ref[r:r + SUBLANES, :]
        extbc_ref[0:SUBLANES, :] = extbc_ref[r:r + SUBLANES, :]
        z = z_ref[...]
        dt_raw = dt_ref[...]
    else:
        z = jnp.concatenate([z_ref[...], jnp.zeros((r - blk_rows, D_SSD), F32)], axis=0)
        dt_raw = jnp.concatenate([dt_ref[...], jnp.zeros((r - blk_rows, LANES), F32)], axis=0)

    dt = _softplus(dt_raw + dtb_ref[...])
    if n_valid < r:
        dt = jnp.where(_iota((r, LANES), 0) < n_valid, dt, 0.0)
    a = -jnp.exp(alog_ref[...])
    acs = _dot_sel_l(_tril_bf16(r), dt * a)
    acs_t = acs.T
    e = e_ref[...]
    acs_x = _dot_sel_r(acs, e)
    dt_x = _dot_sel_r(dt, e)
    xdt = xs * dt_x
    xd = xdt * jnp.exp(acs_x[r - 1:r, :] - acs_x)
    state = state_ref[...]
    state_bf = state.astype(BF16)
    xd_t = xd.T.astype(BF16)
    xdt_bf = xdt.astype(BF16)
    tri = _iota((r, r), 0) >= _iota((r, r), 1)
    upper = _iota((r, LANES), 1) >= HEAD_DIM
    gw = D_SSD // SSD_GROUPS
    heads_per_group = H_SSD // SSD_GROUPS
    y_diag, y_off, new_state = [], [], []
    for g in range(SSD_GROUPS):
        bg = bc[:, g * SSD_STATE:(g + 1) * SSD_STATE].astype(BF16)
        cg = bc[:, D_BC // 2 + g * SSD_STATE:D_BC // 2 + (g + 1) * SSD_STATE].astype(BF16)
        cb = _dot_nt(cg, bg)
        y_off.append(_dot_nt(cg, state_bf[g * gw:(g + 1) * gw, :]))
        new_state.append(jnp.dot(xd_t[g * gw:(g + 1) * gw, :], bg, preferred_element_type=F32))
        for pi in range(heads_per_group // 2):
            lane0 = g * gw + pi * LANES
            pair = xdt_bf[:, lane0:lane0 + LANES]
            acc = None
            for hh in range(2):
                h = g * heads_per_group + 2 * pi + hh
                seg = acs[:, h:h + 1] - acs_t[h:h + 1, :]
                lm = jnp.where(tri, jnp.exp(seg), 0.0)
                mh = (cb * lm).astype(BF16)
                in_head = upper if hh else jnp.logical_not(upper)
                d = jnp.dot(mh, jnp.where(in_head, pair, jnp.zeros_like(pair)), preferred_element_type=F32)
                acc = d if acc is None else acc + d
            y_diag.append(acc)
    y_diag = jnp.concatenate(y_diag, axis=1)
    y_off = jnp.concatenate(y_off, axis=1)
    new_state = jnp.concatenate(new_state, axis=0)
    last_col = _dot_sel_l(et_ref[...], acs_t)[:, r - 1:r]
    state_new = jnp.exp(last_col) * state + new_state
    state_ref[...] = state_new
    st_ref[0, 0] = state_new

    y = y_diag + y_off * jnp.exp(acs_x) + xs * dexp_ref[...]
    yg = y * _silu(z)
    parts = []
    for g in range(SSD_GROUPS):
        p = yg[:, g * gw:(g + 1) * gw]
        parts.append(p * lax.rsqrt(jnp.mean(p * p, axis=-1, keepdims=True) + RMS_EPS))
    out = jnp.concatenate(parts, axis=1) * nw_ref[...]
    y_ref[...] = out[0:blk_rows, :].astype(y_ref.dtype)


def _ssd(proj, hist, h0, state_layer, p, layer, row0, n_seq, n_chunks, blk_rows, n_valid):
    rb0 = row0 // blk_rows
    work_rows = max(blk_rows, SSD_MIN_ROWS)
    row_map = lambda col: (lambda b, c: (rb0 + b * n_chunks + c, col))
    const = lambda b, c: (0, 0)
    lp = lambda r, n: pl.BlockSpec((None, r, n), lambda b, c: (layer, 0, 0))
    return pl.pallas_call(
        functools.partial(_ssd_kernel, blk_rows=blk_rows, n_valid=n_valid, work_rows=work_rows),
        grid=(n_seq, n_chunks),
        in_specs=[
            pl.BlockSpec((blk_rows, D_SSD), row_map(C_Z // D_SSD)),
            pl.BlockSpec((blk_rows, D_SSD), row_map(C_XS // D_SSD)),
            pl.BlockSpec((blk_rows, D_BC), row_map(C_BC // D_BC)),
            pl.BlockSpec((blk_rows, LANES), row_map(C_DT // LANES)),
            pl.BlockSpec((None, 1, SUBLANES, D_SSD), lambda b, c: (state_layer, b, 0, 0)),
            pl.BlockSpec((None, 1, SUBLANES, D_BC), lambda b, c: (state_layer, b, 0, D_SSD // D_BC)),
            pl.BlockSpec((1, 1, D_SSD, SSD_STATE), lambda b, c: (state_layer, b, 0, 0)),
            lp(CONV_W, D_SSD), lp(1, D_SSD), lp(CONV_W, D_BC), lp(1, D_BC),
            lp(1, LANES), lp(1, LANES), lp(1, D_SSD), lp(1, D_SSD),
            pl.BlockSpec((LANES, D_SSD), const),
            pl.BlockSpec((D_SSD, LANES), const),
        ],
        out_specs=[
            pl.BlockSpec((blk_rows, D_SSD), lambda b, c: (b * n_chunks + c, 0)),
            pl.BlockSpec((1, 1, D_SSD, SSD_STATE), lambda b, c: (0, b, 0, 0)),
        ],
        out_shape=[
            jax.ShapeDtypeStruct((n_seq * n_chunks * blk_rows, D_SSD), _tile_dtype(blk_rows)),
            jax.ShapeDtypeStruct((1, n_seq, D_SSD, SSD_STATE), F32),
        ],
        scratch_shapes=[
            pltpu.VMEM((SUBLANES + work_rows, D_SSD), F32),
            pltpu.VMEM((SUBLANES + work_rows, D_BC), F32),
            pltpu.VMEM((D_SSD, SSD_STATE), F32),
        ],
        compiler_params=_cparams(("parallel", "arbitrary")),
        name="ssd",
    )(proj, proj, proj, proj, hist, hist, h0,
      p["cwx"], p["cbx"], p["cwbc"], p["cbbc"], p["dtb"], p["alog"], p["dexp"], p["nw"], p["e"], p["et"])


def _lru_kernel(g_ref, u_ref, hist_ref, h0_ref, cw_ref, cb_ref, wa_ref, ba_ref, wx_ref, bx_ref, lam_ref,
                y_ref, st_ref, ext_ref, hprev_ref, *, blk_rows, n_valid):
    c = pl.program_id(1)
    r = blk_rows

    @pl.when(c == 0)
    def _():
        ext_ref[0:SUBLANES, :] = hist_ref[0]
        hprev_ref[...] = h0_ref[0]

    ext_ref[SUBLANES:SUBLANES + r, :] = u_ref[...]
    u = _conv_from_ext(ext_ref, cw_ref, cb_ref, r)
    ext_ref[0:SUBLANES, :] = ext_ref[r:r + SUBLANES, :]
    a, x = _lru_gates(u, wa_ref, ba_ref, wx_ref, bx_ref, lam_ref)
    a, x = _lru_scan(a, x, r)
    hs = x + a * hprev_ref[...]
    hprev_ref[...] = hs[r - 1:r, :]
    st_ref[0] = hs[n_valid - 1:n_valid, :]
    y_ref[...] = (hs * jax.nn.gelu(g_ref[...])).astype(y_ref.dtype)


def _lru_gates(u, wa_ref, ba_ref, wx_ref, bx_ref, lam_ref):
    ub = u.astype(BF16)
    rg = jax.nn.sigmoid(jnp.dot(ub, wa_ref[...], preferred_element_type=F32) + ba_ref[...])
    ig = jax.nn.sigmoid(jnp.dot(ub, wx_ref[...], preferred_element_type=F32) + bx_ref[...])
    log_a = -LRU_C * rg * _softplus(-lam_ref[...])
    a = jnp.exp(log_a)
    return a, jnp.sqrt(-jnp.tanh(log_a) * (a * a + 1.0)) * (ig * u)


def _lru_scan(a, x, seg):
    rows = a.shape[0]
    pos = _iota(a.shape, 0) % seg if seg < rows else _iota(a.shape, 0)
    sh = 1
    while sh < seg:
        keep = pos >= sh
        a_prev = jnp.where(keep, pltpu.roll(a, sh, axis=0), 1.0)
        x_prev = jnp.where(keep, pltpu.roll(x, sh, axis=0), 0.0)
        x = x + a * x_prev
        a = a * a_prev
        sh *= 2
    return a, x


def _lru_sample_kernel(g_ref, u_ref, hist_ref, h0_ref, cw_ref, cb_ref, wa_ref, ba_ref, wx_ref, bx_ref, lam_ref,
                       y_ref, st_ref, *, n_seq, n_valid):
    rs = SAMPLE_ROWS
    ext = jnp.concatenate([hist_ref[...], u_ref[...].reshape(n_seq, rs, D_LRU)], axis=1)
    ext = ext.reshape(n_seq * 2 * rs, D_LRU)
    conv = cb_ref[...] + ext * cw_ref[CONV_W - 1:CONV_W, :]
    for k in range(1, CONV_W):
        conv = conv + pltpu.roll(ext, k, axis=0) * cw_ref[CONV_W - 1 - k:CONV_W - k, :]
    u = conv.reshape(n_seq, 2 * rs, D_LRU)[:, rs:, :].reshape(n_seq * rs, D_LRU)
    a, x = _lru_gates(u, wa_ref, ba_ref, wx_ref, bx_ref, lam_ref)
    h0 = jnp.broadcast_to(h0_ref[...], (n_seq, rs, D_LRU)).reshape(n_seq * rs, D_LRU)
    first = _iota(a.shape, 0) % rs == 0
    x = x + jnp.where(first, a * h0, 0.0)
    _, hs = _lru_scan(a, x, rs)
    st_ref[...] = hs.reshape(n_seq, rs, D_LRU)[:, n_valid - 1:n_valid, :]
    y_ref[...] = hs * jax.nn.gelu(g_ref[...])


def _lru_sample(proj, hist, h0, p, layer, row0, n_seq, n_valid):
    rows = n_seq * SAMPLE_ROWS
    assert row0 % rows == 0
    lp = lambda r, n: pl.BlockSpec((None, r, n), lambda i: (layer, 0, 0))
    return pl.pallas_call(
        functools.partial(_lru_sample_kernel, n_seq=n_seq, n_valid=n_valid),
        grid=(1,),
        in_specs=[
            pl.BlockSpec((rows, D_LRU), lambda i: (row0 // rows, C_G // D_LRU)),
            pl.BlockSpec((rows, D_LRU), lambda i: (row0 // rows, C_U // D_LRU)),
            pl.BlockSpec((None, n_seq, SUBLANES, D_LRU), lambda i: (layer, 0, 0, 0)),
            pl.BlockSpec((None, n_seq, 1, D_LRU), lambda i: (layer, 0, 0, 0)),
            lp(CONV_W, D_LRU), lp(1, D_LRU), lp(D_LRU, D_LRU), lp(1, D_LRU), lp(D_LRU, D_LRU), lp(1, D_LRU),
            lp(1, D_LRU),
        ],
        out_specs=[
            pl.BlockSpec((rows, D_LRU), lambda i: (0, 0)),
            pl.BlockSpec((n_seq, 1, D_LRU), lambda i: (0, 0, 0)),
        ],
        out_shape=[
            jax.ShapeDtypeStruct((rows, D_LRU), F32),
            jax.ShapeDtypeStruct((n_seq, 1, D_LRU), F32),
        ],
        compiler_params=_cparams(("arbitrary",)),
        name="lru_sample",
    )(proj, proj, hist, h0, p["cw"], p["cb"], p["wa"], p["ba"], p["wx"], p["bx"], p["lam"])


def _lru(proj, hist, h0, state_layer, p, layer, row0, n_seq, n_chunks, blk_rows, n_valid):
    rb0 = row0 // blk_rows
    row_map = lambda col: (lambda b, c: (rb0 + b * n_chunks + c, col))
    lp = lambda r, n: pl.BlockSpec((None, r, n), lambda b, c: (layer, 0, 0))
    return pl.pallas_call(
        functools.partial(_lru_kernel, blk_rows=blk_rows, n_valid=n_valid),
        grid=(n_seq, n_chunks),
        in_specs=[
            pl.BlockSpec((blk_rows, D_LRU), row_map(C_G // D_LRU)),
            pl.BlockSpec((blk_rows, D_LRU), row_map(C_U // D_LRU)),
            pl.BlockSpec((None, 1, SUBLANES, D_LRU), lambda b, c: (state_layer, b, 0, 0)),
            pl.BlockSpec((None, 1, 1, D_LRU), lambda b, c: (state_layer, b, 0, 0)),
            lp(CONV_W, D_LRU), lp(1, D_LRU), lp(D_LRU, D_LRU), lp(1, D_LRU), lp(D_LRU, D_LRU), lp(1, D_LRU),
            lp(1, D_LRU),
        ],
        out_specs=[
            pl.BlockSpec((blk_rows, D_LRU), lambda b, c: (b * n_chunks + c, 0)),
            pl.BlockSpec((1, 1, D_LRU), lambda b, c: (b, 0, 0)),
        ],
        out_shape=[
            jax.ShapeDtypeStruct((n_seq * n_chunks * blk_rows, D_LRU), _tile_dtype(blk_rows)),
            jax.ShapeDtypeStruct((n_seq, 1, D_LRU), F32),
        ],
        scratch_shapes=[
            pltpu.VMEM((SUBLANES + blk_rows, D_LRU), F32),
            pltpu.VMEM((1, D_LRU), F32),
        ],
        compiler_params=_cparams(("parallel", "arbitrary")),
        name="lru",
    )(proj, proj, hist, h0, p["cw"], p["cb"], p["wa"], p["ba"], p["wx"], p["bx"], p["lam"])


def _hist_tile(h):
    return jnp.pad(h, ((0, 0), (0, 0), (SUBLANES - (CONV_W - 1), 0), (0, 0)))


def _block_diag(w):
    eye = jnp.eye(LRU_BLOCKS, dtype=w.dtype)
    return jnp.einsum("lhij,hg->lhigj", w, eye).reshape(w.shape[0], D_LRU, D_LRU)


def _pick_tile(t, cap):
    best = SUBLANES
    for tm in range(SUBLANES, cap + 1, SUBLANES):
        if t % tm == 0:
            best = tm
    return best


def kernel(x_prompt, x_sample, cache_k, cache_v, cache_logf, page_table, state_ssd, state_ssd_conv, state_lru, state_lru_conv, ffn1_norm, ffn1_w_gate, ffn1_w_up, ffn1_w_down, mix_norm, w_in, fox_b_f, ssd_conv_w, ssd_conv_b, ssd_dt_bias, ssd_a_log, ssd_d, ssd_norm, lru_conv_w, lru_conv_b, lru_w_a, lru_b_a, lru_w_x, lru_b_x, lru_lambda, w_out, ffn2_norm, ffn2_w_gate, ffn2_w_up, ffn2_w_down, final_norm):
    bp, seq, _ = x_prompt.shape
    bd, t_len, _ = x_sample.shape
    depth = w_in.shape[0]
    n_pool = cache_k.shape[1]
    assert seq % 512 == 0 and t_len <= SAMPLE_ROWS and t_len >= CONV_W - 1
    assert page_table.shape[1] % PAGES_PER_STEP == 0
    tp = bp * seq
    ts = bd * SAMPLE_ROWS
    t = tp + ts
    tm_ffn = _pick_tile(t, 768)
    tm_proj = _pick_tile(t, 1056)
    tq = 256 if seq % 256 == 0 else CHUNK

    xs_pad = jnp.pad(x_sample, ((0, 0), (0, SAMPLE_ROWS - t_len), (0, 0)))
    x = jnp.concatenate([x_prompt.reshape(tp, D_MODEL), xs_pad.reshape(ts, D_MODEL)], axis=0)

    ck = jnp.transpose(cache_k, (0, 1, 3, 4, 2)).reshape(depth, n_pool, D_ATT, PAGE)
    cv = jnp.transpose(cache_v, (0, 1, 3, 4, 2)).reshape(depth, n_pool, D_ATT, PAGE)
    clf = jnp.transpose(cache_logf, (0, 1, 3, 2))
    e_mat = (jnp.arange(LANES)[:, None] == (jnp.arange(D_SSD)[None, :] // HEAD_DIM)).astype(BF16)

    rows3 = lambda v: v.reshape(depth, 1, -1).astype(F32)
    lanes3 = lambda v: jnp.pad(rows3(v), ((0, 0), (0, 0), (0, LANES - v.shape[-1])))
    bf16 = lambda w: w.astype(BF16)
    ffn1 = (rows3(ffn1_norm), bf16(ffn1_w_gate), bf16(ffn1_w_up), bf16(ffn1_w_down))
    ffn2 = (rows3(ffn2_norm), bf16(ffn2_w_gate), bf16(ffn2_w_up), bf16(ffn2_w_down))
    o_f, o_z, o_xbc = 3 * D_ATT, 3 * D_ATT + H_ATT, 3 * D_ATT + H_ATT + D_SSD
    o_dt = o_xbc + D_SSD + D_BC
    o_g = o_dt + H_SSD
    lane_pad = lambda w: jnp.pad(w, ((0, 0), (0, 0), (0, LANES - w.shape[-1])))
    w_in_p = jnp.concatenate([
        w_in[:, :, o_z:o_z + D_SSD], w_in[:, :, o_xbc:o_xbc + D_SSD + D_BC], w_in[:, :, 0:3 * D_ATT],
        w_in[:, :, o_g:o_g + 2 * D_LRU],
        lane_pad(w_in[:, :, o_f:o_f + H_ATT]), lane_pad(w_in[:, :, o_dt:o_dt + H_SSD]),
    ], axis=2).astype(BF16)
    mix_g = rows3(mix_norm)
    w_out_b = w_out.astype(BF16)
    bf_pad = lanes3(fox_b_f)
    ssd_p = dict(
        cwx=ssd_conv_w[:, :, :D_SSD], cbx=rows3(ssd_conv_b[:, :D_SSD]),
        cwbc=ssd_conv_w[:, :, D_SSD:], cbbc=rows3(ssd_conv_b[:, D_SSD:]),
        dtb=lanes3(ssd_dt_bias), alog=lanes3(ssd_a_log),
        dexp=rows3(jnp.repeat(ssd_d, HEAD_DIM, axis=1)), nw=rows3(ssd_norm), e=e_mat, et=e_mat.T)
    lru_p = dict(
        cw=lru_conv_w, cb=rows3(lru_conv_b),
        wa=_block_diag(lru_w_a).astype(BF16), ba=rows3(lru_b_a),
        wx=_block_diag(lru_w_x).astype(BF16), bx=rows3(lru_b_x), lam=rows3(lru_lambda))
    st_ssd = state_ssd.reshape(depth, bd, D_SSD, SSD_STATE)
    hist_ssd = _hist_tile(state_ssd_conv)
    hist_lru = _hist_tile(state_lru_conv)
    st_lru = state_lru.reshape(depth, bd, 1, D_LRU)
    zeros_ssd = jnp.zeros((1, bp, D_SSD, SSD_STATE), F32)
    zeros_hist_ssd = jnp.zeros((1, bp, SUBLANES, D_SSD + D_BC), F32)
    zeros_hist_lru = jnp.zeros((1, bp, SUBLANES, D_LRU), F32)
    zeros_lru = jnp.zeros((1, bp, 1, D_LRU), F32)
    n_chunks = seq // CHUNK

    outs_p = [[] for _ in range(7)]
    outs_s = [[] for _ in range(7)]
    projs = []
    for l in range(depth):
        last = l == depth - 1
        x = _ffn(x, *ffn1, None, l, tm_ffn)
        proj = _inproj(x, mix_g, w_in_p, l, tm_proj)

        logf_p, ct_p = _fox_prep(proj, bf_pad, l, bp, seq)
        att_p = _fox_prompt(proj, ct_p, bp, seq, tq)
        ssd_y_p, ssd_st_p = _ssd(proj, zeros_hist_ssd, zeros_ssd, 0, ssd_p, l, 0, bp, n_chunks, CHUNK, CHUNK)
        lru_y_p, lru_st_p = _lru(proj, zeros_hist_lru, zeros_lru, 0, lru_p, l, 0, bp, seq // LRU_ROWS, LRU_ROWS,
                                 LRU_ROWS)

        att_s, logf_s = _fox_sample(proj, bf_pad, ck, cv, clf, page_table, l, tp, bd, t_len)
        ssd_y_s, ssd_st_s = _ssd(proj, hist_ssd, st_ssd, l, ssd_p, l, tp, bd, 1, SAMPLE_ROWS, t_len)
        lru_y_s, lru_st_s = _lru_sample(proj, hist_lru, st_lru, lru_p, l, tp, bd, t_len)

        x = _outproj(x, (att_p, ssd_y_p, lru_y_p), (att_s, ssd_y_s, lru_y_s), w_out_b, l, tq, seq // tq)
        x = _ffn(x, *ffn2, final_norm.reshape(1, D_MODEL) if last else None, l, tm_ffn)

        def cols_p(c0, n, r0=0):
            if r0:
                return jnp.stack([lax.slice(proj, (b * seq + r0, c0), ((b + 1) * seq, c0 + n)) for b in range(bp)])
            return lax.slice(proj, (0, c0), (tp, c0 + n)).reshape(bp, seq, n)

        def cols_s(c0, n, r0, r1):
            return lax.slice(proj, (tp, c0), (t, c0 + n)).reshape(bd, SAMPLE_ROWS, n)[:, r0:r1]

        tail = seq - (CONV_W - 1)
        projs.append(proj)
        outs_p[2].append(logf_p[:, :H_ATT].reshape(bp, seq, H_ATT))
        outs_p[3].append(ssd_st_p.reshape(bp, H_SSD, HEAD_DIM, SSD_STATE))
        outs_p[4].append(cols_p(C_XS, D_SSD + D_BC, tail))
        outs_p[5].append(lru_st_p.reshape(bp, D_LRU))
        outs_p[6].append(cols_p(C_U, D_LRU, tail))
        outs_s[0].append(cols_s(C_K, D_ATT, 0, t_len).reshape(bd, t_len, H_ATT, HEAD_DIM))
        outs_s[1].append(cols_s(C_V, D_ATT, 0, t_len).reshape(bd, t_len, H_ATT, HEAD_DIM))
        outs_s[2].append(logf_s.reshape(bd, SAMPLE_ROWS, LANES)[:, :t_len, :H_ATT])
        outs_s[3].append(ssd_st_s.reshape(bd, H_SSD, HEAD_DIM, SSD_STATE))
        outs_s[4].append(cols_s(C_XS, D_SSD + D_BC, t_len - (CONV_W - 1), t_len))
        outs_s[5].append(lru_st_s.reshape(bd, D_LRU))
        outs_s[6].append(cols_s(C_U, D_LRU, t_len - (CONV_W - 1), t_len))

    y_prompt = x[:tp].reshape(bp, seq, D_MODEL)
    y_sample = x[tp:].reshape(bd, SAMPLE_ROWS, D_MODEL)[:, :t_len]
    k_t, v_t = _kv_transpose(projs, bp, seq, 2 * tq)
    heads_last = lambda a: jnp.transpose(a.reshape(depth, bp, H_ATT, HEAD_DIM, seq), (0, 1, 4, 2, 3))
    sp = [jnp.stack(o) for o in outs_p[2:]]
    ss = [jnp.stack(o) for o in outs_s]
    return (y_prompt, y_sample, heads_last(k_t), heads_last(v_t), sp[0], ss[0], ss[1], ss[2], sp[1], sp[2],
            ss[3], ss[4], sp[3], sp[4], ss[5], ss[6])
```
